```python
import math
import jax, jax.numpy as jnp
from jax import lax
import numpy as np

D_MODEL = 2048
BATCH = 4
SEQ = 4096
DEPTH = 1

D_MIX = D_MODEL
D_SSM = D_MIX // 2
SSM_HEAD_DIM = 64
SSM_HEADS = D_SSM // SSM_HEAD_DIM
SSM_GROUPS = 2
D_STATE = 128
D_CONV = 5
CONV_PAD = D_CONV // 2
CONV_DIM = D_SSM + 2 * SSM_GROUPS * D_STATE
CHUNK = 128
MLA_HEADS = 8
QK_NOPE = 128
QK_ROPE = 64
V_DIM = 128
Q_LORA = 768
KV_LORA = 512
ROPE_THETA = 10000.0
Q_BLOCK = 128
IN_PROJ_DIM = D_SSM + CONV_DIM + 2 * SSM_HEADS + Q_LORA + KV_LORA + QK_ROPE
N_EXPERTS = 32
TOP_K = 4
D_EXPERT = D_MODEL
SWIGLU_ALPHA = 1.702
SWIGLU_LIMIT = 7.0
EXPERT_BLOCK = 128
RMS_EPS = 1e-6

kernel_name = "hymba_ssd_mla_moe_encoder"


def rms_norm(x, w, eps=RMS_EPS):
    xf = x.astype(jnp.float32)
    y = xf * lax.rsqrt(jnp.mean(xf * xf, axis=-1, keepdims=True) + eps)
    return (y * w.astype(jnp.float32)).astype(x.dtype)


def centred_depthwise_conv(u, w, b):
    out = lax.conv_general_dilated(
        u, w[:, None, :], window_strides=(1,), padding=[(CONV_PAD, CONV_PAD)],
        dimension_numbers=("NWC", "WIO", "NWC"), feature_group_count=u.shape[-1])
    return out + b


def segsum(a):
    cs = jnp.cumsum(a, axis=-1)
    diff = cs[..., :, None] - cs[..., None, :]
    L = a.shape[-1]
    mask = jnp.tril(jnp.ones((L, L), dtype=bool))
    return jnp.where(mask, diff, -jnp.inf)


def ssd_chunked(xdt, a, bm, cm):
    bsz, s, h, p = xdt.shape
    g, n = bm.shape[2], bm.shape[3]
    r = h // g
    nc = s // CHUNK
    x_c = xdt.reshape(bsz, nc, CHUNK, g, r, p)
    b_c = bm.reshape(bsz, nc, CHUNK, g, n)
    c_c = cm.reshape(bsz, nc, CHUNK, g, n)
    a_c = jnp.transpose(a.reshape(bsz, nc, CHUNK, g, r), (0, 3, 4, 1, 2))
    a_cs = jnp.cumsum(a_c, axis=-1)
    cb = jnp.einsum("bclgn,bcsgn->bgcls", c_c, b_c)
    scores = cb[:, :, None] * jnp.exp(segsum(a_c))
    y_diag = jnp.einsum("bgrcls,bcsgrp->bclgrp", scores, x_c)
    decay_in = jnp.exp(a_cs[..., -1:] - a_cs)
    states = jnp.einsum("bcsgn,bgrcs,bcsgrp->bcgrpn", b_c, decay_in, x_c)
    chunk_decay = jnp.exp(a_cs[..., -1])

    def step(carry, inp):
        st, dec = inp
        return carry * dec[..., None, None] + st, carry

    h0 = jnp.zeros(states.shape[:1] + states.shape[2:], states.dtype)
    _, prev = lax.scan(step, h0, (jnp.moveaxis(states, 1, 0), jnp.moveaxis(chunk_decay, -1, 0)))
    prev = jnp.moveaxis(prev, 0, 1)
    y_off = jnp.einsum("bclgn,bcgrpn,bgrcl->bclgrp", c_c, prev, jnp.exp(a_cs))
    return (y_diag + y_off).reshape(bsz, s, h, p)


def ssd_mixer(z, xbc, dt_raw_f, dt_raw_b, conv_w, conv_b, dt_bias_f, dt_bias_b,
              a_log_f, a_log_b, d_skip, gate_norm_w):
    bsz, s, _ = z.shape
    xbc = jax.nn.silu(centred_depthwise_conv(xbc, conv_w, conv_b))
    xs, bm, cm = jnp.split(xbc, [D_SSM, D_SSM + SSM_GROUPS * D_STATE], axis=-1)
    xs = xs.reshape(bsz, s, SSM_HEADS, SSM_HEAD_DIM)
    bm = bm.reshape(bsz, s, SSM_GROUPS, D_STATE)
    cm = cm.reshape(bsz, s, SSM_GROUPS, D_STATE)
    dt_f = jax.nn.softplus(dt_raw_f + dt_bias_f)
    dt_b = jax.nn.softplus(dt_raw_b + dt_bias_b)
    y_f = ssd_chunked(xs * dt_f[..., None], dt_f * -jnp.exp(a_log_f), bm, cm)
    rev = lambda t: jnp.flip(t, axis=1)
    y_b = rev(ssd_chunked(rev(xs * dt_b[..., None]), rev(dt_b * -jnp.exp(a_log_b)), rev(bm), rev(cm)))
    y = y_f + y_b + xs * d_skip[:, None]
    gs = D_SSM // SSM_GROUPS
    y = (y.reshape(bsz, s, D_SSM) * jax.nn.silu(z)).reshape(bsz, s, SSM_GROUPS, gs)
    y = rms_norm(y, gate_norm_w.reshape(SSM_GROUPS, gs))
    return y.reshape(bsz, s, D_SSM)


def rope_tables(positions, dtype):
    inv_freq = ROPE_THETA ** (-jnp.arange(0, QK_ROPE, 2, dtype=jnp.float32) / QK_ROPE)
    ang = positions.astype(jnp.float32)[..., None] * inv_freq
    ang = jnp.concatenate([ang, ang], axis=-1)
    return jnp.cos(ang).astype(dtype), jnp.sin(ang).astype(dtype)


def apply_rope(t, cos, sin):
    t1, t2 = jnp.split(t, 2, axis=-1)
    return t * cos + jnp.concatenate([-t2, t1], axis=-1) * sin


def mla_mixer(q_a, kv_a, k_pe, positions, q_a_norm_w, w_q_b, kv_a_norm_w, w_kv_b, o_norm_w):
    bsz, s, _ = q_a.shape
    q = (rms_norm(q_a, q_a_norm_w) @ w_q_b).reshape(bsz, s, MLA_HEADS, QK_NOPE + QK_ROPE)
    q_nope, q_pe = q[..., :QK_NOPE], q[..., QK_NOPE:]
    kv = (rms_norm(kv_a, kv_a_norm_w) @ w_kv_b).reshape(bsz, s, MLA_HEADS, QK_NOPE + V_DIM)
    k_nope, v = kv[..., :QK_NOPE], kv[..., QK_NOPE:]
    cos, sin = rope_tables(positions, q.dtype)
    q_pe = apply_rope(q_pe, cos[:, :, None], sin[:, :, None])
    k_pe = apply_rope(k_pe, cos, sin)
    scale = (QK_NOPE + QK_ROPE) ** -0.5
    nqb = s // Q_BLOCK
    to_blocks = lambda t: jnp.moveaxis(t.reshape(bsz, nqb, Q_BLOCK, *t.shape[2:]), 1, 0)

    def attend(blk):
        qn, qr = blk
        sc = (jnp.einsum("bqhd,bkhd->bhqk", qn, k_nope)
              + jnp.einsum("bqhr,bkr->bhqk", qr, k_pe))
        pr = jax.nn.softmax(sc.astype(jnp.float32), axis=-1).astype(v.dtype)
        return jnp.einsum("bhqk,bkhd->bqhd", pr, v)

    o = lax.map(attend, (to_blocks(q_nope * scale), to_blocks(q_pe * scale)))
    o = jnp.moveaxis(o, 0, 1).reshape(bsz, s, MLA_HEADS * V_DIM)
    return rms_norm(o, o_norm_w)


def moe_ffn(xn, w_router, b_router, w_gate_up, b_gate_up, w_down, b_down):
    bsz, s, d = xn.shape
    n_tok = bsz * s
    xt = xn.reshape(n_tok, d)
    logits = (xt @ w_router).astype(jnp.float32) + b_router.astype(jnp.float32)
    top_vals, top_idx = lax.top_k(logits, TOP_K)
    gates = jax.nn.softmax(top_vals, axis=-1)
    n_asg = n_tok * TOP_K
    flat_e = top_idx.reshape(n_asg)
    flat_tok = jnp.arange(n_asg, dtype=jnp.int32) // TOP_K
    order = jnp.argsort(flat_e)
    sorted_e = flat_e[order]
    counts = jnp.bincount(flat_e, length=N_EXPERTS)
    padded = (counts + EXPERT_BLOCK - 1) // EXPERT_BLOCK * EXPERT_BLOCK
    ends = jnp.cumsum(counts)
    pends = jnp.cumsum(padded)
    dest = (pends - padded)[sorted_e] + jnp.arange(n_asg, dtype=jnp.int32) - (ends - counts)[sorted_e]
    n_blocks = (n_asg + N_EXPERTS * (EXPERT_BLOCK - 1) + EXPERT_BLOCK - 1) // EXPERT_BLOCK
    n_rows = n_blocks * EXPERT_BLOCK
    row_tok = jnp.zeros((n_rows,), jnp.int32).at[dest].set(flat_tok[order])
    row_gate = jnp.zeros((n_rows,), jnp.float32).at[dest].set(gates.reshape(n_asg)[order])
    block_e = jnp.minimum(
        jnp.searchsorted(pends, jnp.arange(n_blocks, dtype=jnp.int32) * EXPERT_BLOCK, side="right"),
        N_EXPERTS - 1)

    def expert_block(args):
        tok, e, g = args
        hdn = xt[tok] @ w_gate_up[e] + b_gate_up[e]
        glu = jnp.minimum(hdn[:, ::2], SWIGLU_LIMIT)
        lin = jnp.clip(hdn[:, 1::2], -SWIGLU_LIMIT, SWIGLU_LIMIT)
        act = glu * jax.nn.sigmoid(SWIGLU_ALPHA * glu) * (lin + 1.0)
        y = act @ w_down[e] + b_down[e]
        return y * g[:, None].astype(y.dtype)

    y = lax.map(expert_block, (row_tok.reshape(n_blocks, EXPERT_BLOCK), block_e,
                               row_gate.reshape(n_blocks, EXPERT_BLOCK)))
    out = jnp.zeros_like(xt).at[row_tok].add(y.reshape(n_rows, d))
    return out.reshape(bsz, s, d)


def setup_inputs(seed: int = 0) -> dict:
    key = jax.random.key(seed)
    ks = jax.random.split(key, 32)
    nrm = lambda k, shape, sc: jax.random.normal(k, shape, jnp.float32) * sc
    gain = lambda k, shape: 1.0 + 0.02 * jax.random.normal(k, shape, jnp.float32)

    def dt_bias(k):
        u = jax.random.uniform(k, (DEPTH, SSM_HEADS), jnp.float32)
        dt = jnp.exp(u * (math.log(0.1) - math.log(0.001)) + math.log(0.001))
        dt = jnp.maximum(dt, 1e-4)
        return dt + jnp.log(-jnp.expm1(-dt))

    a_log = lambda k: jnp.log(jax.random.uniform(k, (DEPTH, SSM_HEADS), jnp.float32, 1.0, 16.0))
    positions = (jnp.arange(SEQ, dtype=jnp.int32)[None, :]
                 + jax.random.randint(ks[1], (BATCH, 1), 0, SEQ, dtype=jnp.int32))
    return {
        "x": jax.random.normal(ks[0], (BATCH, SEQ, D_MODEL), jnp.float32),
        "positions": positions,
        "norm1_w": gain(ks[2], (DEPTH, D_MODEL)),
        "w_in": nrm(ks[3], (DEPTH, D_MODEL, IN_PROJ_DIM), D_MODEL ** -0.5),
        "conv_w": nrm(ks[4], (DEPTH, D_CONV, CONV_DIM), D_CONV ** -0.5),
        "conv_b": nrm(ks[5], (DEPTH, CONV_DIM), 0.01),
        "dt_bias_f": dt_bias(ks[6]),
        "dt_bias_b": dt_bias(ks[7]),
        "a_log_f": a_log(ks[8]),
        "a_log_b": a_log(ks[9]),
        "d_skip": 1.0 + 0.1 * jax.random.normal(ks[10], (DEPTH, SSM_HEADS), jnp.float32),
        "ssm_norm_w": gain(ks[11], (DEPTH, D_SSM)),
        "q_a_norm_w": gain(ks[12], (DEPTH, Q_LORA)),
        "w_q_b": nrm(ks[13], (DEPTH, Q_LORA, MLA_HEADS * (QK_NOPE + QK_ROPE)), Q_LORA ** -0.5),
        "kv_a_norm_w": gain(ks[14], (DEPTH, KV_LORA)),
        "w_kv_b": nrm(ks[15], (DEPTH, KV_LORA, MLA_HEADS * (QK_NOPE + V_DIM)), KV_LORA ** -0.5),
        "attn_norm_w": gain(ks[16], (DEPTH, MLA_HEADS * V_DIM)),
        "w_out": nrm(ks[17], (DEPTH, D_MIX, D_MODEL), D_MIX ** -0.5),
        "norm2_w": gain(ks[18], (DEPTH, D_MODEL)),
        "w_router": nrm(ks[19], (DEPTH, D_MODEL, N_EXPERTS), D_MODEL ** -0.5),
        "b_router": nrm(ks[20], (DEPTH, N_EXPERTS), 0.01),
        "w_gate_up": nrm(ks[21], (DEPTH, N_EXPERTS, D_MODEL, 2 * D_EXPERT), D_MODEL ** -0.5),
        "b_gate_up": nrm(ks[22], (DEPTH, N_EXPERTS, 2 * D_EXPERT), 0.01),
        "w_down": nrm(ks[23], (DEPTH, N_EXPERTS, D_EXPERT, D_MODEL), D_EXPERT ** -0.5),
        "b_down": nrm(ks[24], (DEPTH, N_EXPERTS, D_MODEL), 0.01),
        "final_norm_w": gain(ks[25], (D_MODEL,)),
    }


def reference(x, positions, norm1_w, w_in, conv_w, conv_b, dt_bias_f, dt_bias_b, a_log_f, a_log_b,
              d_skip, ssm_norm_w, q_a_norm_w, w_q_b, kv_a_norm_w, w_kv_b, attn_norm_w, w_out,
              norm2_w, w_router, b_router, w_gate_up, b_gate_up, w_down, b_down, final_norm_w):
    split_at = np.cumsum([D_SSM, CONV_DIM, SSM_HEADS, SSM_HEADS, Q_LORA, KV_LORA]).tolist()
    h = x
    for l in range(DEPTH):
        xn = rms_norm(h, norm1_w[l])
        proj = xn @ w_in[l]
        z, xbc, dt_f, dt_b, q_a, kv_a, k_pe = jnp.split(proj, split_at, axis=-1)
        y_ssm = ssd_mixer(z, xbc, dt_f, dt_b, conv_w[l], conv_b[l], dt_bias_f[l], dt_bias_b[l],
                          a_log_f[l], a_log_b[l], d_skip[l], ssm_norm_w[l])
        y_att = mla_mixer(q_a, kv_a, k_pe, positions, q_a_norm_w[l], w_q_b[l],
                          kv_a_norm_w[l], w_kv_b[l], attn_norm_w[l])
        h = h + jnp.concatenate([y_ssm, y_att], axis=-1) @ w_out[l]
        h = h + moe_ffn(rms_norm(h, norm2_w[l]), w_router[l], b_router[l], w_gate_up[l],
                        b_gate_up[l], w_down[l], b_down[l])
    return rms_norm(h, final_norm_w)
```

```python
import functools
import math

import jax
import jax.numpy as jnp
from jax import lax
from jax.experimental import pallas as pl
from jax.experimental.pallas import tpu as pltpu

F32 = jnp.float32
BF16 = jnp.bfloat16

D_MODEL = 2048
D_SSM = 1024
SSM_HEAD_DIM = 64
SSM_HEADS = 16
SSM_GROUPS = 2
D_STATE = 128
D_CONV = 5
CONV_DIM = D_SSM + 2 * SSM_GROUPS * D_STATE
CHUNK = 128
MLA_HEADS = 8
QK_NOPE = 128
QK_ROPE = 64
V_DIM = 128
Q_LORA = 768
KV_LORA = 512
ROPE_THETA = 10000.0
N_EXPERTS = 32
TOP_K = 4
D_EXPERT = 2048
SWIGLU_ALPHA = 1.702
SWIGLU_LIMIT = 7.0
RMS_EPS = 1e-6

LANES = 128
QK_PAD = 256
P_COLS = D_SSM + CONV_DIM + KV_LORA + Q_LORA
MISC_COLS = 3 * LANES
VMEM_LIMIT = 56 * 1024 * 1024

TM_IN = 512
TN_IN = 768
TM_PREP = 512
TQ = 256
TM_OUT = 256
MOE_BLK = 512
MOE_TC = 512
TD = 256


def _cparams(*sem):
    return pltpu.CompilerParams(dimension_semantics=sem, vmem_limit_bytes=VMEM_LIMIT)


def _silu(x):
    return x / (1.0 + jnp.exp(-x))


def _softplus(x):
    return jnp.maximum(x, 0.0) + jnp.log(1.0 + jnp.exp(-jnp.abs(x)))


def _split3(x):
    x1 = x.astype(BF16)
    r = x - x1.astype(F32)
    x2 = r.astype(BF16)
    x3 = (r - x2.astype(F32)).astype(BF16)
    return x1, x2, x3


def _dot(a, b):
    return jnp.dot(a, b, preferred_element_type=F32)


def _dot_nt(a, b):
    return lax.dot_general(a, b, (((1,), (1,)), ((), ())), preferred_element_type=F32)


def _dot_exact_lhs(x, m, passes):
    parts = _split3(x)[:passes]
    acc = _dot(parts[0], m)
    for p in parts[1:]:
        acc = acc + _dot(p, m)
    return acc


def _in_proj_kernel(x_ref, nw_ref, w_ref, wm_ref, p_ref, m_ref, xn_ref):
    @pl.when(pl.program_id(1) == 0)
    def _():
        x = x_ref[...]
        ms = jnp.mean(x * x, axis=-1, keepdims=True)
        xn = (x * lax.rsqrt(ms + RMS_EPS) * nw_ref[...]).astype(BF16)
        xn_ref[...] = xn
        m_ref[...] = _dot(xn, wm_ref[...])

    p_ref[...] = _dot(xn_ref[...], w_ref[...]).astype(BF16)


def _in_proj(x2, norm_w, w_main, w_misc):
    n = x2.shape[0]
    return pl.pallas_call(
        _in_proj_kernel,
        grid=(n // TM_IN, P_COLS // TN_IN),
        in_specs=[
            pl.BlockSpec((TM_IN, D_MODEL), lambda i, j: (i, 0)),
            pl.BlockSpec((1, D_MODEL), lambda i, j: (0, 0)),
            pl.BlockSpec((D_MODEL, TN_IN), lambda i, j: (0, j)),
            pl.BlockSpec((D_MODEL, MISC_COLS), lambda i, j: (0, 0)),
        ],
        out_specs=[
            pl.BlockSpec((TM_IN, TN_IN), lambda i, j: (i, j)),
            pl.BlockSpec((TM_IN, MISC_COLS), lambda i, j: (i, 0)),
        ],
        out_shape=[
            jax.ShapeDtypeStruct((n, P_COLS), BF16),
            jax.ShapeDtypeStruct((n, MISC_COLS), F32),
        ],
        scratch_shapes=[pltpu.VMEM((TM_IN, D_MODEL), BF16)],
        compiler_params=_cparams("parallel", "arbitrary"),
        name="in_proj",
    )(x2, norm_w, w_main, w_misc)


def _conv_kernel(u_ref, w_ref, b_ref, o_ref):
    u = u_ref[0].astype(F32)
    s = u.shape[0]
    row = lax.broadcasted_iota(jnp.int32, u.shape, 0)
    pad = D_CONV // 2
    acc = u * w_ref[pad:pad + 1, :] + b_ref[...]
    for k in range(D_CONV):
        off = k - pad
        if off == 0:
            continue
        shifted = pltpu.roll(u, (-off) % s, 0)
        valid = (row + off >= 0) & (row + off < s)
        acc = acc + jnp.where(valid, shifted, 0.0) * w_ref[k:k + 1, :]
    o_ref[0] = _silu(acc).astype(BF16)


def _conv(p3, conv_w, conv_b):
    b, s, _ = p3.shape
    first = D_SSM // LANES
    return pl.pallas_call(
        _conv_kernel,
        grid=(b, CONV_DIM // LANES),
        in_specs=[
            pl.BlockSpec((1, s, LANES), lambda i, c: (i, 0, first + c)),
            pl.BlockSpec((D_CONV, LANES), lambda i, c: (0, c)),
            pl.BlockSpec((1, LANES), lambda i, c: (0, c)),
        ],
        out_specs=pl.BlockSpec((1, s, LANES), lambda i, c: (i, 0, c)),
        out_shape=jax.ShapeDtypeStruct((b, s, CONV_DIM), BF16),
        compiler_params=_cparams("parallel", "parallel"),
        name="conv",
    )(p3, conv_w, conv_b)


def _ssd_direction(xc, dt_e, cs, cs_e, tot_row, state_ref, d, reverse):
    L = xc.shape[0]
    gw = D_SSM // SSM_GROUPS
    x = xc[:, :D_SSM].astype(F32)
    xdt = x * dt_e
    xdt_b = xdt.astype(BF16)
    tot_e = cs_e[tot_row:tot_row + 1, :]
    decay_out = jnp.exp(cs_e)
    xw = (xdt * jnp.exp(tot_e - cs_e)).astype(BF16)
    chunk_decay = jnp.exp(tot_e)

    cs_t = cs.T
    li = lax.broadcasted_iota(jnp.int32, (L, L), 0)
    si = lax.broadcasted_iota(jnp.int32, (L, L), 1)
    keep = (si >= li) if reverse else (si <= li)
    lane = lax.broadcasted_iota(jnp.int32, (L, LANES), 1)

    ys = []
    for g in range(SSM_GROUPS):
        bm = xc[:, D_SSM + g * D_STATE:D_SSM + (g + 1) * D_STATE]
        cm = xc[:, D_SSM + SSM_GROUPS * D_STATE + g * D_STATE:D_SSM + SSM_GROUPS * D_STATE + (g + 1) * D_STATE]
        cb = _dot_nt(cm, bm)
        st = state_ref[d, g]
        y_off = _dot(cm, st.astype(BF16)) * decay_out[:, g * gw:(g + 1) * gw]
        bm_t = bm.astype(F32).T.astype(BF16)
        state_ref[d, g] = st * chunk_decay[:, g * gw:(g + 1) * gw] + _dot(bm_t, xw[:, g * gw:(g + 1) * gw])

        pieces = []
        heads_per_group = SSM_HEADS // SSM_GROUPS
        for j in range(heads_per_group // 2):
            scores = []
            for hh in range(2):
                h = g * heads_per_group + 2 * j + hh
                hl = SSM_HEADS * d + h
                col = jnp.broadcast_to(cs[:, hl:hl + 1], (L, L))
                rowv = jnp.broadcast_to(cs_t[hl:hl + 1, :], (L, L))
                dec = jnp.exp(jnp.where(keep, col - rowv, -1e30))
                scores.append((cb * dec).astype(BF16))
            lhs = jnp.concatenate(scores, axis=1)
            c0 = g * gw + j * LANES
            xp = xdt_b[:, c0:c0 + LANES]
            zero = jnp.zeros_like(xp)
            rhs = jnp.concatenate([jnp.where(lane < SSM_HEAD_DIM, xp, zero),
                                   jnp.where(lane >= SSM_HEAD_DIM, xp, zero)], axis=0)
            pieces.append(_dot(lhs, rhs))
        ys.append(jnp.concatenate(pieces, axis=1) + y_off)
    return jnp.concatenate(ys, axis=1)


def _ssd_kernel(xf_ref, xb_ref, mf_ref, mb_ref, par_ref, yf_ref, yb_ref, state_ref):
    @pl.when(pl.program_id(1) == 0)
    def _():
        state_ref[...] = jnp.zeros_like(state_ref)

    L = CHUNK
    par = par_ref[...]
    lane = lax.broadcasted_iota(jnp.int32, (1, LANES), 1)
    active = lane < 2 * SSM_HEADS
    a_coef = jnp.where(active, -jnp.exp(par[1:2, :]), 0.0)
    r = lax.broadcasted_iota(jnp.int32, (L, L), 0)
    c = lax.broadcasted_iota(jnp.int32, (L, L), 1)
    tri_lo = jnp.where(c <= r, 1.0, 0.0).astype(BF16)
    tri_up = jnp.where(c >= r, 1.0, 0.0).astype(BF16)
    er = lax.broadcasted_iota(jnp.int32, (LANES, D_SSM), 0)
    ec = lax.broadcasted_iota(jnp.int32, (LANES, D_SSM), 1)

    for d, (x_ref, m_ref, y_ref) in enumerate(((xf_ref, mf_ref, yf_ref), (xb_ref, mb_ref, yb_ref))):
        reverse = d == 1
        expand = jnp.where(er == ec // SSM_HEAD_DIM + SSM_HEADS * d, 1.0, 0.0).astype(BF16)
        dt = jnp.where(active, _softplus(m_ref[0] + par[0:1, :]), 0.0)
        a = dt * a_coef
        a1, a2, a3 = _split3(a)
        tri = tri_up if reverse else tri_lo
        cs = _dot(tri, a1) + _dot(tri, a2) + _dot(tri, a3)
        dt_e = _dot_exact_lhs(dt, expand, 2)
        cs_e = _dot_exact_lhs(cs, expand, 3)
        y = _ssd_direction(x_ref[0], dt_e, cs, cs_e, 0 if reverse else L - 1, state_ref, d, reverse)
        y_ref[0] = y


def _ssd(xc, misc3, par):
    b, s, _ = xc.shape
    nc = s // CHUNK
    dtb = MISC_COLS // LANES - 1
    gw = D_SSM // SSM_GROUPS
    return pl.pallas_call(
        _ssd_kernel,
        grid=(b, nc),
        in_specs=[
            pl.BlockSpec((1, CHUNK, CONV_DIM), lambda i, c: (i, c, 0)),
            pl.BlockSpec((1, CHUNK, CONV_DIM), lambda i, c: (i, nc - 1 - c, 0)),
            pl.BlockSpec((1, CHUNK, LANES), lambda i, c: (i, c, dtb)),
            pl.BlockSpec((1, CHUNK, LANES), lambda i, c: (i, nc - 1 - c, dtb)),
            pl.BlockSpec((8, LANES), lambda i, c: (0, 0)),
        ],
        out_specs=[
            pl.BlockSpec((1, CHUNK, D_SSM), lambda i, c: (i, c, 0)),
            pl.BlockSpec((1, CHUNK, D_SSM), lambda i, c: (i, nc - 1 - c, 0)),
        ],
        out_shape=[jax.ShapeDtypeStruct((b, s, D_SSM), F32)] * 2,
        scratch_shapes=[pltpu.VMEM((2, SSM_GROUPS, D_STATE, gw), F32)],
        compiler_params=_cparams("parallel", "arbitrary"),
        name="ssd",
    )(xc, xc, misc3, misc3, par)


def _rms(x, w):
    ms = jnp.mean(x * x, axis=-1, keepdims=True)
    return x * lax.rsqrt(ms + RMS_EPS) * w


def _mla_prep_kernel(qa_ref, kva_ref, misc_ref, pos_ref, freq_ref, qnw_ref, kvnw_ref,
                     wq_ref, wqr_ref, wkv_ref, q_ref, k_ref, v_ref):
    scale = (QK_NOPE + QK_ROPE) ** -0.5
    qn = _rms(qa_ref[...].astype(F32), qnw_ref[...]).astype(BF16)
    kvn = _rms(kva_ref[...].astype(F32), kvnw_ref[...]).astype(BF16)
    ang = pos_ref[...].astype(F32) * freq_ref[...]
    cos = jnp.cos(ang)
    sin = jnp.sin(ang)
    qm = _dot(qn, wq_ref[...])
    qr = _dot(qn, wqr_ref[...])
    kv = _dot(kvn, wkv_ref[...])
    misc = misc_ref[...]
    k_rope = (misc[:, :LANES] * cos + misc[:, LANES:2 * LANES] * sin).astype(BF16)
    for h in range(MLA_HEADS):
        q_ref[0, h, :, :LANES] = (qm[:, h * QK_PAD:h * QK_PAD + LANES] * scale).astype(BF16)
        q_rope = qm[:, h * QK_PAD + LANES:(h + 1) * QK_PAD] * cos + qr[:, h * LANES:(h + 1) * LANES] * sin
        q_ref[0, h, :, LANES:] = (q_rope * scale).astype(BF16)
        k_ref[0, h, :, :LANES] = kv[:, h * 2 * LANES:h * 2 * LANES + LANES].astype(BF16)
        k_ref[0, h, :, LANES:] = k_rope
        v_ref[0, h] = kv[:, h * 2 * LANES + LANES:(h + 1) * 2 * LANES].astype(BF16)


def _mla_prep(p2, misc2, pos2, freq, qnw, kvnw, wq, wqr, wkv, b, s):
    n = p2.shape[0]
    spb = s // TM_PREP
    kv_blk = (D_SSM + CONV_DIM) // KV_LORA
    qa_blk = (D_SSM + CONV_DIM + KV_LORA) // Q_LORA
    out_idx = lambda i: (i // spb, 0, i % spb, 0)
    return pl.pallas_call(
        _mla_prep_kernel,
        grid=(n // TM_PREP,),
        in_specs=[
            pl.BlockSpec((TM_PREP, Q_LORA), lambda i: (i, qa_blk)),
            pl.BlockSpec((TM_PREP, KV_LORA), lambda i: (i, kv_blk)),
            pl.BlockSpec((TM_PREP, 2 * LANES), lambda i: (i, 0)),
            pl.BlockSpec((TM_PREP, 1), lambda i: (i, 0)),
            pl.BlockSpec((1, LANES), lambda i: (0, 0)),
            pl.BlockSpec((1, Q_LORA), lambda i: (0, 0)),
            pl.BlockSpec((1, KV_LORA), lambda i: (0, 0)),
            pl.BlockSpec((Q_LORA, MLA_HEADS * QK_PAD), lambda i: (0, 0)),
            pl.BlockSpec((Q_LORA, MLA_HEADS * LANES), lambda i: (0, 0)),
            pl.BlockSpec((KV_LORA, MLA_HEADS * 2 * LANES), lambda i: (0, 0)),
        ],
        out_specs=[
            pl.BlockSpec((1, MLA_HEADS, TM_PREP, QK_PAD), out_idx),
            pl.BlockSpec((1, MLA_HEADS, TM_PREP, QK_PAD), out_idx),
            pl.BlockSpec((1, MLA_HEADS, TM_PREP, V_DIM), out_idx),
        ],
        out_shape=[
            jax.ShapeDtypeStruct((b, MLA_HEADS, s, QK_PAD), BF16),
            jax.ShapeDtypeStruct((b, MLA_HEADS, s, QK_PAD), BF16),
            jax.ShapeDtypeStruct((b, MLA_HEADS, s, V_DIM), BF16),
        ],
        compiler_params=_cparams("parallel"),
        name="mla_prep",
    )(p2, p2, misc2, pos2, freq, qnw, kvnw, wq, wqr, wkv)


def _attn_kernel(q_ref, k_ref, v_ref, o_ref):
    s = _dot_nt(q_ref[0, 0], k_ref[0, 0])
    m = jnp.max(s, axis=-1, keepdims=True)
    p = jnp.exp(s - m)
    l = jnp.sum(p, axis=-1, keepdims=True)
    o = _dot(p.astype(BF16), v_ref[0, 0])
    o_ref[0] = (o / l).astype(BF16)


def _attention(q, k, v):
    b, h, s, _ = q.shape
    return pl.pallas_call(
        _attn_kernel,
        grid=(b, h, s // TQ),
        in_specs=[
            pl.BlockSpec((1, 1, TQ, QK_PAD), lambda i, j, t: (i, j, t, 0)),
            pl.BlockSpec((1, 1, s, QK_PAD), lambda i, j, t: (i, j, 0, 0)),
            pl.BlockSpec((1, 1, s, V_DIM), lambda i, j, t: (i, j, 0, 0)),
        ],
        out_specs=pl.BlockSpec((1, TQ, V_DIM), lambda i, j, t: (i, t, j)),
        out_shape=jax.ShapeDtypeStruct((b, s, h * V_DIM), BF16),
        compiler_params=_cparams("parallel", "parallel", "arbitrary"),
        name="attn",
    )(q, k, v)


def _out_proj_kernel(x_ref, yf_ref, yb_ref, xs_ref, z_ref, o_ref, dsk_ref, gnw_ref, anw_ref,
                     wo_ref, n2w_ref, wr1_ref, wr2_ref, br_ref,
                     h_ref, xn_ref, ri_ref, rg_ref, cnt_ref, carry_ref):
    i = pl.program_id(0)

    @pl.when(i == 0)
    def _():
        carry_ref[...] = jnp.zeros_like(carry_ref)

    tm = x_ref.shape[0]
    y = yf_ref[...] + yb_ref[...] + xs_ref[...].astype(F32) * dsk_ref[...]
    y = y * _silu(z_ref[...].astype(F32))
    gs = D_SSM // SSM_GROUPS
    halves = []
    for g in range(SSM_GROUPS):
        yg = y[:, g * gs:(g + 1) * gs]
        halves.append(_rms(yg, gnw_ref[:, g * gs:(g + 1) * gs]).astype(BF16))
    y_att = _rms(o_ref[...].astype(F32), anw_ref[...]).astype(BF16)
    lhs = jnp.concatenate(halves + [y_att], axis=1)
    h1 = x_ref[...] + _dot(lhs, wo_ref[...])
    h_ref[...] = h1
    xn = _rms(h1, n2w_ref[...])
    xn_ref[...] = xn

    x1 = xn.astype(BF16)
    x2 = (xn - x1.astype(F32)).astype(BF16)
    logits = _dot(x1, wr1_ref[...]) + (_dot(x1, wr2_ref[...]) + _dot(x2, wr1_ref[...])) + br_ref[...]
    lane = lax.broadcasted_iota(jnp.int32, (tm, LANES), 1)
    neg = jnp.float32(-jnp.inf)
    work = jnp.where(lane < N_EXPERTS, logits, neg)
    vals, ids = [], []
    for _ in range(TOP_K):
        m = jnp.max(work, axis=-1, keepdims=True)
        idx = jnp.min(jnp.where(work == m, lane, LANES), axis=-1, keepdims=True)
        vals.append(m)
        ids.append(idx)
        work = jnp.where(lane == idx, neg, work)
    es = [jnp.exp(v - vals[0]) for v in vals]
    den = es[0] + es[1] + es[2] + es[3]
    sel = jnp.zeros((tm, LANES), F32)
    for idx in ids:
        sel = sel + jnp.where(lane == idx, 1.0, 0.0)
    r = lax.broadcasted_iota(jnp.int32, (tm, tm), 0)
    c = lax.broadcasted_iota(jnp.int32, (tm, tm), 1)
    strict = jnp.where(c < r, 1.0, 0.0).astype(BF16)
    cum = _dot(strict, sel.astype(BF16)) + carry_ref[0:1, :]
    ri = jnp.zeros((tm, LANES), jnp.int32)
    rg = jnp.zeros((tm, LANES), F32)
    for kk in range(TOP_K):
        rank = jnp.sum(jnp.where(lane == ids[kk], cum, 0.0), axis=-1, keepdims=True).astype(jnp.int32)
        ri = jnp.where(lane == kk, ids[kk], ri)
        ri = jnp.where(lane == TOP_K + kk, rank, ri)
        rg = jnp.where(lane == kk, es[kk] / den, rg)
    ri_ref[...] = ri
    rg_ref[...] = rg
    total = carry_ref[0:1, :] + jnp.sum(sel, axis=0, keepdims=True)
    carry_ref[...] = jnp.broadcast_to(total, carry_ref.shape)
    cnt_ref[...] = jnp.broadcast_to(total, cnt_ref.shape)


def _out_proj(x2, yf, yb, xc2, p2, o2, dsk, gnw, anw, wo, n2w, wr1, wr2, br):
    n = x2.shape[0]
    row = lambda i: (i, 0)
    fixed = lambda i: (0, 0)
    return pl.pallas_call(
        _out_proj_kernel,
        grid=(n // TM_OUT,),
        in_specs=[
            pl.BlockSpec((TM_OUT, D_MODEL), row),
            pl.BlockSpec((TM_OUT, D_SSM), row),
            pl.BlockSpec((TM_OUT, D_SSM), row),
            pl.BlockSpec((TM_OUT, D_SSM), row),
            pl.BlockSpec((TM_OUT, D_SSM), row),
            pl.BlockSpec((TM_OUT, MLA_HEADS * V_DIM), row),
            pl.BlockSpec((1, D_SSM), fixed),
            pl.BlockSpec((1, D_SSM), fixed),
            pl.BlockSpec((1, MLA_HEADS * V_DIM), fixed),
            pl.BlockSpec((D_MODEL, D_MODEL), fixed),
            pl.BlockSpec((1, D_MODEL), fixed),
            pl.BlockSpec((D_MODEL, LANES), fixed),
            pl.BlockSpec((D_MODEL, LANES), fixed),
            pl.BlockSpec((1, LANES), fixed),
        ],
        out_specs=[
            pl.BlockSpec((TM_OUT, D_MODEL), row),
            pl.BlockSpec((TM_OUT, D_MODEL), row),
            pl.BlockSpec((TM_OUT, LANES), row),
            pl.BlockSpec((TM_OUT, LANES), row),
            pl.BlockSpec((8, LANES), fixed),
        ],
        out_shape=[
            jax.ShapeDtypeStruct((n, D_MODEL), F32),
            jax.ShapeDtypeStruct((n, D_MODEL), F32),
            jax.ShapeDtypeStruct((n, LANES), jnp.int32),
            jax.ShapeDtypeStruct((n, LANES), F32),
            jax.ShapeDtypeStruct((8, LANES), F32),
        ],
        scratch_shapes=[pltpu.VMEM((8, LANES), F32)],
        compiler_params=_cparams("arbitrary"),
        name="out_proj",
    )(x2, yf, yb, xc2, p2, o2, dsk, gnw, anw, wo, n2w, wr1, wr2, br)


def _dispatch_kernel(dest_ref, x_hbm, init_hbm, o_hbm, sem):
    del init_hbm
    base = pl.program_id(0) * TD

    def copy(j):
        tok = base + lax.shift_right_logical(j, 2)
        return pltpu.make_async_copy(x_hbm.at[pl.ds(tok, 1)], o_hbm.at[pl.ds(dest_ref[j], 1)], sem)

    def start(j, carry):
        copy(j).start()
        return carry

    def wait(j, carry):
        copy(j).wait()
        return carry

    lax.fori_loop(0, TD * TOP_K, start, 0)
    lax.fori_loop(0, TD * TOP_K, wait, 0)


def _dispatch(dest_flat, xn2, n_rows):
    n = xn2.shape[0]
    init = jnp.zeros((n_rows, D_MODEL), F32)
    return pl.pallas_call(
        _dispatch_kernel,
        grid=(n // TD,),
        in_specs=[
            pl.BlockSpec((TD * TOP_K,), lambda i: (i,), memory_space=pltpu.SMEM),
            pl.BlockSpec(memory_space=pl.ANY),
            pl.BlockSpec(memory_space=pl.ANY),
        ],
        out_specs=pl.BlockSpec(memory_space=pl.ANY),
        out_shape=jax.ShapeDtypeStruct((n_rows, D_MODEL), F32),
        scratch_shapes=[pltpu.SemaphoreType.DMA(())],
        input_output_aliases={2: 0},
        compiler_params=_cparams("arbitrary"),
        name="dispatch",
    )(dest_flat, xn2, init)


def _moe_kernel(be_ref, nu_ref, x_ref, wg_ref, wl_ref, wd_ref, bg_ref, bl_ref, bd_ref, y_ref, xb_ref, acc_ref):
    del be_ref
    blk = pl.program_id(0)
    hc = pl.program_id(1)

    @pl.when(blk < nu_ref[0])
    def _():
        @pl.when(hc == 0)
        def _():
            xb_ref[...] = x_ref[...].astype(BF16)

        xb = xb_ref[...]
        g = _dot(xb, wg_ref[0]) + bg_ref[0]
        l = _dot(xb, wl_ref[0]) + bl_ref[0]
        glu = jnp.minimum(g, SWIGLU_LIMIT)
        lin = jnp.clip(l, -SWIGLU_LIMIT, SWIGLU_LIMIT)
        act = glu / (1.0 + jnp.exp(-SWIGLU_ALPHA * glu)) * (lin + 1.0)
        part = _dot(act.astype(BF16), wd_ref[0])

        @pl.when(hc == 0)
        def _():
            acc_ref[...] = part + bd_ref[0]

        @pl.when(hc > 0)
        def _():
            acc_ref[...] += part

        @pl.when(hc == pl.num_programs(1) - 1)
        def _():
            y_ref[...] = acc_ref[...]

    @pl.when((blk >= nu_ref[0]) & (hc == pl.num_programs(1) - 1))
    def _():
        y_ref[...] = jnp.zeros_like(y_ref)


def _moe(block_expert, n_used, xs, wg, wl, wd, bg, bl, bd):
    n_rows = xs.shape[0]
    nb = n_rows // MOE_BLK
    nh = D_EXPERT // MOE_TC

    def bb(b, nu):
        return jnp.minimum(b, nu[0] - 1)

    def hh(b, h, nu):
        return jnp.where(b < nu[0], h, nh - 1)

    grid_spec = pltpu.PrefetchScalarGridSpec(
        num_scalar_prefetch=2,
        grid=(nb, nh),
        in_specs=[
            pl.BlockSpec((MOE_BLK, D_MODEL), lambda b, h, be, nu: (bb(b, nu), 0)),
            pl.BlockSpec((1, D_MODEL, MOE_TC), lambda b, h, be, nu: (be[bb(b, nu)], 0, hh(b, h, nu))),
            pl.BlockSpec((1, D_MODEL, MOE_TC), lambda b, h, be, nu: (be[bb(b, nu)], 0, hh(b, h, nu))),
            pl.BlockSpec((1, MOE_TC, D_MODEL), lambda b, h, be, nu: (be[bb(b, nu)], hh(b, h, nu), 0)),
            pl.BlockSpec((1, 1, MOE_TC), lambda b, h, be, nu: (be[bb(b, nu)], 0, hh(b, h, nu))),
            pl.BlockSpec((1, 1, MOE_TC), lambda b, h, be, nu: (be[bb(b, nu)], 0, hh(b, h, nu))),
            pl.BlockSpec((1, 1, D_MODEL), lambda b, h, be, nu: (be[bb(b, nu)], 0, 0)),
        ],
        out_specs=pl.BlockSpec((MOE_BLK, D_MODEL), lambda b, h, be, nu: (b, 0)),
        scratch_shapes=[pltpu.VMEM((MOE_BLK, D_MODEL), BF16), pltpu.VMEM((MOE_BLK, D_MODEL), F32)],
    )
    return pl.pallas_call(
        _moe_kernel,
        grid_spec=grid_spec,
        out_shape=jax.ShapeDtypeStruct((n_rows, D_MODEL), F32),
        compiler_params=_cparams("arbitrary", "arbitrary"),
        name="moe",
    )(block_expert, n_used, xs, wg, wl, wd, bg, bl, bd)


def _combine_kernel(dest_ref, y_hbm, h_ref, rg_ref, fw_ref, o_ref, buf_ref, sem):
    def copy(j):
        tok = lax.shift_right_logical(j, 2)
        k = lax.bitwise_and(j, TOP_K - 1)
        return pltpu.make_async_copy(y_hbm.at[pl.ds(dest_ref[j], 1)], buf_ref.at[k, pl.ds(tok, 1)], sem)

    def start(j, carry):
        copy(j).start()
        return carry

    def wait(j, carry):
        copy(j).wait()
        return carry

    lax.fori_loop(0, TD * TOP_K, start, 0)
    lax.fori_loop(0, TD * TOP_K, wait, 0)

    rg = rg_ref[...]
    out = h_ref[...]
    for k in range(TOP_K):
        out = out + buf_ref[k] * rg[:, k:k + 1]
    o_ref[...] = _rms(out, fw_ref[...])


def _combine(dest_flat, y_rows, h1, rg, final_w):
    n = h1.shape[0]
    row = lambda i: (i, 0)
    return pl.pallas_call(
        _combine_kernel,
        grid=(n // TD,),
        in_specs=[
            pl.BlockSpec((TD * TOP_K,), lambda i: (i,), memory_space=pltpu.SMEM),
            pl.BlockSpec(memory_space=pl.ANY),
            pl.BlockSpec((TD, D_MODEL), row),
            pl.BlockSpec((TD, LANES), row),
            pl.BlockSpec((1, D_MODEL), lambda i: (0, 0)),
        ],
        out_specs=pl.BlockSpec((TD, D_MODEL), row),
        out_shape=jax.ShapeDtypeStruct((n, D_MODEL), F32),
        scratch_shapes=[pltpu.VMEM((TOP_K, TD, D_MODEL), F32), pltpu.SemaphoreType.DMA(())],
        compiler_params=_cparams("arbitrary"),
        name="combine",
    )(dest_flat, y_rows, h1, rg, final_w)


def _pad_cols(w, width):
    return jnp.pad(w, ((0, 0), (0, width - w.shape[1])))


def _rot_cols(w):
    half = w.shape[1] // 2
    return jnp.concatenate([-w[:, half:], w[:, :half]], axis=1)


def _layer(h, positions, norm1_w, w_in, conv_w, conv_b, dt_bias_f, dt_bias_b, a_log_f, a_log_b, d_skip,
           ssm_norm_w, q_a_norm_w, w_q_b, kv_a_norm_w, w_kv_b, attn_norm_w, w_out, norm2_w, w_router,
           b_router, w_gate_up, b_gate_up, w_down, b_down, final_norm_w):
    b, s, _ = h.shape
    n = b * s
    x2 = h.reshape(n, D_MODEL)

    o_z, o_xbc = 0, D_SSM
    o_dtf = o_xbc + CONV_DIM
    o_dtb = o_dtf + SSM_HEADS
    o_qa = o_dtb + SSM_HEADS
    o_kva = o_qa + Q_LORA
    o_kpe = o_kva + KV_LORA
    w_z = w_in[:, o_z:o_xbc]
    w_xbc = w_in[:, o_xbc:o_dtf]
    w_qa = w_in[:, o_qa:o_kva]
    w_kva = w_in[:, o_kva:o_kpe]
    w_kpe = w_in[:, o_kpe:o_kpe + QK_ROPE]
    w_dt = w_in[:, o_dtf:o_qa]
    w_main = jnp.concatenate([w_z, w_xbc, w_kva, w_qa], axis=1).astype(BF16)
    w_misc = jnp.concatenate([_pad_cols(w_kpe, LANES), _pad_cols(_rot_cols(w_kpe), LANES),
                              _pad_cols(w_dt, LANES)], axis=1).astype(BF16)

    p2, misc2 = _in_proj(x2, norm1_w.reshape(1, D_MODEL), w_main, w_misc)

    xc = _conv(p2.reshape(b, s, P_COLS), conv_w, conv_b.reshape(1, CONV_DIM))
    par = jnp.zeros((8, LANES), F32)
    par = par.at[0, :2 * SSM_HEADS].set(jnp.concatenate([dt_bias_f, dt_bias_b]))
    par = par.at[1, :2 * SSM_HEADS].set(jnp.concatenate([a_log_f, a_log_b]))
    y_f, y_b = _ssd(xc, misc2.reshape(b, s, MISC_COLS), par)

    inv_freq = ROPE_THETA ** (-jnp.arange(0, QK_ROPE, 2, dtype=F32) / QK_ROPE)
    freq = _pad_cols(jnp.concatenate([inv_freq, inv_freq])[None, :], LANES)
    wq3 = w_q_b.reshape(Q_LORA, MLA_HEADS, QK_NOPE + QK_ROPE)
    wq_rope = wq3[:, :, QK_NOPE:]
    wq = jnp.pad(wq3, ((0, 0), (0, 0), (0, QK_PAD - QK_NOPE - QK_ROPE))).reshape(Q_LORA, MLA_HEADS * QK_PAD)
    wq_rot = jnp.concatenate([-wq_rope[:, :, QK_ROPE // 2:], wq_rope[:, :, :QK_ROPE // 2]], axis=2)
    wqr = jnp.pad(wq_rot, ((0, 0), (0, 0), (0, LANES - QK_ROPE))).reshape(Q_LORA, MLA_HEADS * LANES)
    q, k, v = _mla_prep(p2, misc2, positions.reshape(n, 1), freq, q_a_norm_w.reshape(1, Q_LORA),
                        kv_a_norm_w.reshape(1, KV_LORA), wq.astype(BF16), wqr.astype(BF16),
                        w_kv_b.astype(BF16), b, s)
    o = _attention(q, k, v)

    dsk = jnp.repeat(d_skip, SSM_HEAD_DIM)[None, :]
    wr = _pad_cols(w_router, LANES)
    wr1 = wr.astype(BF16)
    wr2 = (wr - wr1.astype(F32)).astype(BF16)
    br = _pad_cols(b_router[None, :], LANES)
    h1, xn2, ri, rg, cnt = _out_proj(
        x2, y_f.reshape(n, D_SSM), y_b.reshape(n, D_SSM), xc.reshape(n, CONV_DIM), p2, o.reshape(n, MLA_HEADS * V_DIM),
        dsk, ssm_norm_w.reshape(1, D_SSM), attn_norm_w.reshape(1, MLA_HEADS * V_DIM), w_out.astype(BF16),
        norm2_w.reshape(1, D_MODEL), wr1, wr2, br)

    counts = cnt[0, :N_EXPERTS].astype(jnp.int32)
    nblk = (counts + MOE_BLK - 1) // MOE_BLK
    blk_end = jnp.cumsum(nblk)
    row_start = (blk_end - nblk) * MOE_BLK
    n_blocks = (n * TOP_K) // MOE_BLK + N_EXPERTS
    n_rows = n_blocks * MOE_BLK
    ids = ri[:, :TOP_K]
    dest = (row_start[ids] + ri[:, TOP_K:2 * TOP_K]).reshape(n * TOP_K)
    block_expert = jnp.minimum(
        jnp.searchsorted(blk_end, jnp.arange(n_blocks, dtype=jnp.int32), side="right"), N_EXPERTS - 1
    ).astype(jnp.int32)
    n_used = blk_end[-1:].astype(jnp.int32)

    xs = _dispatch(dest, xn2, n_rows)

    wg = w_gate_up[:, :, 0::2].astype(BF16)
    wl = w_gate_up[:, :, 1::2].astype(BF16)
    bg = b_gate_up[:, None, 0::2]
    bl = b_gate_up[:, None, 1::2]
    y_rows = _moe(block_expert, n_used, xs, wg, wl, w_down.astype(BF16), bg, bl, b_down[:, None, :])

    out = _combine(dest, y_rows, h1, rg, final_norm_w.reshape(1, D_MODEL))
    return out.reshape(b, s, D_MODEL)


def kernel(x, positions, norm1_w, w_in, conv_w, conv_b, dt_bias_f, dt_bias_b, a_log_f, a_log_b, d_skip,
           ssm_norm_w, q_a_norm_w, w_q_b, kv_a_norm_w, w_kv_b, attn_norm_w, w_out, norm2_w, w_router, b_router,
           w_gate_up, b_gate_up, w_down, b_down, final_norm_w):
    depth = norm1_w.shape[0]
    assert depth == 1, "the final norm is fused into the single layer's combine step"
    return _layer(x, positions, norm1_w[0], w_in[0], conv_w[0], conv_b[0], dt_bias_f[0], dt_bias_b[0],
                  a_log_f[0], a_log_b[0], d_skip[0], ssm_norm_w[0], q_a_norm_w[0], w_q_b[0], kv_a_norm_w[0],
                  w_kv_b[0], attn_norm_w[0], w_out[0], norm2_w[0], w_router[0], b_router[0], w_gate_up[0],
                  b_gate_up[0], w_down[0], b_down[0], final_norm_w)
```

```python
import functools
import math

import jax
import jax.numpy as jnp
from jax import lax
from jax.experimental import pallas as pl
from jax.experimental.pallas import tpu as pltpu

F32 = jnp.float32
BF16 = jnp.bfloat16

D_MODEL = 2048
D_SSM = 1024
SSM_HEAD_DIM = 64
SSM_HEADS = 16
SSM_GROUPS = 2
D_STATE = 128
D_CONV = 5
CONV_DIM = D_SSM + 2 * SSM_GROUPS * D_STATE
CHUNK = 128
MLA_HEADS = 8
QK_NOPE = 128
QK_ROPE = 64
V_DIM = 128
Q_LORA = 768
KV_LORA = 512
ROPE_THETA = 10000.0
N_EXPERTS = 32
TOP_K = 4
D_EXPERT = 2048
SWIGLU_ALPHA = 1.702
SWIGLU_LIMIT = 7.0
RMS_EPS = 1e-6

LANES = 128
QK_PAD = 256
P_COLS = D_SSM + CONV_DIM + KV_LORA + Q_LORA
MISC_COLS = 3 * LANES
VMEM_LIMIT = 56 * 1024 * 1024

TM_IN = 512
TM_PREP = 512
TQ = 256
TM_OUT = 256
MOE_BLK = 512
MOE_TC = 1024
TD = 256


def _cparams(*sem):
    return pltpu.CompilerParams(dimension_semantics=sem, vmem_limit_bytes=VMEM_LIMIT)


def _silu(x):
    return x / (1.0 + jnp.exp(-x))


def _softplus(x):
    return jnp.maximum(x, 0.0) + jnp.log(1.0 + jnp.exp(-jnp.abs(x)))


def _split3(x):
    x1 = x.astype(BF16)
    r = x - x1.astype(F32)
    x2 = r.astype(BF16)
    x3 = (r - x2.astype(F32)).astype(BF16)
    return x1, x2, x3


def _dot(a, b):
    return jnp.dot(a, b, preferred_element_type=F32)


def _dot_nt(a, b):
    return lax.dot_general(a, b, (((1,), (1,)), ((), ())), preferred_element_type=F32)


def _dot_exact_lhs(x, m, passes):
    parts = _split3(x)[:passes]
    acc = _dot(parts[0], m)
    for p in parts[1:]:
        acc = acc + _dot(p, m)
    return acc


def _in_proj_kernel(x_ref, nw_ref, w_ref, wm_ref, p_ref, m_ref):
    x = x_ref[...]
    ms = jnp.mean(x * x, axis=-1, keepdims=True)
    xn = (x * lax.rsqrt(ms + RMS_EPS) * nw_ref[...]).astype(BF16)
    m_ref[...] = _dot(xn, wm_ref[...])
    p_ref[...] = _dot(xn, w_ref[...]).astype(BF16)


def _resident(shape):
    return pl.BlockSpec(shape, lambda *_: (0, 0), pipeline_mode=pl.Buffered(1))


def _in_proj(x2, norm_w, w_main, w_misc):
    n = x2.shape[0]
    return pl.pallas_call(
        _in_proj_kernel,
        grid=(n // TM_IN,),
        in_specs=[
            pl.BlockSpec((TM_IN, D_MODEL), lambda i: (i, 0)),
            _resident((1, D_MODEL)),
            _resident((D_MODEL, P_COLS)),
            _resident((D_MODEL, MISC_COLS)),
        ],
        out_specs=[
            pl.BlockSpec((TM_IN, P_COLS), lambda i: (i, 0)),
            pl.BlockSpec((TM_IN, MISC_COLS), lambda i: (i, 0)),
        ],
        out_shape=[
            jax.ShapeDtypeStruct((n, P_COLS), BF16),
            jax.ShapeDtypeStruct((n, MISC_COLS), F32),
        ],
        compiler_params=_cparams("parallel"),
        name="in_proj",
    )(x2, norm_w, w_main, w_misc)


def _conv_kernel(u_ref, w_ref, b_ref, o_ref):
    u = u_ref[0].astype(F32)
    s = u.shape[0]
    row = lax.broadcasted_iota(jnp.int32, u.shape, 0)
    pad = D_CONV // 2
    acc = u * w_ref[pad:pad + 1, :] + b_ref[...]
    for k in range(D_CONV):
        off = k - pad
        if off == 0:
            continue
        shifted = pltpu.roll(u, (-off) % s, 0)
        valid = (row + off >= 0) & (row + off < s)
        acc = acc + jnp.where(valid, shifted, 0.0) * w_ref[k:k + 1, :]
    o_ref[0] = _silu(acc).astype(BF16)


def _conv(p3, conv_w, conv_b):
    b, s, _ = p3.shape
    first = D_SSM // LANES
    return pl.pallas_call(
        _conv_kernel,
        grid=(b, CONV_DIM // LANES),
        in_specs=[
            pl.BlockSpec((1, s, LANES), lambda i, c: (i, 0, first + c)),
            pl.BlockSpec((D_CONV, LANES), lambda i, c: (0, c)),
            pl.BlockSpec((1, LANES), lambda i, c: (0, c)),
        ],
        out_specs=pl.BlockSpec((1, s, LANES), lambda i, c: (i, 0, c)),
        out_shape=jax.ShapeDtypeStruct((b, s, CONV_DIM), BF16),
        compiler_params=_cparams("parallel", "parallel"),
        name="conv",
    )(p3, conv_w, conv_b)


def _ssd_direction(xc, dt_e, cs, cs_e, tot_row, state_ref, d, reverse):
    L = xc.shape[0]
    gw = D_SSM // SSM_GROUPS
    x = xc[:, :D_SSM].astype(F32)
    xdt = x * dt_e
    xdt_b = xdt.astype(BF16)
    tot_e = cs_e[tot_row:tot_row + 1, :]
    decay_out = jnp.exp(cs_e)
    xw = (xdt * jnp.exp(tot_e - cs_e)).astype(BF16)
    chunk_decay = jnp.exp(tot_e)

    cs_t = cs.T
    li = lax.broadcasted_iota(jnp.int32, (L, L), 0)
    si = lax.broadcasted_iota(jnp.int32, (L, L), 1)
    keep = (si >= li) if reverse else (si <= li)
    lane = lax.broadcasted_iota(jnp.int32, (L, LANES), 1)

    ys = []
    for g in range(SSM_GROUPS):
        bm = xc[:, D_SSM + g * D_STATE:D_SSM + (g + 1) * D_STATE]
        cm = xc[:, D_SSM + SSM_GROUPS * D_STATE + g * D_STATE:D_SSM + SSM_GROUPS * D_STATE + (g + 1) * D_STATE]
        cb = _dot_nt(cm, bm)
        st = state_ref[d, g]
        y_off = _dot(cm, st.astype(BF16)) * decay_out[:, g * gw:(g + 1) * gw]
        bm_t = bm.astype(F32).T.astype(BF16)
        state_ref[d, g] = st * chunk_decay[:, g * gw:(g + 1) * gw] + _dot(bm_t, xw[:, g * gw:(g + 1) * gw])

        pieces = []
        heads_per_group = SSM_HEADS // SSM_GROUPS
        for j in range(heads_per_group // 2):
            scores = []
            for hh in range(2):
                h = g * heads_per_group + 2 * j + hh
                hl = SSM_HEADS * d + h
                col = jnp.broadcast_to(cs[:, hl:hl + 1], (L, L))
                rowv = jnp.broadcast_to(cs_t[hl:hl + 1, :], (L, L))
                dec = jnp.exp(jnp.where(keep, col - rowv, -1e30))
                scores.append((cb * dec).astype(BF16))
            lhs = jnp.concatenate(scores, axis=1)
            c0 = g * gw + j * LANES
            xp = xdt_b[:, c0:c0 + LANES]
            zero = jnp.zeros_like(xp)
            rhs = jnp.concatenate([jnp.where(lane < SSM_HEAD_DIM, xp, zero),
                                   jnp.where(lane >= SSM_HEAD_DIM, xp, zero)], axis=0)
            pieces.append(_dot(lhs, rhs))
        ys.append(jnp.concatenate(pieces, axis=1) + y_off)
    return jnp.concatenate(ys, axis=1)


def _ssd_kernel(xf_ref, xb_ref, mf_ref, mb_ref, par_ref, yf_ref, yb_ref, state_ref):
    @pl.when(pl.program_id(1) == 0)
    def _():
        state_ref[...] = jnp.zeros_like(state_ref)

    L = CHUNK
    par = par_ref[...]
    lane = lax.broadcasted_iota(jnp.int32, (1, LANES), 1)
    active = lane < 2 * SSM_HEADS
    a_coef = jnp.where(active, -jnp.exp(par[1:2, :]), 0.0)
    r = lax.broadcasted_iota(jnp.int32, (L, L), 0)
    c = lax.broadcasted_iota(jnp.int32, (L, L), 1)
    tri_lo = jnp.where(c <= r, 1.0, 0.0).astype(BF16)
    tri_up = jnp.where(c >= r, 1.0, 0.0).astype(BF16)
    er = lax.broadcasted_iota(jnp.int32, (LANES, D_SSM), 0)
    ec = lax.broadcasted_iota(jnp.int32, (LANES, D_SSM), 1)

    for d, (x_ref, m_ref, y_ref) in enumerate(((xf_ref, mf_ref, yf_ref), (xb_ref, mb_ref, yb_ref))):
        reverse = d == 1
        expand = jnp.where(er == ec // SSM_HEAD_DIM + SSM_HEADS * d, 1.0, 0.0).astype(BF16)
        dt = jnp.where(active, _softplus(m_ref[0] + par[0:1, :]), 0.0)
        a = dt * a_coef
        a1, a2, a3 = _split3(a)
        tri = tri_up if reverse else tri_lo
        cs = _dot(tri, a1) + _dot(tri, a2) + _dot(tri, a3)
        dt_e = _dot_exact_lhs(dt, expand, 2)
        cs_e = _dot_exact_lhs(cs, expand, 3)
        y = _ssd_direction(x_ref[0], dt_e, cs, cs_e, 0 if reverse else L - 1, state_ref, d, reverse)
        y_ref[0] = y


def _ssd(xc, misc3, par):
    b, s, _ = xc.shape
    nc = s // CHUNK
    dtb = MISC_COLS // LANES - 1
    gw = D_SSM // SSM_GROUPS
    return pl.pallas_call(
        _ssd_kernel,
        grid=(b, nc),
        in_specs=[
            pl.BlockSpec((1, CHUNK, CONV_DIM), lambda i, c: (i, c, 0)),
            pl.BlockSpec((1, CHUNK, CONV_DIM), lambda i, c: (i, nc - 1 - c, 0)),
            pl.BlockSpec((1, CHUNK, LANES), lambda i, c: (i, c, dtb)),
            pl.BlockSpec((1, CHUNK, LANES), lambda i, c: (i, nc - 1 - c, dtb)),
            pl.BlockSpec((8, LANES), lambda i, c: (0, 0)),
        ],
        out_specs=[
            pl.BlockSpec((1, CHUNK, D_SSM), lambda i, c: (i, c, 0)),
            pl.BlockSpec((1, CHUNK, D_SSM), lambda i, c: (i, nc - 1 - c, 0)),
        ],
        out_shape=[jax.ShapeDtypeStruct((b, s, D_SSM), F32)] * 2,
        scratch_shapes=[pltpu.VMEM((2, SSM_GROUPS, D_STATE, gw), F32)],
        compiler_params=_cparams("parallel", "arbitrary"),
        name="ssd",
    )(xc, xc, misc3, misc3, par)


def _rms(x, w):
    ms = jnp.mean(x * x, axis=-1, keepdims=True)
    return x * lax.rsqrt(ms + RMS_EPS) * w


def _mla_prep_kernel(qa_ref, kva_ref, misc_ref, pos_ref, freq_ref, qnw_ref, kvnw_ref,
                     wq_ref, wqr_ref, wkv_ref, q_ref, k_ref, v_ref):
    scale = (QK_NOPE + QK_ROPE) ** -0.5
    qn = _rms(qa_ref[...].astype(F32), qnw_ref[...]).astype(BF16)
    kvn = _rms(kva_ref[...].astype(F32), kvnw_ref[...]).astype(BF16)
    ang = pos_ref[...].astype(F32) * freq_ref[...]
    cos = jnp.cos(ang)
    sin = jnp.sin(ang)
    qm = _dot(qn, wq_ref[...])
    qr = _dot(qn, wqr_ref[...])
    kv = _dot(kvn, wkv_ref[...])
    misc = misc_ref[...]
    k_rope = (misc[:, :LANES] * cos + misc[:, LANES:2 * LANES] * sin).astype(BF16)
    for h in range(MLA_HEADS):
        q_ref[0, h, :, :LANES] = (qm[:, h * QK_PAD:h * QK_PAD + LANES] * scale).astype(BF16)
        q_rope = qm[:, h * QK_PAD + LANES:(h + 1) * QK_PAD] * cos + qr[:, h * LANES:(h + 1) * LANES] * sin
        q_ref[0, h, :, LANES:] = (q_rope * scale).astype(BF16)
        k_ref[0, h, :, :LANES] = kv[:, h * 2 * LANES:h * 2 * LANES + LANES].astype(BF16)
        k_ref[0, h, :, LANES:] = k_rope
        v_ref[0, h] = kv[:, h * 2 * LANES + LANES:(h + 1) * 2 * LANES].astype(BF16)


def _mla_prep(p2, misc2, pos2, freq, qnw, kvnw, wq, wqr, wkv, b, s):
    n = p2.shape[0]
    spb = s // TM_PREP
    kv_blk = (D_SSM + CONV_DIM) // KV_LORA
    qa_blk = (D_SSM + CONV_DIM + KV_LORA) // Q_LORA
    out_idx = lambda i: (i // spb, 0, i % spb, 0)
    return pl.pallas_call(
        _mla_prep_kernel,
        grid=(n // TM_PREP,),
        in_specs=[
            pl.BlockSpec((TM_PREP, Q_LORA), lambda i: (i, qa_blk)),
            pl.BlockSpec((TM_PREP, KV_LORA), lambda i: (i, kv_blk)),
            pl.BlockSpec((TM_PREP, 2 * LANES), lambda i: (i, 0)),
            pl.BlockSpec((TM_PREP, 1), lambda i: (i, 0)),
            pl.BlockSpec((1, LANES), lambda i: (0, 0)),
            pl.BlockSpec((1, Q_LORA), lambda i: (0, 0)),
            pl.BlockSpec((1, KV_LORA), lambda i: (0, 0)),
            pl.BlockSpec((Q_LORA, MLA_HEADS * QK_PAD), lambda i: (0, 0)),
            pl.BlockSpec((Q_LORA, MLA_HEADS * LANES), lambda i: (0, 0)),
            pl.BlockSpec((KV_LORA, MLA_HEADS * 2 * LANES), lambda i: (0, 0)),
        ],
        out_specs=[
            pl.BlockSpec((1, MLA_HEADS, TM_PREP, QK_PAD), out_idx),
            pl.BlockSpec((1, MLA_HEADS, TM_PREP, QK_PAD), out_idx),
            pl.BlockSpec((1, MLA_HEADS, TM_PREP, V_DIM), out_idx),
        ],
        out_shape=[
            jax.ShapeDtypeStruct((b, MLA_HEADS, s, QK_PAD), BF16),
            jax.ShapeDtypeStruct((b, MLA_HEADS, s, QK_PAD), BF16),
            jax.ShapeDtypeStruct((b, MLA_HEADS, s, V_DIM), BF16),
        ],
        compiler_params=_cparams("parallel"),
        name="mla_prep",
    )(p2, p2, misc2, pos2, freq, qnw, kvnw, wq, wqr, wkv)


def _attn_kernel(q_ref, k_ref, v_ref, o_ref):
    s = _dot_nt(q_ref[0, 0], k_ref[0, 0])
    m = jnp.max(s, axis=-1, keepdims=True)
    p = jnp.exp(s - m)
    l = jnp.sum(p, axis=-1, keepdims=True)
    o = _dot(p.astype(BF16), v_ref[0, 0])
    o_ref[0] = (o / l).astype(BF16)


def _attention(q, k, v):
    b, h, s, _ = q.shape
    return pl.pallas_call(
        _attn_kernel,
        grid=(b, h, s // TQ),
        in_specs=[
            pl.BlockSpec((1, 1, TQ, QK_PAD), lambda i, j, t: (i, j, t, 0)),
            pl.BlockSpec((1, 1, s, QK_PAD), lambda i, j, t: (i, j, 0, 0)),
            pl.BlockSpec((1, 1, s, V_DIM), lambda i, j, t: (i, j, 0, 0)),
        ],
        out_specs=pl.BlockSpec((1, TQ, V_DIM), lambda i, j, t: (i, t, j)),
        out_shape=jax.ShapeDtypeStruct((b, s, h * V_DIM), BF16),
        compiler_params=_cparams("parallel", "parallel", "arbitrary"),
        name="attn",
    )(q, k, v)


def _out_proj_kernel(x_ref, yf_ref, yb_ref, xs_ref, z_ref, o_ref, dsk_ref, gnw_ref, anw_ref,
                     wo_ref, n2w_ref, wr1_ref, wr2_ref, br_ref,
                     h_ref, xn_ref, ri_ref, rg_ref, cnt_ref, carry_ref):
    i = pl.program_id(0)

    @pl.when(i == 0)
    def _():
        carry_ref[...] = jnp.zeros_like(carry_ref)

    tm = x_ref.shape[0]
    y = yf_ref[...] + yb_ref[...] + xs_ref[...].astype(F32) * dsk_ref[...]
    y = y * _silu(z_ref[...].astype(F32))
    gs = D_SSM // SSM_GROUPS
    halves = []
    for g in range(SSM_GROUPS):
        yg = y[:, g * gs:(g + 1) * gs]
        halves.append(_rms(yg, gnw_ref[:, g * gs:(g + 1) * gs]).astype(BF16))
    y_att = _rms(o_ref[...].astype(F32), anw_ref[...]).astype(BF16)
    lhs = jnp.concatenate(halves + [y_att], axis=1)
    h1 = x_ref[...] + _dot(lhs, wo_ref[...])
    h_ref[...] = h1
    xn = _rms(h1, n2w_ref[...])
    xn_ref[...] = xn

    x1 = xn.astype(BF16)
    x2 = (xn - x1.astype(F32)).astype(BF16)
    logits = _dot(x1, wr1_ref[...]) + (_dot(x1, wr2_ref[...]) + _dot(x2, wr1_ref[...])) + br_ref[...]
    lane = lax.broadcasted_iota(jnp.int32, (tm, LANES), 1)
    neg = jnp.float32(-jnp.inf)
    work = jnp.where(lane < N_EXPERTS, logits, neg)
    vals, ids = [], []
    for _ in range(TOP_K):
        m = jnp.max(work, axis=-1, keepdims=True)
        idx = jnp.min(jnp.where(work == m, lane, LANES), axis=-1, keepdims=True)
        vals.append(m)
        ids.append(idx)
        work = jnp.where(lane == idx, neg, work)
    es = [jnp.exp(v - vals[0]) for v in vals]
    den = es[0] + es[1] + es[2] + es[3]
    sel = jnp.zeros((tm, LANES), F32)
    for idx in ids:
        sel = sel + jnp.where(lane == idx, 1.0, 0.0)
    r = lax.broadcasted_iota(jnp.int32, (tm, tm), 0)
    c = lax.broadcasted_iota(jnp.int32, (tm, tm), 1)
    strict = jnp.where(c < r, 1.0, 0.0).astype(BF16)
    cum = _dot(strict, sel.astype(BF16)) + carry_ref[0:1, :]
    ri = jnp.zeros((tm, LANES), jnp.int32)
    rg = jnp.zeros((tm, LANES), F32)
    for kk in range(TOP_K):
        rank = jnp.sum(jnp.where(lane == ids[kk], cum, 0.0), axis=-1, keepdims=True).astype(jnp.int32)
        ri = jnp.where(lane == kk, ids[kk], ri)
        ri = jnp.where(lane == TOP_K + kk, rank, ri)
        rg = jnp.where(lane == kk, es[kk] / den, rg)
    ri_ref[...] = ri
    rg_ref[...] = rg
    total = carry_ref[0:1, :] + jnp.sum(sel, axis=0, keepdims=True)
    carry_ref[...] = jnp.broadcast_to(total, carry_ref.shape)
    cnt_ref[...] = jnp.broadcast_to(total, cnt_ref.shape)


def _out_proj(x2, yf, yb, xc2, p2, o2, dsk, gnw, anw, wo, n2w, wr1, wr2, br):
    n = x2.shape[0]
    row = lambda i: (i, 0)
    fixed = lambda i: (0, 0)
    return pl.pallas_call(
        _out_proj_kernel,
        grid=(n // TM_OUT,),
        in_specs=[
            pl.BlockSpec((TM_OUT, D_MODEL), row),
            pl.BlockSpec((TM_OUT, D_SSM), row),
            pl.BlockSpec((TM_OUT, D_SSM), row),
            pl.BlockSpec((TM_OUT, D_SSM), row),
            pl.BlockSpec((TM_OUT, D_SSM), row),
            pl.BlockSpec((TM_OUT, MLA_HEADS * V_DIM), row),
            pl.BlockSpec((1, D_SSM), fixed),
            pl.BlockSpec((1, D_SSM), fixed),
            pl.BlockSpec((1, MLA_HEADS * V_DIM), fixed),
            pl.BlockSpec((D_MODEL, D_MODEL), fixed),
            pl.BlockSpec((1, D_MODEL), fixed),
            pl.BlockSpec((D_MODEL, LANES), fixed),
            pl.BlockSpec((D_MODEL, LANES), fixed),
            pl.BlockSpec((1, LANES), fixed),
        ],
        out_specs=[
            pl.BlockSpec((TM_OUT, D_MODEL), row),
            pl.BlockSpec((TM_OUT, D_MODEL), row),
            pl.BlockSpec((TM_OUT, LANES), row),
            pl.BlockSpec((TM_OUT, LANES), row),
            pl.BlockSpec((8, LANES), fixed),
        ],
        out_shape=[
            jax.ShapeDtypeStruct((n, D_MODEL), F32),
            jax.ShapeDtypeStruct((n, D_MODEL), F32),
            jax.ShapeDtypeStruct((n, LANES), jnp.int32),
            jax.ShapeDtypeStruct((n, LANES), F32),
            jax.ShapeDtypeStruct((8, LANES), F32),
        ],
        scratch_shapes=[pltpu.VMEM((8, LANES), F32)],
        compiler_params=_cparams("arbitrary"),
        name="out_proj",
    )(x2, yf, yb, xc2, p2, o2, dsk, gnw, anw, wo, n2w, wr1, wr2, br)


def _dispatch_kernel(dest_ref, x_ref, init_hbm, o_hbm, sem):
    del init_hbm

    def copy(j):
        tok = lax.shift_right_logical(j, 2)
        return pltpu.make_async_copy(x_ref.at[pl.ds(tok, 1)], o_hbm.at[pl.ds(dest_ref[j], 1)], sem)

    def start(j, carry):
        copy(j).start()
        return carry

    def wait(j, carry):
        copy(j).wait()
        return carry

    lax.fori_loop(0, TD * TOP_K, start, 0)
    lax.fori_loop(0, TD * TOP_K, wait, 0)


def _dispatch(dest_flat, xn2, n_rows):
    n = xn2.shape[0]
    init = jnp.zeros((n_rows, D_MODEL), F32)
    return pl.pallas_call(
        _dispatch_kernel,
        grid=(n // TD,),
        in_specs=[
            pl.BlockSpec((TD * TOP_K,), lambda i: (i,), memory_space=pltpu.SMEM),
            pl.BlockSpec((TD, D_MODEL), lambda i: (i, 0)),
            pl.BlockSpec(memory_space=pl.ANY),
        ],
        out_specs=pl.BlockSpec(memory_space=pl.ANY),
        out_shape=jax.ShapeDtypeStruct((n_rows, D_MODEL), F32),
        scratch_shapes=[pltpu.SemaphoreType.DMA(())],
        input_output_aliases={2: 0},
        compiler_params=_cparams("arbitrary"),
        name="dispatch",
    )(dest_flat, xn2, init)


def _regroup_kernel(w_ref, o_ref):
    k = lax.broadcasted_iota(jnp.int32, (2 * LANES, 2 * LANES), 0)
    c = lax.broadcasted_iota(jnp.int32, (2 * LANES, 2 * LANES), 1)
    src = jnp.where(c < LANES, 2 * c, 2 * (c - LANES) + 1)
    perm = jnp.where(k == src, 1.0, 0.0).astype(BF16)
    for j in range(w_ref.shape[2] // (2 * LANES)):
        cols = slice(2 * LANES * j, 2 * LANES * (j + 1))
        o_ref[0, :, cols] = _dot(w_ref[0, :, cols].astype(BF16), perm).astype(BF16)


def _regroup(w_gate_up):
    e, d, f = w_gate_up.shape
    tr = 512
    return pl.pallas_call(
        _regroup_kernel,
        grid=(e, d // tr),
        in_specs=[pl.BlockSpec((1, tr, f), lambda i, r: (i, r, 0))],
        out_specs=pl.BlockSpec((1, tr, f), lambda i, r: (i, r, 0)),
        out_shape=jax.ShapeDtypeStruct((e, d, f), BF16),
        compiler_params=_cparams("parallel", "parallel"),
        name="regroup",
    )(w_gate_up)


def _moe_kernel(be_ref, nu_ref, x_ref, wgl_ref, wd_ref, bgl_ref, bd_ref, y_ref, xb_ref, acc_ref):
    del be_ref
    blk = pl.program_id(0)
    hc = pl.program_id(1)
    last = pl.num_programs(1) - 1

    @pl.when(blk < nu_ref[0])
    def _():
        @pl.when(hc == 0)
        def _():
            xb_ref[...] = x_ref[...].astype(BF16)

        hdn = _dot(xb_ref[...], wgl_ref[0]) + bgl_ref[0]
        acts = []
        for j in range(MOE_TC // LANES):
            glu = jnp.minimum(hdn[:, 2 * LANES * j:2 * LANES * j + LANES], SWIGLU_LIMIT)
            lin = jnp.clip(hdn[:, 2 * LANES * j + LANES:2 * LANES * (j + 1)], -SWIGLU_LIMIT, SWIGLU_LIMIT)
            acts.append((glu / (1.0 + jnp.exp(-SWIGLU_ALPHA * glu)) * (lin + 1.0)).astype(BF16))
        part = _dot(jnp.concatenate(acts, axis=1), wd_ref[0])

        @pl.when(hc == 0)
        def _():
            acc_ref[...] = part + bd_ref[0]

        @pl.when((hc > 0) & (hc < last))
        def _():
            acc_ref[...] += part

        @pl.when(hc == last)
        def _():
            y_ref[...] = acc_ref[...] + part

    @pl.when((blk >= nu_ref[0]) & (hc == last))
    def _():
        y_ref[...] = jnp.zeros_like(y_ref)


def _moe(block_expert, n_used, xs, wgl, wd, bgl, bd):
    n_rows = xs.shape[0]
    nb = n_rows // MOE_BLK
    nh = D_EXPERT // MOE_TC
    assert nh >= 2

    def bb(b, nu):
        return jnp.minimum(b, nu[0] - 1)

    def hh(b, h, nu):
        return jnp.where(b < nu[0], h, nh - 1)

    grid_spec = pltpu.PrefetchScalarGridSpec(
        num_scalar_prefetch=2,
        grid=(nb, nh),
        in_specs=[
            pl.BlockSpec((MOE_BLK, D_MODEL), lambda b, h, be, nu: (bb(b, nu), 0)),
            pl.BlockSpec((1, D_MODEL, 2 * MOE_TC), lambda b, h, be, nu: (be[bb(b, nu)], 0, hh(b, h, nu))),
            pl.BlockSpec((1, MOE_TC, D_MODEL), lambda b, h, be, nu: (be[bb(b, nu)], hh(b, h, nu), 0)),
            pl.BlockSpec((1, 1, 2 * MOE_TC), lambda b, h, be, nu: (be[bb(b, nu)], 0, hh(b, h, nu))),
            pl.BlockSpec((1, 1, D_MODEL), lambda b, h, be, nu: (be[bb(b, nu)], 0, 0)),
        ],
        out_specs=pl.BlockSpec((MOE_BLK, D_MODEL), lambda b, h, be, nu: (b, 0)),
        scratch_shapes=[pltpu.VMEM((MOE_BLK, D_MODEL), BF16), pltpu.VMEM((MOE_BLK, D_MODEL), F32)],
    )
    return pl.pallas_call(
        _moe_kernel,
        grid_spec=grid_spec,
        out_shape=jax.ShapeDtypeStruct((n_rows, D_MODEL), F32),
        compiler_params=_cparams("arbitrary", "arbitrary"),
        name="moe",
    )(block_expert, n_used, xs, wgl, wd, bgl, bd)


def _combine_kernel(dest_ref, y_hbm, h_ref, rg_ref, fw_ref, o_ref, buf_ref, sem):
    def copy(j):
        tok = lax.shift_right_logical(j, 2)
        k = lax.bitwise_and(j, TOP_K - 1)
        return pltpu.make_async_copy(y_hbm.at[pl.ds(dest_ref[j], 1)], buf_ref.at[k, pl.ds(tok, 1)], sem)

    def start(j, carry):
        copy(j).start()
        return carry

    def wait(j, carry):
        copy(j).wait()
        return carry

    lax.fori_loop(0, TD * TOP_K, start, 0)
    lax.fori_loop(0, TD * TOP_K, wait, 0)

    rg = rg_ref[...]
    out = h_ref[...]
    for k in range(TOP_K):
        out = out + buf_ref[k] * rg[:, k:k + 1]
    o_ref[...] = _rms(out, fw_ref[...])


def _combine(dest_flat, y_rows, h1, rg, final_w):
    n = h1.shape[0]
    row = lambda i: (i, 0)
    return pl.pallas_call(
        _combine_kernel,
        grid=(n // TD,),
        in_specs=[
            pl.BlockSpec((TD * TOP_K,), lambda i: (i,), memory_space=pltpu.SMEM),
            pl.BlockSpec(memory_space=pl.ANY),
            pl.BlockSpec((TD, D_MODEL), row),
            pl.BlockSpec((TD, LANES), row),
            pl.BlockSpec((1, D_MODEL), lambda i: (0, 0)),
        ],
        out_specs=pl.BlockSpec((TD, D_MODEL), row),
        out_shape=jax.ShapeDtypeStruct((n, D_MODEL), F32),
        scratch_shapes=[pltpu.VMEM((TOP_K, TD, D_MODEL), F32), pltpu.SemaphoreType.DMA(())],
        compiler_params=_cparams("arbitrary"),
        name="combine",
    )(dest_flat, y_rows, h1, rg, final_w)


def _pad_cols(w, width):
    return jnp.pad(w, ((0, 0), (0, width - w.shape[1])))


def _rot_cols(w):
    half = w.shape[1] // 2
    return jnp.concatenate([-w[:, half:], w[:, :half]], axis=1)


def _layer(h, positions, norm1_w, w_in, conv_w, conv_b, dt_bias_f, dt_bias_b, a_log_f, a_log_b, d_skip,
           ssm_norm_w, q_a_norm_w, w_q_b, kv_a_norm_w, w_kv_b, attn_norm_w, w_out, norm2_w, w_router,
           b_router, w_gate_up, b_gate_up, w_down, b_down, final_norm_w):
    b, s, _ = h.shape
    n = b * s
    x2 = h.reshape(n, D_MODEL)

    o_z, o_xbc = 0, D_SSM
    o_dtf = o_xbc + CONV_DIM
    o_dtb = o_dtf + SSM_HEADS
    o_qa = o_dtb + SSM_HEADS
    o_kva = o_qa + Q_LORA
    o_kpe = o_kva + KV_LORA
    w_z = w_in[:, o_z:o_xbc]
    w_xbc = w_in[:, o_xbc:o_dtf]
    w_qa = w_in[:, o_qa:o_kva]
    w_kva = w_in[:, o_kva:o_kpe]
    w_kpe = w_in[:, o_kpe:o_kpe + QK_ROPE]
    w_dt = w_in[:, o_dtf:o_qa]
    w_main = jnp.concatenate([w_z, w_xbc, w_kva, w_qa], axis=1).astype(BF16)
    w_misc = jnp.concatenate([_pad_cols(w_kpe, LANES), _pad_cols(_rot_cols(w_kpe), LANES),
                              _pad_cols(w_dt, LANES)], axis=1).astype(BF16)

    p2, misc2 = _in_proj(x2, norm1_w.reshape(1, D_MODEL), w_main, w_misc)

    xc = _conv(p2.reshape(b, s, P_COLS), conv_w, conv_b.reshape(1, CONV_DIM))
    par = jnp.zeros((8, LANES), F32)
    par = par.at[0, :2 * SSM_HEADS].set(jnp.concatenate([dt_bias_f, dt_bias_b]))
    par = par.at[1, :2 * SSM_HEADS].set(jnp.concatenate([a_log_f, a_log_b]))
    y_f, y_b = _ssd(xc, misc2.reshape(b, s, MISC_COLS), par)

    inv_freq = ROPE_THETA ** (-jnp.arange(0, QK_ROPE, 2, dtype=F32) / QK_ROPE)
    freq = _pad_cols(jnp.concatenate([inv_freq, inv_freq])[None, :], LANES)
    wq3 = w_q_b.reshape(Q_LORA, MLA_HEADS, QK_NOPE + QK_ROPE)
    wq_rope = wq3[:, :, QK_NOPE:]
    wq = jnp.pad(wq3, ((0, 0), (0, 0), (0, QK_PAD - QK_NOPE - QK_ROPE))).reshape(Q_LORA, MLA_HEADS * QK_PAD)
    wq_rot = jnp.concatenate([-wq_rope[:, :, QK_ROPE // 2:], wq_rope[:, :, :QK_ROPE // 2]], axis=2)
    wqr = jnp.pad(wq_rot, ((0, 0), (0, 0), (0, LANES - QK_ROPE))).reshape(Q_LORA, MLA_HEADS * LANES)
    q, k, v = _mla_prep(p2, misc2, positions.reshape(n, 1), freq, q_a_norm_w.reshape(1, Q_LORA),
                        kv_a_norm_w.reshape(1, KV_LORA), wq.astype(BF16), wqr.astype(BF16),
                        w_kv_b.astype(BF16), b, s)
    o = _attention(q, k, v)

    dsk = jnp.repeat(d_skip, SSM_HEAD_DIM)[None, :]
    wr = _pad_cols(w_router, LANES)
    wr1 = wr.astype(BF16)
    wr2 = (wr - wr1.astype(F32)).astype(BF16)
    br = _pad_cols(b_router[None, :], LANES)
    h1, xn2, ri, rg, cnt = _out_proj(
        x2, y_f.reshape(n, D_SSM), y_b.reshape(n, D_SSM), xc.reshape(n, CONV_DIM), p2, o.reshape(n, MLA_HEADS * V_DIM),
        dsk, ssm_norm_w.reshape(1, D_SSM), attn_norm_w.reshape(1, MLA_HEADS * V_DIM), w_out.astype(BF16),
        norm2_w.reshape(1, D_MODEL), wr1, wr2, br)

    counts = cnt[0, :N_EXPERTS].astype(jnp.int32)
    nblk = (counts + MOE_BLK - 1) // MOE_BLK
    blk_end = jnp.cumsum(nblk)
    row_start = (blk_end - nblk) * MOE_BLK
    n_blocks = (n * TOP_K) // MOE_BLK + N_EXPERTS
    n_rows = n_blocks * MOE_BLK
    ids = ri[:, :TOP_K]
    dest = (row_start[ids] + ri[:, TOP_K:2 * TOP_K]).reshape(n * TOP_K)
    block_ids = jnp.arange(n_blocks, dtype=jnp.int32)
    block_expert = jnp.minimum(jnp.sum((blk_end[None, :] <= block_ids[:, None]).astype(jnp.int32), axis=1),
                               N_EXPERTS - 1)
    n_used = blk_end[-1:].astype(jnp.int32)

    xs = _dispatch(dest, xn2, n_rows)

    wgl = _regroup(w_gate_up)
    bgl = b_gate_up.reshape(N_EXPERTS, D_EXPERT // LANES, LANES, 2).swapaxes(2, 3).reshape(N_EXPERTS, 1, 2 * D_EXPERT)
    y_rows = _moe(block_expert, n_used, xs, wgl, w_down.astype(BF16), bgl, b_down[:, None, :])

    out = _combine(dest, y_rows, h1, rg, final_norm_w.reshape(1, D_MODEL))
    return out.reshape(b, s, D_MODEL)


def kernel(x, positions, norm1_w, w_in, conv_w, conv_b, dt_bias_f, dt_bias_b, a_log_f, a_log_b, d_skip,
           ssm_norm_w, q_a_norm_w, w_q_b, kv_a_norm_w, w_kv_b, attn_norm_w, w_out, norm2_w, w_router, b_router,
           w_gate_up, b_gate_up, w_down, b_down, final_norm_w):
    depth = norm1_w.shape[0]
    assert depth == 1, "the final norm is fused into the single layer's combine step"
    return _layer(x, positions, norm1_w[0], w_in[0], conv_w[0], conv_b[0], dt_bias_f[0], dt_bias_b[0],
                  a_log_f[0], a_log_b[0], d_skip[0], ssm_norm_w[0], q_a_norm_w[0], w_q_b[0], kv_a_norm_w[0],
                  w_kv_b[0], attn_norm_w[0], w_out[0], norm2_w[0], w_router[0], b_router[0], w_gate_up[0],
                  b_gate_up[0], w_down[0], b_down[0], final_norm_w)
```

```python
import functools
import math

import jax
import jax.numpy as jnp
from jax import lax
from jax.experimental import pallas as pl
from jax.experimental.pallas import tpu as pltpu

F32 = jnp.float32
BF16 = jnp.bfloat16

D_MODEL = 2048
D_SSM = 1024
SSM_HEAD_DIM = 64
SSM_HEADS = 16
SSM_GROUPS = 2
D_STATE = 128
D_CONV = 5
CONV_DIM = D_SSM + 2 * SSM_GROUPS * D_STATE
CHUNK = 128
MLA_HEADS = 8
QK_NOPE = 128
QK_ROPE = 64
V_DIM = 128
Q_LORA = 768
KV_LORA = 512
ROPE_THETA = 10000.0
N_EXPERTS = 32
TOP_K = 4
D_EXPERT = 2048
SWIGLU_ALPHA = 1.702
SWIGLU_LIMIT = 7.0
RMS_EPS = 1e-6

LANES = 128
QK_PAD = 256
P_COLS = D_SSM + CONV_DIM + KV_LORA + Q_LORA
MISC_COLS = 3 * LANES
VMEM_LIMIT = 56 * 1024 * 1024

TM_IN = 512
TM_PREP = 512
TQ = 256
TM_OUT = 256
MOE_BLK = 512
MOE_TC = 1024
TD = 256


def _cparams(*sem):
    return pltpu.CompilerParams(dimension_semantics=sem, vmem_limit_bytes=VMEM_LIMIT)


def _silu(x):
    return x / (1.0 + jnp.exp(-x))


def _softplus(x):
    return jnp.maximum(x, 0.0) + jnp.log(1.0 + jnp.exp(-jnp.abs(x)))


def _split3(x):
    x1 = x.astype(BF16)
    r = x - x1.astype(F32)
    x2 = r.astype(BF16)
    x3 = (r - x2.astype(F32)).astype(BF16)
    return x1, x2, x3


def _dot(a, b):
    return jnp.dot(a, b, preferred_element_type=F32)


def _dot_nt(a, b):
    return lax.dot_general(a, b, (((1,), (1,)), ((), ())), preferred_element_type=F32)


def _dot_exact_lhs(x, m, passes):
    parts = _split3(x)[:passes]
    acc = _dot(parts[0], m)
    for p in parts[1:]:
        acc = acc + _dot(p, m)
    return acc


def _in_proj_kernel(x_ref, nw_ref, w_ref, wm_ref, p_ref, m_ref):
    x = x_ref[...]
    ms = jnp.mean(x * x, axis=-1, keepdims=True)
    xn = (x * lax.rsqrt(ms + RMS_EPS) * nw_ref[...]).astype(BF16)
    m_ref[...] = _dot(xn, wm_ref[...])
    p_ref[...] = _dot(xn, w_ref[...]).astype(BF16)


def _resident(shape):
    return pl.BlockSpec(shape, lambda *_: (0, 0), pipeline_mode=pl.Buffered(1))


def _in_proj(x2, norm_w, w_main, w_misc):
    n = x2.shape[0]
    return pl.pallas_call(
        _in_proj_kernel,
        grid=(n // TM_IN,),
        in_specs=[
            pl.BlockSpec((TM_IN, D_MODEL), lambda i: (i, 0)),
            _resident((1, D_MODEL)),
            _resident((D_MODEL, P_COLS)),
            _resident((D_MODEL, MISC_COLS)),
        ],
        out_specs=[
            pl.BlockSpec((TM_IN, P_COLS), lambda i: (i, 0)),
            pl.BlockSpec((TM_IN, MISC_COLS), lambda i: (i, 0)),
        ],
        out_shape=[
            jax.ShapeDtypeStruct((n, P_COLS), BF16),
            jax.ShapeDtypeStruct((n, MISC_COLS), F32),
        ],
        compiler_params=_cparams("parallel"),
        name="in_proj",
    )(x2, norm_w, w_main, w_misc)


def _conv_kernel(u_ref, w_ref, b_ref, o_ref):
    u = u_ref[0].astype(F32)
    s = u.shape[0]
    row = lax.broadcasted_iota(jnp.int32, u.shape, 0)
    pad = D_CONV // 2
    acc = u * w_ref[pad:pad + 1, :] + b_ref[...]
    for k in range(D_CONV):
        off = k - pad
        if off == 0:
            continue
        shifted = pltpu.roll(u, (-off) % s, 0)
        valid = (row + off >= 0) & (row + off < s)
        acc = acc + jnp.where(valid, shifted, 0.0) * w_ref[k:k + 1, :]
    o_ref[0] = _silu(acc).astype(BF16)


def _conv(p3, conv_w, conv_b):
    b, s, _ = p3.shape
    first = D_SSM // LANES
    return pl.pallas_call(
        _conv_kernel,
        grid=(b, CONV_DIM // LANES),
        in_specs=[
            pl.BlockSpec((1, s, LANES), lambda i, c: (i, 0, first + c)),
            pl.BlockSpec((D_CONV, LANES), lambda i, c: (0, c)),
            pl.BlockSpec((1, LANES), lambda i, c: (0, c)),
        ],
        out_specs=pl.BlockSpec((1, s, LANES), lambda i, c: (i, 0, c)),
        out_shape=jax.ShapeDtypeStruct((b, s, CONV_DIM), BF16),
        compiler_params=_cparams("parallel", "parallel"),
        name="conv",
    )(p3, conv_w, conv_b)


def _ssd_direction(xc, dt_e, cs, cs_e, tot_row, state_ref, d, reverse):
    L = xc.shape[0]
    gw = D_SSM // SSM_GROUPS
    x = xc[:, :D_SSM].astype(F32)
    xdt = x * dt_e
    xdt_b = xdt.astype(BF16)
    tot_e = cs_e[tot_row:tot_row + 1, :]
    decay_out = jnp.exp(cs_e)
    xw = (xdt * jnp.exp(tot_e - cs_e)).astype(BF16)
    chunk_decay = jnp.exp(tot_e)

    cs_t = cs.T
    li = lax.broadcasted_iota(jnp.int32, (L, L), 0)
    si = lax.broadcasted_iota(jnp.int32, (L, L), 1)
    keep = (si >= li) if reverse else (si <= li)
    lane = lax.broadcasted_iota(jnp.int32, (L, LANES), 1)

    ys = []
    for g in range(SSM_GROUPS):
        bm = xc[:, D_SSM + g * D_STATE:D_SSM + (g + 1) * D_STATE]
        cm = xc[:, D_SSM + SSM_GROUPS * D_STATE + g * D_STATE:D_SSM + SSM_GROUPS * D_STATE + (g + 1) * D_STATE]
        cb = _dot_nt(cm, bm)
        st = state_ref[d, g]
        y_off = _dot(cm, st.astype(BF16)) * decay_out[:, g * gw:(g + 1) * gw]
        bm_t = bm.astype(F32).T.astype(BF16)
        state_ref[d, g] = st * chunk_decay[:, g * gw:(g + 1) * gw] + _dot(bm_t, xw[:, g * gw:(g + 1) * gw])

        pieces = []
        heads_per_group = SSM_HEADS // SSM_GROUPS
        for j in range(heads_per_group // 2):
            scores = []
            for hh in range(2):
                h = g * heads_per_group + 2 * j + hh
                hl = SSM_HEADS * d + h
                col = jnp.broadcast_to(cs[:, hl:hl + 1], (L, L))
                rowv = jnp.broadcast_to(cs_t[hl:hl + 1, :], (L, L))
                dec = jnp.exp(jnp.where(keep, col - rowv, -1e30))
                scores.append((cb * dec).astype(BF16))
            lhs = jnp.concatenate(scores, axis=1)
            c0 = g * gw + j * LANES
            xp = xdt_b[:, c0:c0 + LANES]
            zero = jnp.zeros_like(xp)
            rhs = jnp.concatenate([jnp.where(lane < SSM_HEAD_DIM, xp, zero),
                                   jnp.where(lane >= SSM_HEAD_DIM, xp, zero)], axis=0)
            pieces.append(_dot(lhs, rhs))
        ys.append(jnp.concatenate(pieces, axis=1) + y_off)
    return jnp.concatenate(ys, axis=1)


def _ssd_kernel(xf_ref, xb_ref, mf_ref, mb_ref, par_ref, yf_ref, yb_ref, state_ref):
    @pl.when(pl.program_id(1) == 0)
    def _():
        state_ref[...] = jnp.zeros_like(state_ref)

    L = CHUNK
    par = par_ref[...]
    lane = lax.broadcasted_iota(jnp.int32, (1, LANES), 1)
    active = lane < 2 * SSM_HEADS
    a_coef = jnp.where(active, -jnp.exp(par[1:2, :]), 0.0)
    r = lax.broadcasted_iota(jnp.int32, (L, L), 0)
    c = lax.broadcasted_iota(jnp.int32, (L, L), 1)
    tri_lo = jnp.where(c <= r, 1.0, 0.0).astype(BF16)
    tri_up = jnp.where(c >= r, 1.0, 0.0).astype(BF16)
    er = lax.broadcasted_iota(jnp.int32, (LANES, D_SSM), 0)
    ec = lax.broadcasted_iota(jnp.int32, (LANES, D_SSM), 1)

    for d, (x_ref, m_ref, y_ref) in enumerate(((xf_ref, mf_ref, yf_ref), (xb_ref, mb_ref, yb_ref))):
        reverse = d == 1
        expand = jnp.where(er == ec // SSM_HEAD_DIM + SSM_HEADS * d, 1.0, 0.0).astype(BF16)
        dt = jnp.where(active, _softplus(m_ref[0] + par[0:1, :]), 0.0)
        a = dt * a_coef
        a1, a2, a3 = _split3(a)
        tri = tri_up if reverse else tri_lo
        cs = _dot(tri, a1) + _dot(tri, a2) + _dot(tri, a3)
        dt_e = _dot_exact_lhs(dt, expand, 2)
        cs_e = _dot_exact_lhs(cs, expand, 3)
        y = _ssd_direction(x_ref[0], dt_e, cs, cs_e, 0 if reverse else L - 1, state_ref, d, reverse)
        y_ref[0] = y


def _ssd(xc, misc3, par):
    b, s, _ = xc.shape
    nc = s // CHUNK
    dtb = MISC_COLS // LANES - 1
    gw = D_SSM // SSM_GROUPS
    return pl.pallas_call(
        _ssd_kernel,
        grid=(b, nc),
        in_specs=[
            pl.BlockSpec((1, CHUNK, CONV_DIM), lambda i, c: (i, c, 0)),
            pl.BlockSpec((1, CHUNK, CONV_DIM), lambda i, c: (i, nc - 1 - c, 0)),
            pl.BlockSpec((1, CHUNK, LANES), lambda i, c: (i, c, dtb)),
            pl.BlockSpec((1, CHUNK, LANES), lambda i, c: (i, nc - 1 - c, dtb)),
            pl.BlockSpec((8, LANES), lambda i, c: (0, 0)),
        ],
        out_specs=[
            pl.BlockSpec((1, CHUNK, D_SSM), lambda i, c: (i, c, 0)),
            pl.BlockSpec((1, CHUNK, D_SSM), lambda i, c: (i, nc - 1 - c, 0)),
        ],
        out_shape=[jax.ShapeDtypeStruct((b, s, D_SSM), F32)] * 2,
        scratch_shapes=[pltpu.VMEM((2, SSM_GROUPS, D_STATE, gw), F32)],
        compiler_params=_cparams("parallel", "arbitrary"),
        name="ssd",
    )(xc, xc, misc3, misc3, par)


def _rms(x, w):
    ms = jnp.mean(x * x, axis=-1, keepdims=True)
    return x * lax.rsqrt(ms + RMS_EPS) * w


def _mla_prep_kernel(qa_ref, kva_ref, misc_ref, pos_ref, freq_ref, qnw_ref, kvnw_ref,
                     wq_ref, wqr_ref, wkv_ref, q_ref, k_ref, v_ref):
    scale = (QK_NOPE + QK_ROPE) ** -0.5
    qn = _rms(qa_ref[...].astype(F32), qnw_ref[...]).astype(BF16)
    kvn = _rms(kva_ref[...].astype(F32), kvnw_ref[...]).astype(BF16)
    ang = pos_ref[...].astype(F32) * freq_ref[...]
    cos = jnp.cos(ang)
    sin = jnp.sin(ang)
    qm = _dot(qn, wq_ref[...])
    qr = _dot(qn, wqr_ref[...])
    kv = _dot(kvn, wkv_ref[...])
    misc = misc_ref[...]
    k_rope = (misc[:, :LANES] * cos + misc[:, LANES:2 * LANES] * sin).astype(BF16)
    for h in range(MLA_HEADS):
        q_ref[0, h, :, :LANES] = (qm[:, h * QK_PAD:h * QK_PAD + LANES] * scale).astype(BF16)
        q_rope = qm[:, h * QK_PAD + LANES:(h + 1) * QK_PAD] * cos + qr[:, h * LANES:(h + 1) * LANES] * sin
        q_ref[0, h, :, LANES:] = (q_rope * scale).astype(BF16)
        k_ref[0, h, :, :LANES] = kv[:, h * 2 * LANES:h * 2 * LANES + LANES].astype(BF16)
        k_ref[0, h, :, LANES:] = k_rope
        v_ref[0, h] = kv[:, h * 2 * LANES + LANES:(h + 1) * 2 * LANES].astype(BF16)


def _mla_prep(p2, misc2, pos2, freq, qnw, kvnw, wq, wqr, wkv, b, s):
    n = p2.shape[0]
    spb = s // TM_PREP
    kv_blk = (D_SSM + CONV_DIM) // KV_LORA
    qa_blk = (D_SSM + CONV_DIM + KV_LORA) // Q_LORA
    out_idx = lambda i: (i // spb, 0, i % spb, 0)
    return pl.pallas_call(
        _mla_prep_kernel,
        grid=(n // TM_PREP,),
        in_specs=[
            pl.BlockSpec((TM_PREP, Q_LORA), lambda i: (i, qa_blk)),
            pl.BlockSpec((TM_PREP, KV_LORA), lambda i: (i, kv_blk)),
            pl.BlockSpec((TM_PREP, 2 * LANES), lambda i: (i, 0)),
            pl.BlockSpec((TM_PREP, 1), lambda i: (i, 0)),
            pl.BlockSpec((1, LANES), lambda i: (0, 0)),
            pl.BlockSpec((1, Q_LORA), lambda i: (0, 0)),
            pl.BlockSpec((1, KV_LORA), lambda i: (0, 0)),
            pl.BlockSpec((Q_LORA, MLA_HEADS * QK_PAD), lambda i: (0, 0)),
            pl.BlockSpec((Q_LORA, MLA_HEADS * LANES), lambda i: (0, 0)),
            pl.BlockSpec((KV_LORA, MLA_HEADS * 2 * LANES), lambda i: (0, 0)),
        ],
        out_specs=[
            pl.BlockSpec((1, MLA_HEADS, TM_PREP, QK_PAD), out_idx),
            pl.BlockSpec((1, MLA_HEADS, TM_PREP, QK_PAD), out_idx),
            pl.BlockSpec((1, MLA_HEADS, TM_PREP, V_DIM), out_idx),
        ],
        out_shape=[
            jax.ShapeDtypeStruct((b, MLA_HEADS, s, QK_PAD), BF16),
            jax.ShapeDtypeStruct((b, MLA_HEADS, s, QK_PAD), BF16),
            jax.ShapeDtypeStruct((b, MLA_HEADS, s, V_DIM), BF16),
        ],
        compiler_params=_cparams("parallel"),
        name="mla_prep",
    )(p2, p2, misc2, pos2, freq, qnw, kvnw, wq, wqr, wkv)


def _attn_kernel(q_ref, k_ref, v_ref, o_ref):
    s = _dot_nt(q_ref[0, 0], k_ref[0, 0])
    m = jnp.max(s, axis=-1, keepdims=True)
    p = jnp.exp(s - m)
    l = jnp.sum(p, axis=-1, keepdims=True)
    o = _dot(p.astype(BF16), v_ref[0, 0])
    o_ref[0] = (o / l).astype(BF16)


def _attention(q, k, v):
    b, h, s, _ = q.shape
    return pl.pallas_call(
        _attn_kernel,
        grid=(b, h, s // TQ),
        in_specs=[
            pl.BlockSpec((1, 1, TQ, QK_PAD), lambda i, j, t: (i, j, t, 0)),
            pl.BlockSpec((1, 1, s, QK_PAD), lambda i, j, t: (i, j, 0, 0)),
            pl.BlockSpec((1, 1, s, V_DIM), lambda i, j, t: (i, j, 0, 0)),
        ],
        out_specs=pl.BlockSpec((1, TQ, V_DIM), lambda i, j, t: (i, t, j)),
        out_shape=jax.ShapeDtypeStruct((b, s, h * V_DIM), BF16),
        compiler_params=_cparams("parallel", "parallel", "arbitrary"),
        name="attn",
    )(q, k, v)


def _out_proj_kernel(x_ref, yf_ref, yb_ref, xs_ref, z_ref, o_ref, dsk_ref, gnw_ref, anw_ref,
                     wo_ref, n2w_ref, wr1_ref, wr2_ref, br_ref,
                     h_ref, xn_ref, ri_ref, rg_ref, cnt_ref, carry_ref):
    i = pl.program_id(0)

    @pl.when(i == 0)
    def _():
        carry_ref[...] = jnp.zeros_like(carry_ref)

    tm = x_ref.shape[0]
    y = yf_ref[...] + yb_ref[...] + xs_ref[...].astype(F32) * dsk_ref[...]
    y = y * _silu(z_ref[...].astype(F32))
    gs = D_SSM // SSM_GROUPS
    halves = []
    for g in range(SSM_GROUPS):
        yg = y[:, g * gs:(g + 1) * gs]
        halves.append(_rms(yg, gnw_ref[:, g * gs:(g + 1) * gs]).astype(BF16))
    y_att = _rms(o_ref[...].astype(F32), anw_ref[...]).astype(BF16)
    lhs = jnp.concatenate(halves + [y_att], axis=1)
    h1 = x_ref[...] + _dot(lhs, wo_ref[...])
    h_ref[...] = h1
    xn = _rms(h1, n2w_ref[...])
    xn_ref[...] = xn

    x1 = xn.astype(BF16)
    x2 = (xn - x1.astype(F32)).astype(BF16)
    logits = _dot(x1, wr1_ref[...]) + (_dot(x1, wr2_ref[...]) + _dot(x2, wr1_ref[...])) + br_ref[...]
    lane = lax.broadcasted_iota(jnp.int32, (tm, LANES), 1)
    neg = jnp.float32(-jnp.inf)
    work = jnp.where(lane < N_EXPERTS, logits, neg)
    vals, ids = [], []
    for _ in range(TOP_K):
        m = jnp.max(work, axis=-1, keepdims=True)
        idx = jnp.min(jnp.where(work == m, lane, LANES), axis=-1, keepdims=True)
        vals.append(m)
        ids.append(idx)
        work = jnp.where(lane == idx, neg, work)
    es = [jnp.exp(v - vals[0]) for v in vals]
    den = es[0] + es[1] + es[2] + es[3]
    sel = jnp.zeros((tm, LANES), F32)
    for idx in ids:
        sel = sel + jnp.where(lane == idx, 1.0, 0.0)
    r = lax.broadcasted_iota(jnp.int32, (tm, tm), 0)
    c = lax.broadcasted_iota(jnp.int32, (tm, tm), 1)
    strict = jnp.where(c < r, 1.0, 0.0).astype(BF16)
    cum = _dot(strict, sel.astype(BF16)) + carry_ref[0:1, :]
    ri = jnp.zeros((tm, LANES), jnp.int32)
    rg = jnp.zeros((tm, LANES), F32)
    for kk in range(TOP_K):
        rank = jnp.sum(jnp.where(lane == ids[kk], cum, 0.0), axis=-1, keepdims=True).astype(jnp.int32)
        ri = jnp.where(lane == kk, ids[kk], ri)
        ri = jnp.where(lane == TOP_K + kk, rank, ri)
        rg = jnp.where(lane == kk, es[kk] / den, rg)
    ri_ref[...] = ri
    rg_ref[...] = rg
    total = carry_ref[0:1, :] + jnp.sum(sel, axis=0, keepdims=True)
    carry_ref[...] = jnp.broadcast_to(total, carry_ref.shape)
    cnt_ref[...] = jnp.broadcast_to(total, cnt_ref.shape)


def _out_proj(x2, yf, yb, xc2, p2, o2, dsk, gnw, anw, wo, n2w, wr1, wr2, br):
    n = x2.shape[0]
    row = lambda i: (i, 0)
    fixed = lambda i: (0, 0)
    return pl.pallas_call(
        _out_proj_kernel,
        grid=(n // TM_OUT,),
        in_specs=[
            pl.BlockSpec((TM_OUT, D_MODEL), row),
            pl.BlockSpec((TM_OUT, D_SSM), row),
            pl.BlockSpec((TM_OUT, D_SSM), row),
            pl.BlockSpec((TM_OUT, D_SSM), row),
            pl.BlockSpec((TM_OUT, D_SSM), row),
            pl.BlockSpec((TM_OUT, MLA_HEADS * V_DIM), row),
            pl.BlockSpec((1, D_SSM), fixed),
            pl.BlockSpec((1, D_SSM), fixed),
            pl.BlockSpec((1, MLA_HEADS * V_DIM), fixed),
            pl.BlockSpec((D_MODEL, D_MODEL), fixed),
            pl.BlockSpec((1, D_MODEL), fixed),
            pl.BlockSpec((D_MODEL, LANES), fixed),
            pl.BlockSpec((D_MODEL, LANES), fixed),
            pl.BlockSpec((1, LANES), fixed),
        ],
        out_specs=[
            pl.BlockSpec((TM_OUT, D_MODEL), row),
            pl.BlockSpec((TM_OUT, D_MODEL), row),
            pl.BlockSpec((TM_OUT, LANES), row),
            pl.BlockSpec((TM_OUT, LANES), row),
            pl.BlockSpec((8, LANES), fixed),
        ],
        out_shape=[
            jax.ShapeDtypeStruct((n, D_MODEL), F32),
            jax.ShapeDtypeStruct((n, D_MODEL), F32),
            jax.ShapeDtypeStruct((n, LANES), jnp.int32),
            jax.ShapeDtypeStruct((n, LANES), F32),
            jax.ShapeDtypeStruct((8, LANES), F32),
        ],
        scratch_shapes=[pltpu.VMEM((8, LANES), F32)],
        compiler_params=_cparams("arbitrary"),
        name="out_proj",
    )(x2, yf, yb, xc2, p2, o2, dsk, gnw, anw, wo, n2w, wr1, wr2, br)


def _regroup_kernel(w_ref, o_ref):
    k = lax.broadcasted_iota(jnp.int32, (2 * LANES, 2 * LANES), 0)
    c = lax.broadcasted_iota(jnp.int32, (2 * LANES, 2 * LANES), 1)
    src = jnp.where(c < LANES, 2 * c, 2 * (c - LANES) + 1)
    perm = jnp.where(k == src, 1.0, 0.0).astype(BF16)
    for j in range(w_ref.shape[2] // (2 * LANES)):
        cols = slice(2 * LANES * j, 2 * LANES * (j + 1))
        o_ref[0, :, cols] = _dot(w_ref[0, :, cols].astype(BF16), perm).astype(BF16)


def _regroup(w_gate_up):
    e, d, f = w_gate_up.shape
    tr = 512
    return pl.pallas_call(
        _regroup_kernel,
        grid=(e, d // tr),
        in_specs=[pl.BlockSpec((1, tr, f), lambda i, r: (i, r, 0))],
        out_specs=pl.BlockSpec((1, tr, f), lambda i, r: (i, r, 0)),
        out_shape=jax.ShapeDtypeStruct((e, d, f), BF16),
        compiler_params=_cparams("parallel", "parallel"),
        name="regroup",
    )(w_gate_up)


def _moe_kernel(be_ref, nu_ref, tok_ref, tokn_ref, pos_ref, x_hbm, wgl_ref, wd_ref, bgl_ref, bd_ref,
                y_hbm, xg_ref, xb_ref, acc_ref, yb_ref, gsem, ssem):
    del be_ref
    blk = pl.program_id(0)
    hc = pl.program_id(1)
    nblk = pl.num_programs(0)
    last = pl.num_programs(1) - 1
    nu = nu_ref[0]
    slot = lax.rem(blk, 2)

    def gather(src_row, s, j):
        return pltpu.make_async_copy(x_hbm.at[pl.ds(src_row, 1)], xg_ref.at[s, pl.ds(j, 1)], gsem.at[s])

    def scatter(j, dst_row):
        return pltpu.make_async_copy(yb_ref.at[pl.ds(j, 1)], y_hbm.at[pl.ds(dst_row, 1)], ssem)

    def start_gather(rows_ref, s):
        def body(j, carry):
            gather(rows_ref[0, 0, j], s, j).start()
            return carry
        lax.fori_loop(0, MOE_BLK, body, 0, unroll=8)

    def wait_gather(s):
        for _ in range(MOE_BLK):
            gather(0, s, 0).wait()

    def start_scatter():
        def body(j, carry):
            scatter(j, pos_ref[0, 0, j]).start()
            return carry
        lax.fori_loop(0, MOE_BLK, body, 0, unroll=8)

    def wait_scatter():
        for _ in range(MOE_BLK):
            scatter(0, 0).wait()

    @pl.when(blk < nu)
    def _():
        @pl.when(hc == 0)
        def _():
            @pl.when(blk == 0)
            def _():
                start_gather(tok_ref, 0)

            wait_gather(slot)
            xb_ref[...] = xg_ref[slot].astype(BF16)

            @pl.when(blk + 1 < nu)
            def _():
                start_gather(tokn_ref, 1 - slot)

        hdn = _dot(xb_ref[...], wgl_ref[0]) + bgl_ref[0]
        acts = []
        for j in range(MOE_TC // LANES):
            glu = jnp.minimum(hdn[:, 2 * LANES * j:2 * LANES * j + LANES], SWIGLU_LIMIT)
            lin = jnp.clip(hdn[:, 2 * LANES * j + LANES:2 * LANES * (j + 1)], -SWIGLU_LIMIT, SWIGLU_LIMIT)
            acts.append((glu / (1.0 + jnp.exp(-SWIGLU_ALPHA * glu)) * (lin + 1.0)).astype(BF16))
        part = _dot(jnp.concatenate(acts, axis=1), wd_ref[0])

        @pl.when(hc == 0)
        def _():
            acc_ref[...] = part + bd_ref[0]

        @pl.when((hc > 0) & (hc < last))
        def _():
            acc_ref[...] += part

        @pl.when(hc == last)
        def _():
            @pl.when(blk > 0)
            def _():
                wait_scatter()

            yb_ref[...] = acc_ref[...] + part
            start_scatter()

    @pl.when((blk >= nu) & (hc == last))
    def _():
        wait_scatter()
        yb_ref[...] = jnp.zeros_like(yb_ref)
        start_scatter()

    @pl.when((blk == nblk - 1) & (hc == last))
    def _():
        wait_scatter()


def _moe(block_expert, n_used, row_tok, row_pos, xn2, wgl, wd, bgl, bd):
    nb = row_tok.shape[0]
    nh = D_EXPERT // MOE_TC
    assert nh >= 2

    def bb(b, nu):
        return jnp.minimum(b, nu[0] - 1)

    def hh(b, h, nu):
        return jnp.where(b < nu[0], h, nh - 1)

    smem_rows = lambda index_map: pl.BlockSpec((1, 1, MOE_BLK), index_map, memory_space=pltpu.SMEM)
    grid_spec = pltpu.PrefetchScalarGridSpec(
        num_scalar_prefetch=2,
        grid=(nb, nh),
        in_specs=[
            smem_rows(lambda b, h, be, nu: (b, 0, 0)),
            smem_rows(lambda b, h, be, nu: (jnp.minimum(b + 1, nb - 1), 0, 0)),
            smem_rows(lambda b, h, be, nu: (b, 0, 0)),
            pl.BlockSpec(memory_space=pl.ANY),
            pl.BlockSpec((1, D_MODEL, 2 * MOE_TC), lambda b, h, be, nu: (be[bb(b, nu)], 0, hh(b, h, nu))),
            pl.BlockSpec((1, MOE_TC, D_MODEL), lambda b, h, be, nu: (be[bb(b, nu)], hh(b, h, nu), 0)),
            pl.BlockSpec((1, 1, 2 * MOE_TC), lambda b, h, be, nu: (be[bb(b, nu)], 0, hh(b, h, nu))),
            pl.BlockSpec((1, 1, D_MODEL), lambda b, h, be, nu: (be[bb(b, nu)], 0, 0)),
        ],
        out_specs=pl.BlockSpec(memory_space=pl.ANY),
        scratch_shapes=[
            pltpu.VMEM((2, MOE_BLK, D_MODEL), F32),
            pltpu.VMEM((MOE_BLK, D_MODEL), BF16),
            pltpu.VMEM((MOE_BLK, D_MODEL), F32),
            pltpu.VMEM((MOE_BLK, D_MODEL), F32),
            pltpu.SemaphoreType.DMA((2,)),
            pltpu.SemaphoreType.DMA(()),
        ],
    )
    return pl.pallas_call(
        _moe_kernel,
        grid_spec=grid_spec,
        out_shape=jax.ShapeDtypeStruct((nb * MOE_BLK, D_MODEL), F32),
        compiler_params=_cparams("arbitrary", "arbitrary"),
        name="moe",
    )(block_expert, n_used, row_tok, row_tok, row_pos, xn2, wgl, wd, bgl, bd)


def _combine_kernel(y0_ref, y1_ref, y2_ref, y3_ref, h_ref, rg_ref, fw_ref, o_ref):
    rg = rg_ref[...]
    out = h_ref[...]
    for k, y_ref in enumerate((y0_ref, y1_ref, y2_ref, y3_ref)):
        out = out + y_ref[...] * rg[:, k:k + 1]
    o_ref[...] = _rms(out, fw_ref[...])


def _combine(y_rows, h1, rg, final_w):
    n = h1.shape[0]
    nt = n // TD
    row = lambda i: (i, 0)
    slot_spec = lambda k: pl.BlockSpec((TD, D_MODEL), lambda i: (k * nt + i, 0))
    return pl.pallas_call(
        _combine_kernel,
        grid=(nt,),
        in_specs=[slot_spec(k) for k in range(TOP_K)] + [
            pl.BlockSpec((TD, D_MODEL), row),
            pl.BlockSpec((TD, LANES), row),
            pl.BlockSpec((1, D_MODEL), lambda i: (0, 0)),
        ],
        out_specs=pl.BlockSpec((TD, D_MODEL), row),
        out_shape=jax.ShapeDtypeStruct((n, D_MODEL), F32),
        compiler_params=_cparams("parallel"),
        name="combine",
    )(y_rows, y_rows, y_rows, y_rows, h1, rg, final_w)


def _pad_cols(w, width):
    return jnp.pad(w, ((0, 0), (0, width - w.shape[1])))


def _rot_cols(w):
    half = w.shape[1] // 2
    return jnp.concatenate([-w[:, half:], w[:, :half]], axis=1)


def _layer(h, positions, norm1_w, w_in, conv_w, conv_b, dt_bias_f, dt_bias_b, a_log_f, a_log_b, d_skip,
           ssm_norm_w, q_a_norm_w, w_q_b, kv_a_norm_w, w_kv_b, attn_norm_w, w_out, norm2_w, w_router,
           b_router, w_gate_up, b_gate_up, w_down, b_down, final_norm_w):
    b, s, _ = h.shape
    n = b * s
    x2 = h.reshape(n, D_MODEL)

    o_z, o_xbc = 0, D_SSM
    o_dtf = o_xbc + CONV_DIM
    o_dtb = o_dtf + SSM_HEADS
    o_qa = o_dtb + SSM_HEADS
    o_kva = o_qa + Q_LORA
    o_kpe = o_kva + KV_LORA
    w_z = w_in[:, o_z:o_xbc]
    w_xbc = w_in[:, o_xbc:o_dtf]
    w_qa = w_in[:, o_qa:o_kva]
    w_kva = w_in[:, o_kva:o_kpe]
    w_kpe = w_in[:, o_kpe:o_kpe + QK_ROPE]
    w_dt = w_in[:, o_dtf:o_qa]
    w_main = jnp.concatenate([w_z, w_xbc, w_kva, w_qa], axis=1).astype(BF16)
    w_misc = jnp.concatenate([_pad_cols(w_kpe, LANES), _pad_cols(_rot_cols(w_kpe), LANES),
                              _pad_cols(w_dt, LANES)], axis=1).astype(BF16)

    p2, misc2 = _in_proj(x2, norm1_w.reshape(1, D_MODEL), w_main, w_misc)

    xc = _conv(p2.reshape(b, s, P_COLS), conv_w, conv_b.reshape(1, CONV_DIM))
    par = jnp.zeros((8, LANES), F32)
    par = par.at[0, :2 * SSM_HEADS].set(jnp.concatenate([dt_bias_f, dt_bias_b]))
    par = par.at[1, :2 * SSM_HEADS].set(jnp.concatenate([a_log_f, a_log_b]))
    y_f, y_b = _ssd(xc, misc2.reshape(b, s, MISC_COLS), par)

    inv_freq = ROPE_THETA ** (-jnp.arange(0, QK_ROPE, 2, dtype=F32) / QK_ROPE)
    freq = _pad_cols(jnp.concatenate([inv_freq, inv_freq])[None, :], LANES)
    wq3 = w_q_b.reshape(Q_LORA, MLA_HEADS, QK_NOPE + QK_ROPE)
    wq_rope = wq3[:, :, QK_NOPE:]
    wq = jnp.pad(wq3, ((0, 0), (0, 0), (0, QK_PAD - QK_NOPE - QK_ROPE))).reshape(Q_LORA, MLA_HEADS * QK_PAD)
    wq_rot = jnp.concatenate([-wq_rope[:, :, QK_ROPE // 2:], wq_rope[:, :, :QK_ROPE // 2]], axis=2)
    wqr = jnp.pad(wq_rot, ((0, 0), (0, 0), (0, LANES - QK_ROPE))).reshape(Q_LORA, MLA_HEADS * LANES)
    q, k, v = _mla_prep(p2, misc2, positions.reshape(n, 1), freq, q_a_norm_w.reshape(1, Q_LORA),
                        kv_a_norm_w.reshape(1, KV_LORA), wq.astype(BF16), wqr.astype(BF16),
                        w_kv_b.astype(BF16), b, s)
    o = _attention(q, k, v)

    dsk = jnp.repeat(d_skip, SSM_HEAD_DIM)[None, :]
    wr = _pad_cols(w_router, LANES)
    wr1 = wr.astype(BF16)
    wr2 = (wr - wr1.astype(F32)).astype(BF16)
    br = _pad_cols(b_router[None, :], LANES)
    h1, xn2, ri, rg, cnt = _out_proj(
        x2, y_f.reshape(n, D_SSM), y_b.reshape(n, D_SSM), xc.reshape(n, CONV_DIM), p2, o.reshape(n, MLA_HEADS * V_DIM),
        dsk, ssm_norm_w.reshape(1, D_SSM), attn_norm_w.reshape(1, MLA_HEADS * V_DIM), w_out.astype(BF16),
        norm2_w.reshape(1, D_MODEL), wr1, wr2, br)

    counts = cnt[0, :N_EXPERTS].astype(jnp.int32)
    nblk = (counts + MOE_BLK - 1) // MOE_BLK
    blk_end = jnp.cumsum(nblk)
    row_start = (blk_end - nblk) * MOE_BLK
    n_blocks = (n * TOP_K) // MOE_BLK + N_EXPERTS
    n_rows = n_blocks * MOE_BLK
    ids = ri[:, :TOP_K]
    dest = (row_start[ids] + ri[:, TOP_K:2 * TOP_K]).reshape(n * TOP_K)
    block_ids = jnp.arange(n_blocks, dtype=jnp.int32)
    block_expert = jnp.minimum(jnp.sum((blk_end[None, :] <= block_ids[:, None]).astype(jnp.int32), axis=1),
                               N_EXPERTS - 1)
    n_used = blk_end[-1:].astype(jnp.int32)

    assign = (jnp.arange(TOP_K, dtype=jnp.int32)[None, :] * n + jnp.arange(n, dtype=jnp.int32)[:, None])
    inv = jnp.full((n_rows,), -1, jnp.int32).at[dest].set(assign.reshape(n * TOP_K), unique_indices=True)
    is_pad = inv < 0
    pad_pos = n * TOP_K + jnp.cumsum(is_pad.astype(jnp.int32)) - 1
    row_pos = jnp.where(is_pad, pad_pos, inv).reshape(n_blocks, 1, MOE_BLK)
    row_tok = jnp.where(is_pad, 0, inv % n).reshape(n_blocks, 1, MOE_BLK)

    wgl = _regroup(w_gate_up)
    bgl = b_gate_up.reshape(N_EXPERTS, D_EXPERT // LANES, LANES, 2).swapaxes(2, 3).reshape(N_EXPERTS, 1, 2 * D_EXPERT)
    y_rows = _moe(block_expert, n_used, row_tok, row_pos, xn2, wgl, w_down.astype(BF16), bgl, b_down[:, None, :])

    out = _combine(y_rows, h1, rg, final_norm_w.reshape(1, D_MODEL))
    return out.reshape(b, s, D_MODEL)


def kernel(x, positions, norm1_w, w_in, conv_w, conv_b, dt_bias_f, dt_bias_b, a_log_f, a_log_b, d_skip,
           ssm_norm_w, q_a_norm_w, w_q_b, kv_a_norm_w, w_kv_b, attn_norm_w, w_out, norm2_w, w_router, b_router,
           w_gate_up, b_gate_up, w_down, b_down, final_norm_w):
    depth = norm1_w.shape[0]
    assert depth == 1, "the final norm is fused into the single layer's combine step"
    return _layer(x, positions, norm1_w[0], w_in[0], conv_w[0], conv_b[0], dt_bias_f[0], dt_bias_b[0],
                  a_log_f[0], a_log_b[0], d_skip[0], ssm_norm_w[0], q_a_norm_w[0], w_q_b[0], kv_a_norm_w[0],
                  w_kv_b[0], attn_norm_w[0], w_out[0], norm2_w[0], w_router[0], b_router[0], w_gate_up[0],
                  b_gate_up[0], w_down[0], b_down[0], final_norm_w)
```

```python
import functools
import math

import jax
import jax.numpy as jnp
from jax import lax
from jax.experimental import pallas as pl
from jax.experimental.pallas import tpu as pltpu

F32 = jnp.float32
BF16 = jnp.bfloat16

D_MODEL = 2048
D_SSM = 1024
SSM_HEAD_DIM = 64
SSM_HEADS = 16
SSM_GROUPS = 2
D_STATE = 128
D_CONV = 5
CONV_DIM = D_SSM + 2 * SSM_GROUPS * D_STATE
CHUNK = 128
MLA_HEADS = 8
QK_NOPE = 128
QK_ROPE = 64
V_DIM = 128
Q_LORA = 768
KV_LORA = 512
ROPE_THETA = 10000.0
N_EXPERTS = 32
TOP_K = 4
D_EXPERT = 2048
SWIGLU_ALPHA = 1.702
SWIGLU_LIMIT = 7.0
RMS_EPS = 1e-6
LOG2_E = 1.0 / math.log(2.0)

LANES = 128
QK_PAD = 256
P_COLS = D_SSM + CONV_DIM + KV_LORA + Q_LORA
MISC_COLS = 3 * LANES
VMEM_LIMIT = 56 * 1024 * 1024

TM_IN = 512
TM_PREP = 512
TQ = 256
TM_OUT = 256
MOE_BLK = 512
MOE_TC = 1024
TD = 256


def _cparams(*sem):
    return pltpu.CompilerParams(dimension_semantics=sem, vmem_limit_bytes=VMEM_LIMIT)


def _silu(x):
    return x / (1.0 + jnp.exp(-x))


def _softplus(x):
    return jnp.maximum(x, 0.0) + jnp.log(1.0 + jnp.exp(-jnp.abs(x)))


def _split3(x):
    x1 = x.astype(BF16)
    r = x - x1.astype(F32)
    x2 = r.astype(BF16)
    x3 = (r - x2.astype(F32)).astype(BF16)
    return x1, x2, x3


def _dot(a, b):
    return jnp.dot(a, b, preferred_element_type=F32)


def _dot_nt(a, b):
    return lax.dot_general(a, b, (((1,), (1,)), ((), ())), preferred_element_type=F32)


def _dot_exact_lhs(x, m, passes):
    parts = _split3(x)[:passes]
    acc = _dot(parts[0], m)
    for p in parts[1:]:
        acc = acc + _dot(p, m)
    return acc


def _in_proj_kernel(x_ref, nw_ref, w_ref, wm_ref, p_ref, m_ref):
    x = x_ref[...]
    ms = jnp.mean(x * x, axis=-1, keepdims=True)
    xn = (x * lax.rsqrt(ms + RMS_EPS) * nw_ref[...]).astype(BF16)
    m_ref[...] = _dot(xn, wm_ref[...])
    p_ref[...] = _dot(xn, w_ref[...]).astype(BF16)


def _resident(shape):
    return pl.BlockSpec(shape, lambda *_: (0, 0), pipeline_mode=pl.Buffered(1))


def _in_proj(x2, norm_w, w_main, w_misc):
    n = x2.shape[0]
    return pl.pallas_call(
        _in_proj_kernel,
        grid=(n // TM_IN,),
        in_specs=[
            pl.BlockSpec((TM_IN, D_MODEL), lambda i: (i, 0)),
            _resident((1, D_MODEL)),
            _resident((D_MODEL, P_COLS)),
            _resident((D_MODEL, MISC_COLS)),
        ],
        out_specs=[
            pl.BlockSpec((TM_IN, P_COLS), lambda i: (i, 0)),
            pl.BlockSpec((TM_IN, MISC_COLS), lambda i: (i, 0)),
        ],
        out_shape=[
            jax.ShapeDtypeStruct((n, P_COLS), BF16),
            jax.ShapeDtypeStruct((n, MISC_COLS), F32),
        ],
        compiler_params=_cparams("parallel"),
        name="in_proj",
    )(x2, norm_w, w_main, w_misc)


def _conv_kernel(u_ref, w_ref, b_ref, o_ref):
    u = u_ref[0].astype(F32)
    s = u.shape[0]
    row = lax.broadcasted_iota(jnp.int32, u.shape, 0)
    pad = D_CONV // 2
    acc = u * w_ref[pad:pad + 1, :] + b_ref[...]
    for k in range(D_CONV):
        off = k - pad
        if off == 0:
            continue
        shifted = pltpu.roll(u, (-off) % s, 0)
        valid = (row + off >= 0) & (row + off < s)
        acc = acc + jnp.where(valid, shifted, 0.0) * w_ref[k:k + 1, :]
    o_ref[0] = _silu(acc).astype(BF16)


def _conv(p3, conv_w, conv_b):
    b, s, _ = p3.shape
    first = D_SSM // LANES
    return pl.pallas_call(
        _conv_kernel,
        grid=(b, CONV_DIM // LANES),
        in_specs=[
            pl.BlockSpec((1, s, LANES), lambda i, c: (i, 0, first + c)),
            pl.BlockSpec((D_CONV, LANES), lambda i, c: (0, c)),
            pl.BlockSpec((1, LANES), lambda i, c: (0, c)),
        ],
        out_specs=pl.BlockSpec((1, s, LANES), lambda i, c: (i, 0, c)),
        out_shape=jax.ShapeDtypeStruct((b, s, CONV_DIM), BF16),
        compiler_params=_cparams("parallel", "parallel"),
        name="conv",
    )(p3, conv_w, conv_b)


def _ssd_direction(xc, dt_e, cs, cs_e, tot_row, state_ref, d, reverse):
    L = xc.shape[0]
    gw = D_SSM // SSM_GROUPS
    x = xc[:, :D_SSM].astype(F32)
    xdt = x * dt_e
    xdt_b = xdt.astype(BF16)
    tot_e = cs_e[tot_row:tot_row + 1, :]
    decay_out = jnp.exp(cs_e)
    xw = (xdt * jnp.exp(tot_e - cs_e)).astype(BF16)
    chunk_decay = jnp.exp(tot_e)

    cs_t = cs.T
    li = lax.broadcasted_iota(jnp.int32, (L, L), 0)
    si = lax.broadcasted_iota(jnp.int32, (L, L), 1)
    keep = (si >= li) if reverse else (si <= li)
    lane = lax.broadcasted_iota(jnp.int32, (L, LANES), 1)

    ys = []
    for g in range(SSM_GROUPS):
        bm = xc[:, D_SSM + g * D_STATE:D_SSM + (g + 1) * D_STATE]
        cm = xc[:, D_SSM + SSM_GROUPS * D_STATE + g * D_STATE:D_SSM + SSM_GROUPS * D_STATE + (g + 1) * D_STATE]
        cb = _dot_nt(cm, bm)
        st = state_ref[d, g]
        y_off = _dot(cm, st.astype(BF16)) * decay_out[:, g * gw:(g + 1) * gw]
        bm_t = bm.astype(F32).T.astype(BF16)
        state_ref[d, g] = st * chunk_decay[:, g * gw:(g + 1) * gw] + _dot(bm_t, xw[:, g * gw:(g + 1) * gw])

        pieces = []
        heads_per_group = SSM_HEADS // SSM_GROUPS
        for j in range(heads_per_group // 2):
            scores = []
            for hh in range(2):
                h = g * heads_per_group + 2 * j + hh
                hl = SSM_HEADS * d + h
                col = jnp.broadcast_to(cs[:, hl:hl + 1], (L, L))
                rowv = jnp.broadcast_to(cs_t[hl:hl + 1, :], (L, L))
                dec = jnp.exp(jnp.where(keep, col - rowv, -1e30))
                scores.append((cb * dec).astype(BF16))
            lhs = jnp.concatenate(scores, axis=1)
            c0 = g * gw + j * LANES
            xp = xdt_b[:, c0:c0 + LANES]
            zero = jnp.zeros_like(xp)
            rhs = jnp.concatenate([jnp.where(lane < SSM_HEAD_DIM, xp, zero),
                                   jnp.where(lane >= SSM_HEAD_DIM, xp, zero)], axis=0)
            pieces.append(_dot(lhs, rhs))
        ys.append(jnp.concatenate(pieces, axis=1) + y_off)
    return jnp.concatenate(ys, axis=1)


def _ssd_kernel(xf_ref, xb_ref, mf_ref, mb_ref, par_ref, yf_ref, yb_ref, state_ref):
    @pl.when(pl.program_id(1) == 0)
    def _():
        state_ref[...] = jnp.zeros_like(state_ref)

    L = CHUNK
    par = par_ref[...]
    lane = lax.broadcasted_iota(jnp.int32, (1, LANES), 1)
    active = lane < 2 * SSM_HEADS
    a_coef = jnp.where(active, -jnp.exp(par[1:2, :]), 0.0)
    r = lax.broadcasted_iota(jnp.int32, (L, L), 0)
    c = lax.broadcasted_iota(jnp.int32, (L, L), 1)
    tri_lo = jnp.where(c <= r, 1.0, 0.0).astype(BF16)
    tri_up = jnp.where(c >= r, 1.0, 0.0).astype(BF16)
    er = lax.broadcasted_iota(jnp.int32, (LANES, D_SSM), 0)
    ec = lax.broadcasted_iota(jnp.int32, (LANES, D_SSM), 1)

    for d, (x_ref, m_ref, y_ref) in enumerate(((xf_ref, mf_ref, yf_ref), (xb_ref, mb_ref, yb_ref))):
        reverse = d == 1
        expand = jnp.where(er == ec // SSM_HEAD_DIM + SSM_HEADS * d, 1.0, 0.0).astype(BF16)
        dt = jnp.where(active, _softplus(m_ref[0] + par[0:1, :]), 0.0)
        a = dt * a_coef
        a1, a2, a3 = _split3(a)
        tri = tri_up if reverse else tri_lo
        cs = _dot(tri, a1) + _dot(tri, a2) + _dot(tri, a3)
        dt_e = _dot_exact_lhs(dt, expand, 2)
        cs_e = _dot_exact_lhs(cs, expand, 3)
        y = _ssd_direction(x_ref[0], dt_e, cs, cs_e, 0 if reverse else L - 1, state_ref, d, reverse)
        y_ref[0] = y


def _ssd(xc, misc3, par):
    b, s, _ = xc.shape
    nc = s // CHUNK
    dtb = MISC_COLS // LANES - 1
    gw = D_SSM // SSM_GROUPS
    return pl.pallas_call(
        _ssd_kernel,
        grid=(b, nc),
        in_specs=[
            pl.BlockSpec((1, CHUNK, CONV_DIM), lambda i, c: (i, c, 0)),
            pl.BlockSpec((1, CHUNK, CONV_DIM), lambda i, c: (i, nc - 1 - c, 0)),
            pl.BlockSpec((1, CHUNK, LANES), lambda i, c: (i, c, dtb)),
            pl.BlockSpec((1, CHUNK, LANES), lambda i, c: (i, nc - 1 - c, dtb)),
            pl.BlockSpec((8, LANES), lambda i, c: (0, 0)),
        ],
        out_specs=[
            pl.BlockSpec((1, CHUNK, D_SSM), lambda i, c: (i, c, 0)),
            pl.BlockSpec((1, CHUNK, D_SSM), lambda i, c: (i, nc - 1 - c, 0)),
        ],
        out_shape=[jax.ShapeDtypeStruct((b, s, D_SSM), F32)] * 2,
        scratch_shapes=[pltpu.VMEM((2, SSM_GROUPS, D_STATE, gw), F32)],
        compiler_params=_cparams("parallel", "arbitrary"),
        name="ssd",
    )(xc, xc, misc3, misc3, par)


def _rms(x, w):
    ms = jnp.mean(x * x, axis=-1, keepdims=True)
    return x * lax.rsqrt(ms + RMS_EPS) * w


def _mla_prep_kernel(qa_ref, kva_ref, misc_ref, pos_ref, freq_ref, qnw_ref, kvnw_ref,
                     wq_ref, wqr_ref, wkv_ref, q_ref, k_ref, v_ref):
    scale = (QK_NOPE + QK_ROPE) ** -0.5 * LOG2_E
    qn = _rms(qa_ref[...].astype(F32), qnw_ref[...]).astype(BF16)
    kvn = _rms(kva_ref[...].astype(F32), kvnw_ref[...]).astype(BF16)
    ang = pos_ref[...].astype(F32) * freq_ref[...]
    cos = jnp.cos(ang)
    sin = jnp.sin(ang)
    qm = _dot(qn, wq_ref[...])
    qr = _dot(qn, wqr_ref[...])
    kv = _dot(kvn, wkv_ref[...])
    misc = misc_ref[...]
    k_rope = (misc[:, :LANES] * cos + misc[:, LANES:2 * LANES] * sin).astype(BF16)
    for h in range(MLA_HEADS):
        q_ref[0, h, :, :LANES] = (qm[:, h * QK_PAD:h * QK_PAD + LANES] * scale).astype(BF16)
        q_rope = qm[:, h * QK_PAD + LANES:(h + 1) * QK_PAD] * cos + qr[:, h * LANES:(h + 1) * LANES] * sin
        q_ref[0, h, :, LANES:] = (q_rope * scale).astype(BF16)
        k_ref[0, h, :, :LANES] = kv[:, h * 2 * LANES:h * 2 * LANES + LANES].astype(BF16)
        k_ref[0, h, :, LANES:] = k_rope
        v_ref[0, h] = kv[:, h * 2 * LANES + LANES:(h + 1) * 2 * LANES].astype(BF16)


def _mla_prep(p2, misc2, pos2, freq, qnw, kvnw, wq, wqr, wkv, b, s):
    n = p2.shape[0]
    spb = s // TM_PREP
    kv_blk = (D_SSM + CONV_DIM) // KV_LORA
    qa_blk = (D_SSM + CONV_DIM + KV_LORA) // Q_LORA
    out_idx = lambda i: (i // spb, 0, i % spb, 0)
    return pl.pallas_call(
        _mla_prep_kernel,
        grid=(n // TM_PREP,),
        in_specs=[
            pl.BlockSpec((TM_PREP, Q_LORA), lambda i: (i, qa_blk)),
            pl.BlockSpec((TM_PREP, KV_LORA), lambda i: (i, kv_blk)),
            pl.BlockSpec((TM_PREP, 2 * LANES), lambda i: (i, 0)),
            pl.BlockSpec((TM_PREP, 1), lambda i: (i, 0)),
            pl.BlockSpec((1, LANES), lambda i: (0, 0)),
            pl.BlockSpec((1, Q_LORA), lambda i: (0, 0)),
            pl.BlockSpec((1, KV_LORA), lambda i: (0, 0)),
            pl.BlockSpec((Q_LORA, MLA_HEADS * QK_PAD), lambda i: (0, 0)),
            pl.BlockSpec((Q_LORA, MLA_HEADS * LANES), lambda i: (0, 0)),
            pl.BlockSpec((KV_LORA, MLA_HEADS * 2 * LANES), lambda i: (0, 0)),
        ],
        out_specs=[
            pl.BlockSpec((1, MLA_HEADS, TM_PREP, QK_PAD), out_idx),
            pl.BlockSpec((1, MLA_HEADS, TM_PREP, QK_PAD), out_idx),
            pl.BlockSpec((1, MLA_HEADS, TM_PREP, V_DIM), out_idx),
        ],
        out_shape=[
            jax.ShapeDtypeStruct((b, MLA_HEADS, s, QK_PAD), BF16),
            jax.ShapeDtypeStruct((b, MLA_HEADS, s, QK_PAD), BF16),
            jax.ShapeDtypeStruct((b, MLA_HEADS, s, V_DIM), BF16),
        ],
        compiler_params=_cparams("parallel"),
        name="mla_prep",
    )(p2, p2, misc2, pos2, freq, qnw, kvnw, wq, wqr, wkv)


def _regroup_gate_lin(w_ref, o_ref):
    k = lax.broadcasted_iota(jnp.int32, (2 * LANES, 2 * LANES), 0)
    c = lax.broadcasted_iota(jnp.int32, (2 * LANES, 2 * LANES), 1)
    src = jnp.where(c < LANES, 2 * c, 2 * (c - LANES) + 1)
    perm = jnp.where(k == src, 1.0, 0.0).astype(BF16)
    for j in range(w_ref.shape[1] // (2 * LANES)):
        cols = slice(2 * LANES * j, 2 * LANES * (j + 1))
        o_ref[:, cols] = _dot(w_ref[:, cols].astype(BF16), perm).astype(BF16)


def _attn_kernel(q_ref, k_ref, v_ref, wgu_ref, wd_ref, o_ref, wgl_ref, wdb_ref):
    s = _dot_nt(q_ref[0, 0], k_ref[0, 0])
    m = jnp.max(s, axis=-1, keepdims=True)
    p = jnp.exp2(s - m)
    l = jnp.sum(p, axis=-1, keepdims=True)
    o = _dot(p.astype(BF16), v_ref[0, 0])
    o_ref[0] = (o / l).astype(BF16)
    _regroup_gate_lin(wgu_ref, wgl_ref)
    wdb_ref[...] = wd_ref[...].astype(BF16)


def _attention(q, k, v, w_gate_up, w_down):
    b, h, s, _ = q.shape
    nq = s // TQ
    steps = b * h * nq
    e, d, f2 = w_gate_up.shape
    f = w_down.shape[1]
    assert (e * d) % steps == 0 and (e * f) % steps == 0
    rows_gu, rows_d = e * d // steps, e * f // steps
    assert rows_gu % 16 == 0 and rows_d % 16 == 0 and rows_gu <= 512 and rows_d <= 512
    step = lambda i, j, t: ((i * h + j) * nq + t, 0)
    o, wgl, wdb = pl.pallas_call(
        _attn_kernel,
        grid=(b, h, nq),
        in_specs=[
            pl.BlockSpec((1, 1, TQ, QK_PAD), lambda i, j, t: (i, j, t, 0)),
            pl.BlockSpec((1, 1, s, QK_PAD), lambda i, j, t: (i, j, 0, 0)),
            pl.BlockSpec((1, 1, s, V_DIM), lambda i, j, t: (i, j, 0, 0)),
            pl.BlockSpec((rows_gu, f2), step),
            pl.BlockSpec((rows_d, d), step),
        ],
        out_specs=[
            pl.BlockSpec((1, TQ, V_DIM), lambda i, j, t: (i, t, j)),
            pl.BlockSpec((rows_gu, f2), step),
            pl.BlockSpec((rows_d, d), step),
        ],
        out_shape=[
            jax.ShapeDtypeStruct((b, s, h * V_DIM), BF16),
            jax.ShapeDtypeStruct((e * d, f2), BF16),
            jax.ShapeDtypeStruct((e * f, d), BF16),
        ],
        compiler_params=_cparams("parallel", "parallel", "arbitrary"),
        name="attn",
    )(q, k, v, w_gate_up.reshape(e * d, f2), w_down.reshape(e * f, d))
    return o, wgl.reshape(e, d, f2), wdb.reshape(e, f, d)


def _out_proj_kernel(x_ref, yf_ref, yb_ref, xs_ref, z_ref, o_ref, dsk_ref, gnw_ref, anw_ref,
                     wo_ref, n2w_ref, wr1_ref, wr2_ref, br_ref,
                     h_ref, xn_ref, ri_ref, rg_ref, cnt_ref, carry_ref):
    i = pl.program_id(0)

    @pl.when(i == 0)
    def _():
        carry_ref[...] = jnp.zeros_like(carry_ref)

    tm = x_ref.shape[0]
    y = yf_ref[...] + yb_ref[...] + xs_ref[...].astype(F32) * dsk_ref[...]
    y = y * _silu(z_ref[...].astype(F32))
    gs = D_SSM // SSM_GROUPS
    halves = []
    for g in range(SSM_GROUPS):
        yg = y[:, g * gs:(g + 1) * gs]
        halves.append(_rms(yg, gnw_ref[:, g * gs:(g + 1) * gs]).astype(BF16))
    y_att = _rms(o_ref[...].astype(F32), anw_ref[...]).astype(BF16)
    lhs = jnp.concatenate(halves + [y_att], axis=1)
    h1 = x_ref[...] + _dot(lhs, wo_ref[...])
    h_ref[...] = h1
    xn = _rms(h1, n2w_ref[...])
    xn_ref[...] = xn

    x1 = xn.astype(BF16)
    x2 = (xn - x1.astype(F32)).astype(BF16)
    logits = _dot(x1, wr1_ref[...]) + (_dot(x1, wr2_ref[...]) + _dot(x2, wr1_ref[...])) + br_ref[...]
    lane = lax.broadcasted_iota(jnp.int32, (tm, LANES), 1)
    neg = jnp.float32(-jnp.inf)
    work = jnp.where(lane < N_EXPERTS, logits, neg)
    vals, ids = [], []
    for _ in range(TOP_K):
        m = jnp.max(work, axis=-1, keepdims=True)
        idx = jnp.min(jnp.where(work == m, lane, LANES), axis=-1, keepdims=True)
        vals.append(m)
        ids.append(idx)
        work = jnp.where(lane == idx, neg, work)
    es = [jnp.exp(v - vals[0]) for v in vals]
    den = es[0] + es[1] + es[2] + es[3]
    sel = jnp.zeros((tm, LANES), F32)
    for idx in ids:
        sel = sel + jnp.where(lane == idx, 1.0, 0.0)
    r = lax.broadcasted_iota(jnp.int32, (tm, tm), 0)
    c = lax.broadcasted_iota(jnp.int32, (tm, tm), 1)
    strict = jnp.where(c < r, 1.0, 0.0).astype(BF16)
    cum = _dot(strict, sel.astype(BF16)) + carry_ref[0:1, :]
    ri = jnp.zeros((tm, LANES), jnp.int32)
    rg = jnp.zeros((tm, LANES), F32)
    for kk in range(TOP_K):
        rank = jnp.sum(jnp.where(lane == ids[kk], cum, 0.0), axis=-1, keepdims=True).astype(jnp.int32)
        ri = jnp.where(lane == kk, ids[kk], ri)
        ri = jnp.where(lane == TOP_K + kk, rank, ri)
        rg = jnp.where(lane == kk, es[kk] / den, rg)
    ri_ref[...] = ri
    rg_ref[...] = rg
    total = carry_ref[0:1, :] + jnp.sum(sel, axis=0, keepdims=True)
    carry_ref[...] = jnp.broadcast_to(total, carry_ref.shape)
    cnt_ref[...] = jnp.broadcast_to(total, cnt_ref.shape)


def _out_proj(x2, yf, yb, xc2, p2, o2, dsk, gnw, anw, wo, n2w, wr1, wr2, br):
    n = x2.shape[0]
    row = lambda i: (i, 0)
    fixed = lambda i: (0, 0)
    return pl.pallas_call(
        _out_proj_kernel,
        grid=(n // TM_OUT,),
        in_specs=[
            pl.BlockSpec((TM_OUT, D_MODEL), row),
            pl.BlockSpec((TM_OUT, D_SSM), row),
            pl.BlockSpec((TM_OUT, D_SSM), row),
            pl.BlockSpec((TM_OUT, D_SSM), row),
            pl.BlockSpec((TM_OUT, D_SSM), row),
            pl.BlockSpec((TM_OUT, MLA_HEADS * V_DIM), row),
            pl.BlockSpec((1, D_SSM), fixed),
            pl.BlockSpec((1, D_SSM), fixed),
            pl.BlockSpec((1, MLA_HEADS * V_DIM), fixed),
            pl.BlockSpec((D_MODEL, D_MODEL), fixed),
            pl.BlockSpec((1, D_MODEL), fixed),
            pl.BlockSpec((D_MODEL, LANES), fixed),
            pl.BlockSpec((D_MODEL, LANES), fixed),
            pl.BlockSpec((1, LANES), fixed),
        ],
        out_specs=[
            pl.BlockSpec((TM_OUT, D_MODEL), row),
            pl.BlockSpec((TM_OUT, D_MODEL), row),
            pl.BlockSpec((TM_OUT, LANES), row),
            pl.BlockSpec((TM_OUT, LANES), row),
            pl.BlockSpec((8, LANES), fixed),
        ],
        out_shape=[
            jax.ShapeDtypeStruct((n, D_MODEL), F32),
            jax.ShapeDtypeStruct((n, D_MODEL), F32),
            jax.ShapeDtypeStruct((n, LANES), jnp.int32),
            jax.ShapeDtypeStruct((n, LANES), F32),
            jax.ShapeDtypeStruct((8, LANES), F32),
        ],
        scratch_shapes=[pltpu.VMEM((8, LANES), F32)],
        compiler_params=_cparams("arbitrary"),
        name="out_proj",
    )(x2, yf, yb, xc2, p2, o2, dsk, gnw, anw, wo, n2w, wr1, wr2, br)


def _moe_kernel(be_ref, nu_ref, tok_ref, tokn_ref, pos_ref, x_hbm, wgl_ref, wd_ref, bgl_ref, bd_ref,
                y_hbm, xg_ref, xb_ref, acc_ref, yb_ref, gsem, ssem):
    del be_ref
    blk = pl.program_id(0)
    hc = pl.program_id(1)
    nblk = pl.num_programs(0)
    last = pl.num_programs(1) - 1
    nu = nu_ref[0]
    slot = lax.rem(blk, 2)

    def gather(src_row, s, j):
        return pltpu.make_async_copy(x_hbm.at[pl.ds(src_row, 1)], xg_ref.at[s, pl.ds(j, 1)], gsem.at[s])

    def scatter(j, dst_row):
        return pltpu.make_async_copy(yb_ref.at[pl.ds(j, 1)], y_hbm.at[pl.ds(dst_row, 1)], ssem)

    def start_gather(rows_ref, s):
        for j in range(MOE_BLK):
            gather(rows_ref[0, 0, j], s, j).start()

    def wait_gather(s):
        for _ in range(MOE_BLK):
            gather(0, s, 0).wait()

    def start_scatter():
        for j in range(MOE_BLK):
            scatter(j, pos_ref[0, 0, j]).start()

    def wait_scatter():
        for _ in range(MOE_BLK):
            scatter(0, 0).wait()

    @pl.when(blk < nu)
    def _():
        @pl.when(hc == 0)
        def _():
            @pl.when(blk == 0)
            def _():
                start_gather(tok_ref, 0)

            wait_gather(slot)
            xb_ref[...] = xg_ref[slot].astype(BF16)

            @pl.when(blk + 1 < nu)
            def _():
                start_gather(tokn_ref, 1 - slot)

        hdn = _dot(xb_ref[...], wgl_ref[0]) + bgl_ref[0]
        acts = []
        for j in range(MOE_TC // LANES):
            glu = jnp.minimum(hdn[:, 2 * LANES * j:2 * LANES * j + LANES], SWIGLU_LIMIT)
            lin = jnp.clip(hdn[:, 2 * LANES * j + LANES:2 * LANES * (j + 1)], -SWIGLU_LIMIT, SWIGLU_LIMIT)
            acts.append((glu / (1.0 + jnp.exp(-SWIGLU_ALPHA * glu)) * (lin + 1.0)).astype(BF16))
        part = _dot(jnp.concatenate(acts, axis=1), wd_ref[0])

        @pl.when(hc == 0)
        def _():
            acc_ref[...] = part + bd_ref[0]

        @pl.when((hc > 0) & (hc < last))
        def _():
            acc_ref[...] += part

        @pl.when(hc == last)
        def _():
            @pl.when(blk > 0)
            def _():
                wait_scatter()

            yb_ref[...] = acc_ref[...] + part
            start_scatter()

    @pl.when((blk >= nu) & (hc == last))
    def _():
        wait_scatter()
        yb_ref[...] = jnp.zeros_like(yb_ref)
        start_scatter()

    @pl.when((blk == nblk - 1) & (hc == last))
    def _():
        wait_scatter()


def _moe(block_expert, n_used, row_tok, row_pos, xn2, wgl, wd, bgl, bd):
    nb = row_tok.shape[0]
    nh = D_EXPERT // MOE_TC
    assert nh >= 2

    def bb(b, nu):
        return jnp.minimum(b, nu[0] - 1)

    def hh(b, h, nu):
        order = jnp.where(b % 2 == 0, h, nh - 1 - h)
        return jnp.where(b < nu[0], order, nh - 1 - (nu[0] - 1) % 2 * (nh - 1))

    smem_rows = lambda index_map: pl.BlockSpec((1, 1, MOE_BLK), index_map, memory_space=pltpu.SMEM)
    grid_spec = pltpu.PrefetchScalarGridSpec(
        num_scalar_prefetch=2,
        grid=(nb, nh),
        in_specs=[
            smem_rows(lambda b, h, be, nu: (b, 0, 0)),
            smem_rows(lambda b, h, be, nu: (jnp.minimum(b + 1, nb - 1), 0, 0)),
            smem_rows(lambda b, h, be, nu: (b, 0, 0)),
            pl.BlockSpec(memory_space=pl.ANY),
            pl.BlockSpec((1, D_MODEL, 2 * MOE_TC), lambda b, h, be, nu: (be[bb(b, nu)], 0, hh(b, h, nu))),
            pl.BlockSpec((1, MOE_TC, D_MODEL), lambda b, h, be, nu: (be[bb(b, nu)], hh(b, h, nu), 0)),
            pl.BlockSpec((1, 1, 2 * MOE_TC), lambda b, h, be, nu: (be[bb(b, nu)], 0, hh(b, h, nu))),
            pl.BlockSpec((1, 1, D_MODEL), lambda b, h, be, nu: (be[bb(b, nu)], 0, 0)),
        ],
        out_specs=pl.BlockSpec(memory_space=pl.ANY),
        scratch_shapes=[
            pltpu.VMEM((2, MOE_BLK, D_MODEL), F32),
            pltpu.VMEM((MOE_BLK, D_MODEL), BF16),
            pltpu.VMEM((MOE_BLK, D_MODEL), F32),
            pltpu.VMEM((MOE_BLK, D_MODEL), F32),
            pltpu.SemaphoreType.DMA((2,)),
            pltpu.SemaphoreType.DMA(()),
        ],
    )
    return pl.pallas_call(
        _moe_kernel,
        grid_spec=grid_spec,
        out_shape=jax.ShapeDtypeStruct((nb * MOE_BLK, D_MODEL), F32),
        compiler_params=_cparams("arbitrary", "arbitrary"),
        name="moe",
    )(block_expert, n_used, row_tok, row_tok, row_pos, xn2, wgl, wd, bgl, bd)


def _combine_kernel(y0_ref, y1_ref, y2_ref, y3_ref, h_ref, rg_ref, fw_ref, o_ref):
    rg = rg_ref[...]
    out = h_ref[...]
    for k, y_ref in enumerate((y0_ref, y1_ref, y2_ref, y3_ref)):
        out = out + y_ref[...] * rg[:, k:k + 1]
    o_ref[...] = _rms(out, fw_ref[...])


def _combine(y_rows, h1, rg, final_w):
    n = h1.shape[0]
    nt = n // TD
    row = lambda i: (i, 0)
    slot_spec = lambda k: pl.BlockSpec((TD, D_MODEL), lambda i: (k * nt + i, 0))
    return pl.pallas_call(
        _combine_kernel,
        grid=(nt,),
        in_specs=[slot_spec(k) for k in range(TOP_K)] + [
            pl.BlockSpec((TD, D_MODEL), row),
            pl.BlockSpec((TD, LANES), row),
            pl.BlockSpec((1, D_MODEL), lambda i: (0, 0)),
        ],
        out_specs=pl.BlockSpec((TD, D_MODEL), row),
        out_shape=jax.ShapeDtypeStruct((n, D_MODEL), F32),
        compiler_params=_cparams("parallel"),
        name="combine",
    )(y_rows, y_rows, y_rows, y_rows, h1, rg, final_w)


def _pad_cols(w, width):
    return jnp.pad(w, ((0, 0), (0, width - w.shape[1])))


def _rot_cols(w):
    half = w.shape[1] // 2
    return jnp.concatenate([-w[:, half:], w[:, :half]], axis=1)


def _layer(h, positions, norm1_w, w_in, conv_w, conv_b, dt_bias_f, dt_bias_b, a_log_f, a_log_b, d_skip,
           ssm_norm_w, q_a_norm_w, w_q_b, kv_a_norm_w, w_kv_b, attn_norm_w, w_out, norm2_w, w_router,
           b_router, w_gate_up, b_gate_up, w_down, b_down, final_norm_w):
    b, s, _ = h.shape
    n = b * s
    x2 = h.reshape(n, D_MODEL)

    o_z, o_xbc = 0, D_SSM
    o_dtf = o_xbc + CONV_DIM
    o_dtb = o_dtf + SSM_HEADS
    o_qa = o_dtb + SSM_HEADS
    o_kva = o_qa + Q_LORA
    o_kpe = o_kva + KV_LORA
    w_z = w_in[:, o_z:o_xbc]
    w_xbc = w_in[:, o_xbc:o_dtf]
    w_qa = w_in[:, o_qa:o_kva]
    w_kva = w_in[:, o_kva:o_kpe]
    w_kpe = w_in[:, o_kpe:o_kpe + QK_ROPE]
    w_dt = w_in[:, o_dtf:o_qa]
    w_main = jnp.concatenate([w_z, w_xbc, w_kva, w_qa], axis=1).astype(BF16)
    w_misc = jnp.concatenate([_pad_cols(w_kpe, LANES), _pad_cols(_rot_cols(w_kpe), LANES),
                              _pad_cols(w_dt, LANES)], axis=1).astype(BF16)

    p2, misc2 = _in_proj(x2, norm1_w.reshape(1, D_MODEL), w_main, w_misc)

    xc = _conv(p2.reshape(b, s, P_COLS), conv_w, conv_b.reshape(1, CONV_DIM))
    par = jnp.zeros((8, LANES), F32)
    par = par.at[0, :2 * SSM_HEADS].set(jnp.concatenate([dt_bias_f, dt_bias_b]))
    par = par.at[1, :2 * SSM_HEADS].set(jnp.concatenate([a_log_f, a_log_b]))
    y_f, y_b = _ssd(xc, misc2.reshape(b, s, MISC_COLS), par)

    inv_freq = ROPE_THETA ** (-jnp.arange(0, QK_ROPE, 2, dtype=F32) / QK_ROPE)
    freq = _pad_cols(jnp.concatenate([inv_freq, inv_freq])[None, :], LANES)
    wq3 = w_q_b.reshape(Q_LORA, MLA_HEADS, QK_NOPE + QK_ROPE)
    wq_rope = wq3[:, :, QK_NOPE:]
    wq = jnp.pad(wq3, ((0, 0), (0, 0), (0, QK_PAD - QK_NOPE - QK_ROPE))).reshape(Q_LORA, MLA_HEADS * QK_PAD)
    wq_rot = jnp.concatenate([-wq_rope[:, :, QK_ROPE // 2:], wq_rope[:, :, :QK_ROPE // 2]], axis=2)
    wqr = jnp.pad(wq_rot, ((0, 0), (0, 0), (0, LANES - QK_ROPE))).reshape(Q_LORA, MLA_HEADS * LANES)
    q, k, v = _mla_prep(p2, misc2, positions.reshape(n, 1), freq, q_a_norm_w.reshape(1, Q_LORA),
                        kv_a_norm_w.reshape(1, KV_LORA), wq.astype(BF16), wqr.astype(BF16),
                        w_kv_b.astype(BF16), b, s)
    o, wgl, wdb = _attention(q, k, v, w_gate_up, w_down)

    dsk = jnp.repeat(d_skip, SSM_HEAD_DIM)[None, :]
    wr = _pad_cols(w_router, LANES)
    wr1 = wr.astype(BF16)
    wr2 = (wr - wr1.astype(F32)).astype(BF16)
    br = _pad_cols(b_router[None, :], LANES)
    h1, xn2, ri, rg, cnt = _out_proj(
        x2, y_f.reshape(n, D_SSM), y_b.reshape(n, D_SSM), xc.reshape(n, CONV_DIM), p2, o.reshape(n, MLA_HEADS * V_DIM),
        dsk, ssm_norm_w.reshape(1, D_SSM), attn_norm_w.reshape(1, MLA_HEADS * V_DIM), w_out.astype(BF16),
        norm2_w.reshape(1, D_MODEL), wr1, wr2, br)

    counts = cnt[0, :N_EXPERTS].astype(jnp.int32)
    nblk = (counts + MOE_BLK - 1) // MOE_BLK
    blk_end = jnp.cumsum(nblk)
    row_start = (blk_end - nblk) * MOE_BLK
    n_blocks = (n * TOP_K) // MOE_BLK + N_EXPERTS
    n_rows = n_blocks * MOE_BLK
    ids = ri[:, :TOP_K]
    dest = (row_start[ids] + ri[:, TOP_K:2 * TOP_K]).reshape(n * TOP_K)
    block_ids = jnp.arange(n_blocks, dtype=jnp.int32)
    block_expert = jnp.minimum(jnp.sum((blk_end[None, :] <= block_ids[:, None]).astype(jnp.int32), axis=1),
                               N_EXPERTS - 1)
    n_used = blk_end[-1:].astype(jnp.int32)

    assign = (jnp.arange(TOP_K, dtype=jnp.int32)[None, :] * n + jnp.arange(n, dtype=jnp.int32)[:, None])
    inv = jnp.full((n_rows,), -1, jnp.int32).at[dest].set(assign.reshape(n * TOP_K), unique_indices=True)
    is_pad = inv < 0
    pad_pos = n * TOP_K + jnp.cumsum(is_pad.astype(jnp.int32)) - 1
    row_pos = jnp.where(is_pad, pad_pos, inv).reshape(n_blocks, 1, MOE_BLK)
    row_tok = jnp.where(is_pad, 0, inv % n).reshape(n_blocks, 1, MOE_BLK)

    bgl = b_gate_up.reshape(N_EXPERTS, D_EXPERT // LANES, LANES, 2).swapaxes(2, 3).reshape(N_EXPERTS, 1, 2 * D_EXPERT)
    y_rows = _moe(block_expert, n_used, row_tok, row_pos, xn2, wgl, wdb, bgl, b_down[:, None, :])

    out = _combine(y_rows, h1, rg, final_norm_w.reshape(1, D_MODEL))
    return out.reshape(b, s, D_MODEL)


def kernel(x, positions, norm1_w, w_in, conv_w, conv_b, dt_bias_f, dt_bias_b, a_log_f, a_log_b, d_skip,
           ssm_norm_w, q_a_norm_w, w_q_b, kv_a_norm_w, w_kv_b, attn_norm_w, w_out, norm2_w, w_router, b_router,
           w_gate_up, b_gate_up, w_down, b_down, final_norm_w):
    depth = norm1_w.shape[0]
    assert depth == 1, "the final norm is fused into the single layer's combine step"
    return _layer(x, positions, norm1_w[0], w_in[0], conv_w[0], conv_b[0], dt_bias_f[0], dt_bias_b[0],
                  a_log_f[0], a_log_b[0], d_skip[0], ssm_norm_w[0], q_a_norm_w[0], w_q_b[0], kv_a_norm_w[0],
                  w_kv_b[0], attn_norm_w[0], w_out[0], norm2_w[0], w_router[0], b_router[0], w_gate_up[0],
                  b_gate_up[0], w_down[0], b_down[0], final_norm_w)
```

```python
import functools
import math

import jax
import jax.numpy as jnp
from jax import lax
from jax.experimental import pallas as pl
from jax.experimental.pallas import tpu as pltpu

F32 = jnp.float32
BF16 = jnp.bfloat16

D_MODEL = 2048
D_SSM = 1024
SSM_HEAD_DIM = 64
SSM_HEADS = 16
SSM_GROUPS = 2
D_STATE = 128
D_CONV = 5
CONV_DIM = D_SSM + 2 * SSM_GROUPS * D_STATE
CHUNK = 128
MLA_HEADS = 8
QK_NOPE = 128
QK_ROPE = 64
V_DIM = 128
Q_LORA = 768
KV_LORA = 512
ROPE_THETA = 10000.0
N_EXPERTS = 32
TOP_K = 4
D_EXPERT = 2048
SWIGLU_ALPHA = 1.702
SWIGLU_LIMIT = 7.0
RMS_EPS = 1e-6
LOG2_E = 1.0 / math.log(2.0)

LANES = 128
QK_PAD = 256
P_COLS = D_SSM + CONV_DIM + KV_LORA + Q_LORA
MISC_COLS = 3 * LANES
VMEM_LIMIT = 56 * 1024 * 1024

TM_IN = 512
TM_PREP = 512
TQ = 256
TM_OUT = 256
MOE_BLK = 512
MOE_TC = 1024
TD = 256


def _cparams(*sem):
    return pltpu.CompilerParams(dimension_semantics=sem, vmem_limit_bytes=VMEM_LIMIT)


def _silu(x):
    return x / (1.0 + jnp.exp(-x))


def _softplus(x):
    return jnp.maximum(x, 0.0) + jnp.log(1.0 + jnp.exp(-jnp.abs(x)))


def _split3(x):
    x1 = x.astype(BF16)
    r = x - x1.astype(F32)
    x2 = r.astype(BF16)
    x3 = (r - x2.astype(F32)).astype(BF16)
    return x1, x2, x3


def _dot(a, b):
    return jnp.dot(a, b, preferred_element_type=F32)


def _dot_nt(a, b):
    return lax.dot_general(a, b, (((1,), (1,)), ((), ())), preferred_element_type=F32)


def _dot_exact_lhs(x, m, passes):
    parts = _split3(x)[:passes]
    acc = _dot(parts[0], m)
    for p in parts[1:]:
        acc = acc + _dot(p, m)
    return acc


def _in_proj_kernel(x_ref, nw_ref, w_ref, wm_ref, p_ref, m_ref):
    x = x_ref[...]
    ms = jnp.mean(x * x, axis=-1, keepdims=True)
    xn = (x * lax.rsqrt(ms + RMS_EPS) * nw_ref[...]).astype(BF16)
    m_ref[...] = _dot(xn, wm_ref[...])
    p_ref[...] = _dot(xn, w_ref[...]).astype(BF16)


def _resident(shape):
    return pl.BlockSpec(shape, lambda *_: (0, 0), pipeline_mode=pl.Buffered(1))


def _in_proj(x2, norm_w, w_main, w_misc):
    n = x2.shape[0]
    return pl.pallas_call(
        _in_proj_kernel,
        grid=(n // TM_IN,),
        in_specs=[
            pl.BlockSpec((TM_IN, D_MODEL), lambda i: (i, 0)),
            _resident((1, D_MODEL)),
            _resident((D_MODEL, P_COLS)),
            _resident((D_MODEL, MISC_COLS)),
        ],
        out_specs=[
            pl.BlockSpec((TM_IN, P_COLS), lambda i: (i, 0)),
            pl.BlockSpec((TM_IN, MISC_COLS), lambda i: (i, 0)),
        ],
        out_shape=[
            jax.ShapeDtypeStruct((n, P_COLS), BF16),
            jax.ShapeDtypeStruct((n, MISC_COLS), F32),
        ],
        compiler_params=_cparams("parallel"),
        name="in_proj",
    )(x2, norm_w, w_main, w_misc)


def _conv_kernel(u_ref, w_ref, b_ref, o_ref):
    u = u_ref[0].astype(F32)
    s = u.shape[0]
    row = lax.broadcasted_iota(jnp.int32, u.shape, 0)
    pad = D_CONV // 2
    acc = u * w_ref[pad:pad + 1, :] + b_ref[...]
    for k in range(D_CONV):
        off = k - pad
        if off == 0:
            continue
        shifted = pltpu.roll(u, (-off) % s, 0)
        valid = (row + off >= 0) & (row + off < s)
        acc = acc + jnp.where(valid, shifted, 0.0) * w_ref[k:k + 1, :]
    o_ref[0] = _silu(acc).astype(BF16)


def _conv(p3, conv_w, conv_b):
    b, s, _ = p3.shape
    first = D_SSM // LANES
    return pl.pallas_call(
        _conv_kernel,
        grid=(b, CONV_DIM // LANES),
        in_specs=[
            pl.BlockSpec((1, s, LANES), lambda i, c: (i, 0, first + c)),
            pl.BlockSpec((D_CONV, LANES), lambda i, c: (0, c)),
            pl.BlockSpec((1, LANES), lambda i, c: (0, c)),
        ],
        out_specs=pl.BlockSpec((1, s, LANES), lambda i, c: (i, 0, c)),
        out_shape=jax.ShapeDtypeStruct((b, s, CONV_DIM), BF16),
        compiler_params=_cparams("parallel", "parallel"),
        name="conv",
    )(p3, conv_w, conv_b)


def _ssd_direction(xc, dt_e, cs, cs_e, tot_row, state_ref, d, reverse):
    L = xc.shape[0]
    gw = D_SSM // SSM_GROUPS
    x = xc[:, :D_SSM].astype(F32)
    xdt = x * dt_e
    xdt_b = xdt.astype(BF16)
    tot_e = cs_e[tot_row:tot_row + 1, :]
    decay_out = jnp.exp(cs_e)
    xw = (xdt * jnp.exp(tot_e - cs_e)).astype(BF16)
    chunk_decay = jnp.exp(tot_e)

    cs_t = cs.T
    li = lax.broadcasted_iota(jnp.int32, (L, L), 0)
    si = lax.broadcasted_iota(jnp.int32, (L, L), 1)
    keep = (si >= li) if reverse else (si <= li)
    lane = lax.broadcasted_iota(jnp.int32, (L, LANES), 1)

    ys = []
    for g in range(SSM_GROUPS):
        bm = xc[:, D_SSM + g * D_STATE:D_SSM + (g + 1) * D_STATE]
        cm = xc[:, D_SSM + SSM_GROUPS * D_STATE + g * D_STATE:D_SSM + SSM_GROUPS * D_STATE + (g + 1) * D_STATE]
        cb = _dot_nt(cm, bm)
        st = state_ref[d, g]
        y_off = _dot(cm, st.astype(BF16)) * decay_out[:, g * gw:(g + 1) * gw]
        bm_t = bm.astype(F32).T.astype(BF16)
        state_ref[d, g] = st * chunk_decay[:, g * gw:(g + 1) * gw] + _dot(bm_t, xw[:, g * gw:(g + 1) * gw])

        pieces = []
        heads_per_group = SSM_HEADS // SSM_GROUPS
        for j in range(heads_per_group // 2):
            scores = []
            for hh in range(2):
                h = g * heads_per_group + 2 * j + hh
                hl = SSM_HEADS * d + h
                col = jnp.broadcast_to(cs[:, hl:hl + 1], (L, L))
                rowv = jnp.broadcast_to(cs_t[hl:hl + 1, :], (L, L))
                dec = jnp.exp(jnp.where(keep, col - rowv, -1e30))
                scores.append((cb * dec).astype(BF16))
            lhs = jnp.concatenate(scores, axis=1)
            c0 = g * gw + j * LANES
            xp = xdt_b[:, c0:c0 + LANES]
            zero = jnp.zeros_like(xp)
            rhs = jnp.concatenate([jnp.where(lane < SSM_HEAD_DIM, xp, zero),
                                   jnp.where(lane >= SSM_HEAD_DIM, xp, zero)], axis=0)
            pieces.append(_dot(lhs, rhs))
        ys.append(jnp.concatenate(pieces, axis=1) + y_off)
    return jnp.concatenate(ys, axis=1)


def _ssd_kernel(xf_ref, xb_ref, mf_ref, mb_ref, par_ref, yf_ref, yb_ref, state_ref):
    @pl.when(pl.program_id(1) == 0)
    def _():
        state_ref[...] = jnp.zeros_like(state_ref)

    L = CHUNK
    par = par_ref[...]
    lane = lax.broadcasted_iota(jnp.int32, (1, LANES), 1)
    active = lane < 2 * SSM_HEADS
    a_coef = jnp.where(active, -jnp.exp(par[1:2, :]), 0.0)
    r = lax.broadcasted_iota(jnp.int32, (L, L), 0)
    c = lax.broadcasted_iota(jnp.int32, (L, L), 1)
    tri_lo = jnp.where(c <= r, 1.0, 0.0).astype(BF16)
    tri_up = jnp.where(c >= r, 1.0, 0.0).astype(BF16)
    er = lax.broadcasted_iota(jnp.int32, (LANES, D_SSM), 0)
    ec = lax.broadcasted_iota(jnp.int32, (LANES, D_SSM), 1)

    for d, (x_ref, m_ref, y_ref) in enumerate(((xf_ref, mf_ref, yf_ref), (xb_ref, mb_ref, yb_ref))):
        reverse = d == 1
        expand = jnp.where(er == ec // SSM_HEAD_DIM + SSM_HEADS * d, 1.0, 0.0).astype(BF16)
        dt = jnp.where(active, _softplus(m_ref[0] + par[0:1, :]), 0.0)
        a = dt * a_coef
        a1, a2, a3 = _split3(a)
        tri = tri_up if reverse else tri_lo
        cs = _dot(tri, a1) + _dot(tri, a2) + _dot(tri, a3)
        dt_e = _dot_exact_lhs(dt, expand, 2)
        cs_e = _dot_exact_lhs(cs, expand, 3)
        y = _ssd_direction(x_ref[0], dt_e, cs, cs_e, 0 if reverse else L - 1, state_ref, d, reverse)
        y_ref[0] = y


def _ssd(xc, misc3, par):
    b, s, _ = xc.shape
    nc = s // CHUNK
    dtb = MISC_COLS // LANES - 1
    gw = D_SSM // SSM_GROUPS
    return pl.pallas_call(
        _ssd_kernel,
        grid=(b, nc),
        in_specs=[
            pl.BlockSpec((1, CHUNK, CONV_DIM), lambda i, c: (i, c, 0)),
            pl.BlockSpec((1, CHUNK, CONV_DIM), lambda i, c: (i, nc - 1 - c, 0)),
            pl.BlockSpec((1, CHUNK, LANES), lambda i, c: (i, c, dtb)),
            pl.BlockSpec((1, CHUNK, LANES), lambda i, c: (i, nc - 1 - c, dtb)),
            pl.BlockSpec((8, LANES), lambda i, c: (0, 0)),
        ],
        out_specs=[
            pl.BlockSpec((1, CHUNK, D_SSM), lambda i, c: (i, c, 0)),
            pl.BlockSpec((1, CHUNK, D_SSM), lambda i, c: (i, nc - 1 - c, 0)),
        ],
        out_shape=[jax.ShapeDtypeStruct((b, s, D_SSM), F32)] * 2,
        scratch_shapes=[pltpu.VMEM((2, SSM_GROUPS, D_STATE, gw), F32)],
        compiler_params=_cparams("parallel", "arbitrary"),
        name="ssd",
    )(xc, xc, misc3, misc3, par)


def _rms(x, w):
    ms = jnp.mean(x * x, axis=-1, keepdims=True)
    return x * lax.rsqrt(ms + RMS_EPS) * w


def _mla_prep_kernel(qa_ref, kva_ref, misc_ref, pos_ref, freq_ref, qnw_ref, kvnw_ref,
                     wq_ref, wqr_ref, wkv_ref, q_ref, k_ref, v_ref):
    scale = (QK_NOPE + QK_ROPE) ** -0.5 * LOG2_E
    qn = _rms(qa_ref[...].astype(F32), qnw_ref[...]).astype(BF16)
    kvn = _rms(kva_ref[...].astype(F32), kvnw_ref[...]).astype(BF16)
    ang = pos_ref[...].astype(F32) * freq_ref[...]
    cos = jnp.cos(ang)
    sin = jnp.sin(ang)
    qm = _dot(qn, wq_ref[...])
    qr = _dot(qn, wqr_ref[...])
    kv = _dot(kvn, wkv_ref[...])
    misc = misc_ref[...]
    k_rope = (misc[:, :LANES] * cos + misc[:, LANES:2 * LANES] * sin).astype(BF16)
    for h in range(MLA_HEADS):
        q_ref[0, h, :, :LANES] = (qm[:, h * QK_PAD:h * QK_PAD + LANES] * scale).astype(BF16)
        q_rope = qm[:, h * QK_PAD + LANES:(h + 1) * QK_PAD] * cos + qr[:, h * LANES:(h + 1) * LANES] * sin
        q_ref[0, h, :, LANES:] = (q_rope * scale).astype(BF16)
        k_ref[0, h, :, :LANES] = kv[:, h * 2 * LANES:h * 2 * LANES + LANES].astype(BF16)
        k_ref[0, h, :, LANES:] = k_rope
        v_ref[0, h] = kv[:, h * 2 * LANES + LANES:(h + 1) * 2 * LANES].astype(BF16)


def _mla_prep(p2, misc2, pos2, freq, qnw, kvnw, wq, wqr, wkv, b, s):
    n = p2.shape[0]
    spb = s // TM_PREP
    kv_blk = (D_SSM + CONV_DIM) // KV_LORA
    qa_blk = (D_SSM + CONV_DIM + KV_LORA) // Q_LORA
    out_idx = lambda i: (i // spb, 0, i % spb, 0)
    return pl.pallas_call(
        _mla_prep_kernel,
        grid=(n // TM_PREP,),
        in_specs=[
            pl.BlockSpec((TM_PREP, Q_LORA), lambda i: (i, qa_blk)),
            pl.BlockSpec((TM_PREP, KV_LORA), lambda i: (i, kv_blk)),
            pl.BlockSpec((TM_PREP, 2 * LANES), lambda i: (i, 0)),
            pl.BlockSpec((TM_PREP, 1), lambda i: (i, 0)),
            pl.BlockSpec((1, LANES), lambda i: (0, 0)),
            pl.BlockSpec((1, Q_LORA), lambda i: (0, 0)),
            pl.BlockSpec((1, KV_LORA), lambda i: (0, 0)),
            pl.BlockSpec((Q_LORA, MLA_HEADS * QK_PAD), lambda i: (0, 0)),
            pl.BlockSpec((Q_LORA, MLA_HEADS * LANES), lambda i: (0, 0)),
            pl.BlockSpec((KV_LORA, MLA_HEADS * 2 * LANES), lambda i: (0, 0)),
        ],
        out_specs=[
            pl.BlockSpec((1, MLA_HEADS, TM_PREP, QK_PAD), out_idx),
            pl.BlockSpec((1, MLA_HEADS, TM_PREP, QK_PAD), out_idx),
            pl.BlockSpec((1, MLA_HEADS, TM_PREP, V_DIM), out_idx),
        ],
        out_shape=[
            jax.ShapeDtypeStruct((b, MLA_HEADS, s, QK_PAD), BF16),
            jax.ShapeDtypeStruct((b, MLA_HEADS, s, QK_PAD), BF16),
            jax.ShapeDtypeStruct((b, MLA_HEADS, s, V_DIM), BF16),
        ],
        compiler_params=_cparams("parallel"),
        name="mla_prep",
    )(p2, p2, misc2, pos2, freq, qnw, kvnw, wq, wqr, wkv)


def _regroup_gate_lin(w_ref, o_ref):
    k = lax.broadcasted_iota(jnp.int32, (2 * LANES, 2 * LANES), 0)
    c = lax.broadcasted_iota(jnp.int32, (2 * LANES, 2 * LANES), 1)
    src = jnp.where(c < LANES, 2 * c, 2 * (c - LANES) + 1)
    perm = jnp.where(k == src, 1.0, 0.0).astype(BF16)
    for j in range(w_ref.shape[1] // (2 * LANES)):
        cols = slice(2 * LANES * j, 2 * LANES * (j + 1))
        o_ref[:, cols] = _dot(w_ref[:, cols].astype(BF16), perm).astype(BF16)


def _attn_kernel(q_ref, k_ref, v_ref, wgu_ref, wd_ref, o_ref, wgl_ref, wdb_ref):
    s = _dot_nt(q_ref[0, 0], k_ref[0, 0])
    m = jnp.max(s, axis=-1, keepdims=True)
    p = jnp.exp2(s - m)
    l = jnp.sum(p, axis=-1, keepdims=True)
    o = _dot(p.astype(BF16), v_ref[0, 0])
    o_ref[0] = (o / l).astype(BF16)
    _regroup_gate_lin(wgu_ref, wgl_ref)
    wdb_ref[...] = wd_ref[...].astype(BF16)


def _attention(q, k, v, w_gate_up, w_down):
    b, h, s, _ = q.shape
    nq = s // TQ
    steps = b * h * nq
    e, d, f2 = w_gate_up.shape
    f = w_down.shape[1]
    assert (e * d) % steps == 0 and (e * f) % steps == 0
    rows_gu, rows_d = e * d // steps, e * f // steps
    assert rows_gu % 16 == 0 and rows_d % 16 == 0 and rows_gu <= 512 and rows_d <= 512
    step = lambda i, j, t: ((i * h + j) * nq + t, 0)
    o, wgl, wdb = pl.pallas_call(
        _attn_kernel,
        grid=(b, h, nq),
        in_specs=[
            pl.BlockSpec((1, 1, TQ, QK_PAD), lambda i, j, t: (i, j, t, 0)),
            pl.BlockSpec((1, 1, s, QK_PAD), lambda i, j, t: (i, j, 0, 0)),
            pl.BlockSpec((1, 1, s, V_DIM), lambda i, j, t: (i, j, 0, 0)),
            pl.BlockSpec((rows_gu, f2), step),
            pl.BlockSpec((rows_d, d), step),
        ],
        out_specs=[
            pl.BlockSpec((1, TQ, V_DIM), lambda i, j, t: (i, t, j)),
            pl.BlockSpec((rows_gu, f2), step),
            pl.BlockSpec((rows_d, d), step),
        ],
        out_shape=[
            jax.ShapeDtypeStruct((b, s, h * V_DIM), BF16),
            jax.ShapeDtypeStruct((e * d, f2), BF16),
            jax.ShapeDtypeStruct((e * f, d), BF16),
        ],
        compiler_params=_cparams("parallel", "parallel", "arbitrary"),
        name="attn",
    )(q, k, v, w_gate_up.reshape(e * d, f2), w_down.reshape(e * f, d))
    return o, wgl.reshape(e, d, f2), wdb.reshape(e, f, d)


def _out_proj_kernel(x_ref, yf_ref, yb_ref, xs_ref, z_ref, o_ref, dsk_ref, gnw_ref, anw_ref,
                     wo_ref, n2w_ref, wr1_ref, wr2_ref, br_ref,
                     h_ref, xn_ref, ri_ref, rg_ref, cnt_ref, carry_ref):
    i = pl.program_id(0)

    @pl.when(i == 0)
    def _():
        carry_ref[...] = jnp.zeros_like(carry_ref)

    tm = x_ref.shape[0]
    y = yf_ref[...] + yb_ref[...] + xs_ref[...].astype(F32) * dsk_ref[...]
    y = y * _silu(z_ref[...].astype(F32))
    gs = D_SSM // SSM_GROUPS
    halves = []
    for g in range(SSM_GROUPS):
        yg = y[:, g * gs:(g + 1) * gs]
        halves.append(_rms(yg, gnw_ref[:, g * gs:(g + 1) * gs]).astype(BF16))
    y_att = _rms(o_ref[...].astype(F32), anw_ref[...]).astype(BF16)
    lhs = jnp.concatenate(halves + [y_att], axis=1)
    h1 = x_ref[...] + _dot(lhs, wo_ref[...])
    h_ref[...] = h1
    xn = _rms(h1, n2w_ref[...])
    xn_ref[...] = xn

    x1 = xn.astype(BF16)
    x2 = (xn - x1.astype(F32)).astype(BF16)
    logits = _dot(x1, wr1_ref[...]) + (_dot(x1, wr2_ref[...]) + _dot(x2, wr1_ref[...])) + br_ref[...]
    lane = lax.broadcasted_iota(jnp.int32, (tm, LANES), 1)
    neg = jnp.float32(-jnp.inf)
    work = jnp.where(lane < N_EXPERTS, logits, neg)
    vals, ids = [], []
    for _ in range(TOP_K):
        m = jnp.max(work, axis=-1, keepdims=True)
        idx = jnp.min(jnp.where(work == m, lane, LANES), axis=-1, keepdims=True)
        vals.append(m)
        ids.append(idx)
        work = jnp.where(lane == idx, neg, work)
    es = [jnp.exp(v - vals[0]) for v in vals]
    den = es[0] + es[1] + es[2] + es[3]
    sel = jnp.zeros((tm, LANES), F32)
    for idx in ids:
        sel = sel + jnp.where(lane == idx, 1.0, 0.0)
    r = lax.broadcasted_iota(jnp.int32, (tm, tm), 0)
    c = lax.broadcasted_iota(jnp.int32, (tm, tm), 1)
    strict = jnp.where(c < r, 1.0, 0.0).astype(BF16)
    cum = _dot(strict, sel.astype(BF16)) + carry_ref[0:1, :]
    ri = jnp.zeros((tm, LANES), jnp.int32)
    rg = jnp.zeros((tm, LANES), F32)
    for kk in range(TOP_K):
        rank = jnp.sum(jnp.where(lane == ids[kk], cum, 0.0), axis=-1, keepdims=True).astype(jnp.int32)
        ri = jnp.where(lane == kk, ids[kk], ri)
        ri = jnp.where(lane == TOP_K + kk, rank, ri)
        rg = jnp.where(lane == kk, es[kk] / den, rg)
    ri_ref[...] = ri
    rg_ref[...] = rg
    total = carry_ref[0:1, :] + jnp.sum(sel, axis=0, keepdims=True)
    carry_ref[...] = jnp.broadcast_to(total, carry_ref.shape)
    cnt_ref[...] = jnp.broadcast_to(total, cnt_ref.shape)


def _out_proj(x2, yf, yb, xc2, p2, o2, dsk, gnw, anw, wo, n2w, wr1, wr2, br):
    n = x2.shape[0]
    row = lambda i: (i, 0)
    fixed = lambda i: (0, 0)
    return pl.pallas_call(
        _out_proj_kernel,
        grid=(n // TM_OUT,),
        in_specs=[
            pl.BlockSpec((TM_OUT, D_MODEL), row),
            pl.BlockSpec((TM_OUT, D_SSM), row),
            pl.BlockSpec((TM_OUT, D_SSM), row),
            pl.BlockSpec((TM_OUT, D_SSM), row),
            pl.BlockSpec((TM_OUT, D_SSM), row),
            pl.BlockSpec((TM_OUT, MLA_HEADS * V_DIM), row),
            pl.BlockSpec((1, D_SSM), fixed),
            pl.BlockSpec((1, D_SSM), fixed),
            pl.BlockSpec((1, MLA_HEADS * V_DIM), fixed),
            pl.BlockSpec((D_MODEL, D_MODEL), fixed),
            pl.BlockSpec((1, D_MODEL), fixed),
            pl.BlockSpec((D_MODEL, LANES), fixed),
            pl.BlockSpec((D_MODEL, LANES), fixed),
            pl.BlockSpec((1, LANES), fixed),
        ],
        out_specs=[
            pl.BlockSpec((TM_OUT, D_MODEL), row),
            pl.BlockSpec((TM_OUT, D_MODEL), row),
            pl.BlockSpec((TM_OUT, LANES), row),
            pl.BlockSpec((TM_OUT, LANES), row),
            pl.BlockSpec((8, LANES), fixed),
        ],
        out_shape=[
            jax.ShapeDtypeStruct((n, D_MODEL), F32),
            jax.ShapeDtypeStruct((n, D_MODEL), F32),
            jax.ShapeDtypeStruct((n, LANES), jnp.int32),
            jax.ShapeDtypeStruct((n, LANES), F32),
            jax.ShapeDtypeStruct((8, LANES), F32),
        ],
        scratch_shapes=[pltpu.VMEM((8, LANES), F32)],
        compiler_params=_cparams("arbitrary"),
        name="out_proj",
    )(x2, yf, yb, xc2, p2, o2, dsk, gnw, anw, wo, n2w, wr1, wr2, br)


def _moe_kernel(be_ref, nu_ref, nv_ref, tok_ref, tokn_ref, pos_ref, x_hbm, wgl_ref, wd_ref, bgl_ref, bd_ref,
                y_hbm, xg_ref, xb_ref, acc_ref, yb_ref, gsem, ssem):
    del be_ref
    blk = pl.program_id(0)
    hc = pl.program_id(1)
    nblk = pl.num_programs(0)
    last = pl.num_programs(1) - 1
    nu = nu_ref[0]
    slot = lax.rem(blk, 2)

    def gather(src_row, s, j):
        return pltpu.make_async_copy(x_hbm.at[pl.ds(src_row, 1)], xg_ref.at[s, pl.ds(j, 1)], gsem.at[s])

    def scatter(j, dst_row):
        return pltpu.make_async_copy(yb_ref.at[pl.ds(j, 1)], y_hbm.at[pl.ds(dst_row, 1)], ssem)

    def start_gather(rows_ref, s):
        for j in range(MOE_BLK):
            gather(rows_ref[0, 0, j], s, j).start()

    def wait_gather(s):
        for _ in range(MOE_BLK):
            gather(0, s, 0).wait()

    def start_scatter():
        for j in range(MOE_BLK):
            scatter(j, pos_ref[0, 0, j]).start()

    def wait_scatter():
        for _ in range(MOE_BLK):
            scatter(0, 0).wait()

    def compute(rows):
        hdn = _dot(xb_ref[:rows], wgl_ref[0]) + bgl_ref[0]
        acts = []
        for j in range(MOE_TC // LANES):
            glu = jnp.minimum(hdn[:, 2 * LANES * j:2 * LANES * j + LANES], SWIGLU_LIMIT)
            lin = jnp.clip(hdn[:, 2 * LANES * j + LANES:2 * LANES * (j + 1)], -SWIGLU_LIMIT, SWIGLU_LIMIT)
            acts.append((glu / (1.0 + jnp.exp(-SWIGLU_ALPHA * glu)) * (lin + 1.0)).astype(BF16))
        part = _dot(jnp.concatenate(acts, axis=1), wd_ref[0])

        @pl.when(hc == 0)
        def _():
            acc_ref[:rows] = part + bd_ref[0]

        @pl.when((hc > 0) & (hc < last))
        def _():
            acc_ref[:rows] += part

        @pl.when(hc == last)
        def _():
            yb_ref[:rows] = acc_ref[:rows] + part

    @pl.when(blk < nu)
    def _():
        @pl.when(hc == 0)
        def _():
            @pl.when(blk == 0)
            def _():
                yb_ref[...] = jnp.zeros_like(yb_ref)
                start_gather(tok_ref, 0)

            wait_gather(slot)
            xb_ref[...] = xg_ref[slot].astype(BF16)

            @pl.when(blk + 1 < nu)
            def _():
                start_gather(tokn_ref, 1 - slot)

        @pl.when((hc == last) & (blk > 0))
        def _():
            wait_scatter()

        half = nv_ref[blk] <= MOE_BLK // 2

        @pl.when(half)
        def _():
            compute(MOE_BLK // 2)

        @pl.when(jnp.logical_not(half))
        def _():
            compute(MOE_BLK)

        @pl.when(hc == last)
        def _():
            start_scatter()

    @pl.when((blk >= nu) & (hc == last))
    def _():
        wait_scatter()
        yb_ref[...] = jnp.zeros_like(yb_ref)
        start_scatter()

    @pl.when((blk == nblk - 1) & (hc == last))
    def _():
        wait_scatter()


def _moe(block_expert, n_used, rows_valid, row_tok, row_pos, xn2, wgl, wd, bgl, bd):
    nb = row_tok.shape[0]
    nh = D_EXPERT // MOE_TC
    assert nh >= 2

    def bb(b, nu):
        return jnp.minimum(b, nu[0] - 1)

    def hh(b, h, nu):
        order = jnp.where(b % 2 == 0, h, nh - 1 - h)
        return jnp.where(b < nu[0], order, nh - 1 - (nu[0] - 1) % 2 * (nh - 1))

    smem_rows = lambda index_map: pl.BlockSpec((1, 1, MOE_BLK), index_map, memory_space=pltpu.SMEM)
    grid_spec = pltpu.PrefetchScalarGridSpec(
        num_scalar_prefetch=3,
        grid=(nb, nh),
        in_specs=[
            smem_rows(lambda b, h, be, nu, nv: (b, 0, 0)),
            smem_rows(lambda b, h, be, nu, nv: (jnp.minimum(b + 1, nb - 1), 0, 0)),
            smem_rows(lambda b, h, be, nu, nv: (b, 0, 0)),
            pl.BlockSpec(memory_space=pl.ANY),
            pl.BlockSpec((1, D_MODEL, 2 * MOE_TC), lambda b, h, be, nu, nv: (be[bb(b, nu)], 0, hh(b, h, nu))),
            pl.BlockSpec((1, MOE_TC, D_MODEL), lambda b, h, be, nu, nv: (be[bb(b, nu)], hh(b, h, nu), 0)),
            pl.BlockSpec((1, 1, 2 * MOE_TC), lambda b, h, be, nu, nv: (be[bb(b, nu)], 0, hh(b, h, nu))),
            pl.BlockSpec((1, 1, D_MODEL), lambda b, h, be, nu, nv: (be[bb(b, nu)], 0, 0)),
        ],
        out_specs=pl.BlockSpec(memory_space=pl.ANY),
        scratch_shapes=[
            pltpu.VMEM((2, MOE_BLK, D_MODEL), F32),
            pltpu.VMEM((MOE_BLK, D_MODEL), BF16),
            pltpu.VMEM((MOE_BLK, D_MODEL), F32),
            pltpu.VMEM((MOE_BLK, D_MODEL), F32),
            pltpu.SemaphoreType.DMA((2,)),
            pltpu.SemaphoreType.DMA(()),
        ],
    )
    return pl.pallas_call(
        _moe_kernel,
        grid_spec=grid_spec,
        out_shape=jax.ShapeDtypeStruct((nb * MOE_BLK, D_MODEL), F32),
        compiler_params=_cparams("arbitrary", "arbitrary"),
        name="moe",
    )(block_expert, n_used, rows_valid, row_tok, row_tok, row_pos, xn2, wgl, wd, bgl, bd)


def _combine_kernel(y0_ref, y1_ref, y2_ref, y3_ref, h_ref, rg_ref, fw_ref, o_ref):
    rg = rg_ref[...]
    out = h_ref[...]
    for k, y_ref in enumerate((y0_ref, y1_ref, y2_ref, y3_ref)):
        out = out + y_ref[...] * rg[:, k:k + 1]
    o_ref[...] = _rms(out, fw_ref[...])


def _combine(y_rows, h1, rg, final_w):
    n = h1.shape[0]
    nt = n // TD
    row = lambda i: (i, 0)
    slot_spec = lambda k: pl.BlockSpec((TD, D_MODEL), lambda i: (k * nt + i, 0))
    return pl.pallas_call(
        _combine_kernel,
        grid=(nt,),
        in_specs=[slot_spec(k) for k in range(TOP_K)] + [
            pl.BlockSpec((TD, D_MODEL), row),
            pl.BlockSpec((TD, LANES), row),
            pl.BlockSpec((1, D_MODEL), lambda i: (0, 0)),
        ],
        out_specs=pl.BlockSpec((TD, D_MODEL), row),
        out_shape=jax.ShapeDtypeStruct((n, D_MODEL), F32),
        compiler_params=_cparams("parallel"),
        name="combine",
    )(y_rows, y_rows, y_rows, y_rows, h1, rg, final_w)


def _pad_cols(w, width):
    return jnp.pad(w, ((0, 0), (0, width - w.shape[1])))


def _rot_cols(w):
    half = w.shape[1] // 2
    return jnp.concatenate([-w[:, half:], w[:, :half]], axis=1)


def _layer(h, positions, norm1_w, w_in, conv_w, conv_b, dt_bias_f, dt_bias_b, a_log_f, a_log_b, d_skip,
           ssm_norm_w, q_a_norm_w, w_q_b, kv_a_norm_w, w_kv_b, attn_norm_w, w_out, norm2_w, w_router,
           b_router, w_gate_up, b_gate_up, w_down, b_down, final_norm_w):
    b, s, _ = h.shape
    n = b * s
    x2 = h.reshape(n, D_MODEL)

    o_z, o_xbc = 0, D_SSM
    o_dtf = o_xbc + CONV_DIM
    o_dtb = o_dtf + SSM_HEADS
    o_qa = o_dtb + SSM_HEADS
    o_kva = o_qa + Q_LORA
    o_kpe = o_kva + KV_LORA
    w_z = w_in[:, o_z:o_xbc]
    w_xbc = w_in[:, o_xbc:o_dtf]
    w_qa = w_in[:, o_qa:o_kva]
    w_kva = w_in[:, o_kva:o_kpe]
    w_kpe = w_in[:, o_kpe:o_kpe + QK_ROPE]
    w_dt = w_in[:, o_dtf:o_qa]
    w_main = jnp.concatenate([w_z, w_xbc, w_kva, w_qa], axis=1).astype(BF16)
    w_misc = jnp.concatenate([_pad_cols(w_kpe, LANES), _pad_cols(_rot_cols(w_kpe), LANES),
                              _pad_cols(w_dt, LANES)], axis=1).astype(BF16)

    p2, misc2 = _in_proj(x2, norm1_w.reshape(1, D_MODEL), w_main, w_misc)

    xc = _conv(p2.reshape(b, s, P_COLS), conv_w, conv_b.reshape(1, CONV_DIM))
    par = jnp.zeros((8, LANES), F32)
    par = par.at[0, :2 * SSM_HEADS].set(jnp.concatenate([dt_bias_f, dt_bias_b]))
    par = par.at[1, :2 * SSM_HEADS].set(jnp.concatenate([a_log_f, a_log_b]))
    y_f, y_b = _ssd(xc, misc2.reshape(b, s, MISC_COLS), par)

    inv_freq = ROPE_THETA ** (-jnp.arange(0, QK_ROPE, 2, dtype=F32) / QK_ROPE)
    freq = _pad_cols(jnp.concatenate([inv_freq, inv_freq])[None, :], LANES)
    wq3 = w_q_b.reshape(Q_LORA, MLA_HEADS, QK_NOPE + QK_ROPE)
    wq_rope = wq3[:, :, QK_NOPE:]
    wq = jnp.pad(wq3, ((0, 0), (0, 0), (0, QK_PAD - QK_NOPE - QK_ROPE))).reshape(Q_LORA, MLA_HEADS * QK_PAD)
    wq_rot = jnp.concatenate([-wq_rope[:, :, QK_ROPE // 2:], wq_rope[:, :, :QK_ROPE // 2]], axis=2)
    wqr = jnp.pad(wq_rot, ((0, 0), (0, 0), (0, LANES - QK_ROPE))).reshape(Q_LORA, MLA_HEADS * LANES)
    q, k, v = _mla_prep(p2, misc2, positions.reshape(n, 1), freq, q_a_norm_w.reshape(1, Q_LORA),
                        kv_a_norm_w.reshape(1, KV_LORA), wq.astype(BF16), wqr.astype(BF16),
                        w_kv_b.astype(BF16), b, s)
    o, wgl, wdb = _attention(q, k, v, w_gate_up, w_down)

    dsk = jnp.repeat(d_skip, SSM_HEAD_DIM)[None, :]
    wr = _pad_cols(w_router, LANES)
    wr1 = wr.astype(BF16)
    wr2 = (wr - wr1.astype(F32)).astype(BF16)
    br = _pad_cols(b_router[None, :], LANES)
    h1, xn2, ri, rg, cnt = _out_proj(
        x2, y_f.reshape(n, D_SSM), y_b.reshape(n, D_SSM), xc.reshape(n, CONV_DIM), p2, o.reshape(n, MLA_HEADS * V_DIM),
        dsk, ssm_norm_w.reshape(1, D_SSM), attn_norm_w.reshape(1, MLA_HEADS * V_DIM), w_out.astype(BF16),
        norm2_w.reshape(1, D_MODEL), wr1, wr2, br)

    counts = cnt[0, :N_EXPERTS].astype(jnp.int32)
    nblk = (counts + MOE_BLK - 1) // MOE_BLK
    blk_end = jnp.cumsum(nblk)
    row_start = (blk_end - nblk) * MOE_BLK
    n_blocks = (n * TOP_K) // MOE_BLK + N_EXPERTS
    n_rows = n_blocks * MOE_BLK
    ids = ri[:, :TOP_K]
    dest = (row_start[ids] + ri[:, TOP_K:2 * TOP_K]).reshape(n * TOP_K)
    block_ids = jnp.arange(n_blocks, dtype=jnp.int32)
    block_expert = jnp.minimum(jnp.sum((blk_end[None, :] <= block_ids[:, None]).astype(jnp.int32), axis=1),
                               N_EXPERTS - 1)
    n_used = blk_end[-1:].astype(jnp.int32)
    rows_valid = jnp.clip(counts[block_expert] - (block_ids - (blk_end - nblk)[block_expert]) * MOE_BLK, 0, MOE_BLK)

    assign = (jnp.arange(TOP_K, dtype=jnp.int32)[None, :] * n + jnp.arange(n, dtype=jnp.int32)[:, None])
    inv = jnp.full((n_rows,), -1, jnp.int32).at[dest].set(assign.reshape(n * TOP_K), unique_indices=True)
    is_pad = inv < 0
    pad_pos = n * TOP_K + jnp.cumsum(is_pad.astype(jnp.int32)) - 1
    row_pos = jnp.where(is_pad, pad_pos, inv).reshape(n_blocks, 1, MOE_BLK)
    row_tok = jnp.where(is_pad, 0, inv % n).reshape(n_blocks, 1, MOE_BLK)

    bgl = b_gate_up.reshape(N_EXPERTS, D_EXPERT // LANES, LANES, 2).swapaxes(2, 3).reshape(N_EXPERTS, 1, 2 * D_EXPERT)
    y_rows = _moe(block_expert, n_used, rows_valid, row_tok, row_pos, xn2, wgl, wdb, bgl, b_down[:, None, :])

    out = _combine(y_rows, h1, rg, final_norm_w.reshape(1, D_MODEL))
    return out.reshape(b, s, D_MODEL)


def kernel(x, positions, norm1_w, w_in, conv_w, conv_b, dt_bias_f, dt_bias_b, a_log_f, a_log_b, d_skip,
           ssm_norm_w, q_a_norm_w, w_q_b, kv_a_norm_w, w_kv_b, attn_norm_w, w_out, norm2_w, w_router, b_router,
           w_gate_up, b_gate_up, w_down, b_down, final_norm_w):
    depth = norm1_w.shape[0]
    assert depth == 1, "the final norm is fused into the single layer's combine step"
    return _layer(x, positions, norm1_w[0], w_in[0], conv_w[0], conv_b[0], dt_bias_f[0], dt_bias_b[0],
                  a_log_f[0], a_log_b[0], d_skip[0], ssm_norm_w[0], q_a_norm_w[0], w_q_b[0], kv_a_norm_w[0],
                  w_kv_b[0], attn_norm_w[0], w_out[0], norm2_w[0], w_router[0], b_router[0], w_gate_up[0],
                  b_gate_up[0], w_down[0], b_down[0], final_norm_w)
```

```python
import functools
import math

import jax
import jax.numpy as jnp
from jax import lax
from jax.experimental import pallas as pl
from jax.experimental.pallas import tpu as pltpu

F32 = jnp.float32
BF16 = jnp.bfloat16

D_MODEL = 2048
D_SSM = 1024
SSM_HEAD_DIM = 64
SSM_HEADS = 16
SSM_GROUPS = 2
D_STATE = 128
D_CONV = 5
CONV_DIM = D_SSM + 2 * SSM_GROUPS * D_STATE
CHUNK = 128
MLA_HEADS = 8
QK_NOPE = 128
QK_ROPE = 64
V_DIM = 128
Q_LORA = 768
KV_LORA = 512
ROPE_THETA = 10000.0
N_EXPERTS = 32
TOP_K = 4
D_EXPERT = 2048
SWIGLU_ALPHA = 1.702
SWIGLU_LIMIT = 7.0
RMS_EPS = 1e-6
LOG2_E = 1.0 / math.log(2.0)

LANES = 128
QK_PAD = 256
P_COLS = D_SSM + CONV_DIM + KV_LORA + Q_LORA
MISC_COLS = 3 * LANES
VMEM_LIMIT = 56 * 1024 * 1024

TM_IN = 512
TM_PREP = 512
TQ = 256
TM_OUT = 256
MOE_BLK = 512
MOE_TC = 1024
TD = 256


def _cparams(*sem):
    return pltpu.CompilerParams(dimension_semantics=sem, vmem_limit_bytes=VMEM_LIMIT)


def _silu(x):
    return x / (1.0 + jnp.exp(-x))


def _softplus(x):
    return jnp.maximum(x, 0.0) + jnp.log(1.0 + jnp.exp(-jnp.abs(x)))


def _split3(x):
    x1 = x.astype(BF16)
    r = x - x1.astype(F32)
    x2 = r.astype(BF16)
    x3 = (r - x2.astype(F32)).astype(BF16)
    return x1, x2, x3


def _dot(a, b):
    return jnp.dot(a, b, preferred_element_type=F32)


def _dot_nt(a, b):
    return lax.dot_general(a, b, (((1,), (1,)), ((), ())), preferred_element_type=F32)


def _dot_exact_lhs(x, m, passes):
    parts = _split3(x)[:passes]
    acc = _dot(parts[0], m)
    for p in parts[1:]:
        acc = acc + _dot(p, m)
    return acc


def _in_proj_kernel(x_ref, nw_ref, w_ref, wm_ref, p_ref, m_ref):
    x = x_ref[...]
    ms = jnp.mean(x * x, axis=-1, keepdims=True)
    xn = (x * lax.rsqrt(ms + RMS_EPS) * nw_ref[...]).astype(BF16)
    m_ref[...] = _dot(xn, wm_ref[...])
    p_ref[...] = _dot(xn, w_ref[...]).astype(BF16)


def _resident(shape):
    return pl.BlockSpec(shape, lambda *_: (0, 0), pipeline_mode=pl.Buffered(1))


def _in_proj(x2, norm_w, w_main, w_misc):
    n = x2.shape[0]
    return pl.pallas_call(
        _in_proj_kernel,
        grid=(n // TM_IN,),
        in_specs=[
            pl.BlockSpec((TM_IN, D_MODEL), lambda i: (i, 0)),
            _resident((1, D_MODEL)),
            _resident((D_MODEL, P_COLS)),
            _resident((D_MODEL, MISC_COLS)),
        ],
        out_specs=[
            pl.BlockSpec((TM_IN, P_COLS), lambda i: (i, 0)),
            pl.BlockSpec((TM_IN, MISC_COLS), lambda i: (i, 0)),
        ],
        out_shape=[
            jax.ShapeDtypeStruct((n, P_COLS), BF16),
            jax.ShapeDtypeStruct((n, MISC_COLS), F32),
        ],
        compiler_params=_cparams("parallel"),
        name="in_proj",
    )(x2, norm_w, w_main, w_misc)


def _conv_kernel(u_ref, w_ref, b_ref, o_ref):
    u = u_ref[0].astype(F32)
    s = u.shape[0]
    row = lax.broadcasted_iota(jnp.int32, u.shape, 0)
    pad = D_CONV // 2
    acc = u * w_ref[pad:pad + 1, :] + b_ref[...]
    for k in range(D_CONV):
        off = k - pad
        if off == 0:
            continue
        shifted = pltpu.roll(u, (-off) % s, 0)
        valid = (row + off >= 0) & (row + off < s)
        acc = acc + jnp.where(valid, shifted, 0.0) * w_ref[k:k + 1, :]
    o_ref[0] = _silu(acc).astype(BF16)


def _conv(p3, conv_w, conv_b):
    b, s, _ = p3.shape
    first = D_SSM // LANES
    return pl.pallas_call(
        _conv_kernel,
        grid=(b, CONV_DIM // LANES),
        in_specs=[
            pl.BlockSpec((1, s, LANES), lambda i, c: (i, 0, first + c)),
            pl.BlockSpec((D_CONV, LANES), lambda i, c: (0, c)),
            pl.BlockSpec((1, LANES), lambda i, c: (0, c)),
        ],
        out_specs=pl.BlockSpec((1, s, LANES), lambda i, c: (i, 0, c)),
        out_shape=jax.ShapeDtypeStruct((b, s, CONV_DIM), BF16),
        compiler_params=_cparams("parallel", "parallel"),
        name="conv",
    )(p3, conv_w, conv_b)


def _ssd_direction(xc, dt_e, cs, cs_e, tot_row, state_ref, d, reverse):
    L = xc.shape[0]
    gw = D_SSM // SSM_GROUPS
    x = xc[:, :D_SSM].astype(F32)
    xdt = x * dt_e
    xdt_b = xdt.astype(BF16)
    tot_e = cs_e[tot_row:tot_row + 1, :]
    decay_out = jnp.exp(cs_e)
    xw = (xdt * jnp.exp(tot_e - cs_e)).astype(BF16)
    chunk_decay = jnp.exp(tot_e)

    cs_t = cs.T
    li = lax.broadcasted_iota(jnp.int32, (L, L), 0)
    si = lax.broadcasted_iota(jnp.int32, (L, L), 1)
    keep = (si >= li) if reverse else (si <= li)
    lane = lax.broadcasted_iota(jnp.int32, (L, LANES), 1)

    ys = []
    for g in range(SSM_GROUPS):
        bm = xc[:, D_SSM + g * D_STATE:D_SSM + (g + 1) * D_STATE]
        cm = xc[:, D_SSM + SSM_GROUPS * D_STATE + g * D_STATE:D_SSM + SSM_GROUPS * D_STATE + (g + 1) * D_STATE]
        cb = _dot_nt(cm, bm)
        st = state_ref[d, g]
        y_off = _dot(cm, st.astype(BF16)) * decay_out[:, g * gw:(g + 1) * gw]
        bm_t = bm.astype(F32).T.astype(BF16)
        state_ref[d, g] = st * chunk_decay[:, g * gw:(g + 1) * gw] + _dot(bm_t, xw[:, g * gw:(g + 1) * gw])

        pieces = []
        heads_per_group = SSM_HEADS // SSM_GROUPS
        for j in range(heads_per_group // 2):
            scores = []
            for hh in range(2):
                h = g * heads_per_group + 2 * j + hh
                hl = SSM_HEADS * d + h
                col = jnp.broadcast_to(cs[:, hl:hl + 1], (L, L))
                rowv = jnp.broadcast_to(cs_t[hl:hl + 1, :], (L, L))
                dec = jnp.exp(jnp.where(keep, col - rowv, -1e30))
                scores.append((cb * dec).astype(BF16))
            lhs = jnp.concatenate(scores, axis=1)
            c0 = g * gw + j * LANES
            xp = xdt_b[:, c0:c0 + LANES]
            zero = jnp.zeros_like(xp)
            rhs = jnp.concatenate([jnp.where(lane < SSM_HEAD_DIM, xp, zero),
                                   jnp.where(lane >= SSM_HEAD_DIM, xp, zero)], axis=0)
            pieces.append(_dot(lhs, rhs))
        ys.append(jnp.concatenate(pieces, axis=1) + y_off)
    return jnp.concatenate(ys, axis=1)


def _ssd_kernel(xf_ref, xb_ref, mf_ref, mb_ref, par_ref, yf_ref, yb_ref, state_ref):
    @pl.when(pl.program_id(1) == 0)
    def _():
        state_ref[...] = jnp.zeros_like(state_ref)

    L = CHUNK
    par = par_ref[...]
    lane = lax.broadcasted_iota(jnp.int32, (1, LANES), 1)
    active = lane < 2 * SSM_HEADS
    a_coef = jnp.where(active, -jnp.exp(par[1:2, :]), 0.0)
    r = lax.broadcasted_iota(jnp.int32, (L, L), 0)
    c = lax.broadcasted_iota(jnp.int32, (L, L), 1)
    tri_lo = jnp.where(c <= r, 1.0, 0.0).astype(BF16)
    tri_up = jnp.where(c >= r, 1.0, 0.0).astype(BF16)
    er = lax.broadcasted_iota(jnp.int32, (LANES, D_SSM), 0)
    ec = lax.broadcasted_iota(jnp.int32, (LANES, D_SSM), 1)

    for d, (x_ref, m_ref, y_ref) in enumerate(((xf_ref, mf_ref, yf_ref), (xb_ref, mb_ref, yb_ref))):
        reverse = d == 1
        expand = jnp.where(er == ec // SSM_HEAD_DIM + SSM_HEADS * d, 1.0, 0.0).astype(BF16)
        dt = jnp.where(active, _softplus(m_ref[0] + par[0:1, :]), 0.0)
        a = dt * a_coef
        a1, a2, a3 = _split3(a)
        tri = tri_up if reverse else tri_lo
        cs = _dot(tri, a1) + _dot(tri, a2) + _dot(tri, a3)
        dt_e = _dot_exact_lhs(dt, expand, 2)
        cs_e = _dot_exact_lhs(cs, expand, 3)
        y = _ssd_direction(x_ref[0], dt_e, cs, cs_e, 0 if reverse else L - 1, state_ref, d, reverse)
        y_ref[0] = y


def _ssd(xc, misc3, par):
    b, s, _ = xc.shape
    nc = s // CHUNK
    dtb = MISC_COLS // LANES - 1
    gw = D_SSM // SSM_GROUPS
    return pl.pallas_call(
        _ssd_kernel,
        grid=(b, nc),
        in_specs=[
            pl.BlockSpec((1, CHUNK, CONV_DIM), lambda i, c: (i, c, 0)),
            pl.BlockSpec((1, CHUNK, CONV_DIM), lambda i, c: (i, nc - 1 - c, 0)),
            pl.BlockSpec((1, CHUNK, LANES), lambda i, c: (i, c, dtb)),
            pl.BlockSpec((1, CHUNK, LANES), lambda i, c: (i, nc - 1 - c, dtb)),
            pl.BlockSpec((8, LANES), lambda i, c: (0, 0)),
        ],
        out_specs=[
            pl.BlockSpec((1, CHUNK, D_SSM), lambda i, c: (i, c, 0)),
            pl.BlockSpec((1, CHUNK, D_SSM), lambda i, c: (i, nc - 1 - c, 0)),
        ],
        out_shape=[jax.ShapeDtypeStruct((b, s, D_SSM), F32)] * 2,
        scratch_shapes=[pltpu.VMEM((2, SSM_GROUPS, D_STATE, gw), F32)],
        compiler_params=_cparams("parallel", "arbitrary"),
        name="ssd",
    )(xc, xc, misc3, misc3, par)


def _rms(x, w):
    ms = jnp.mean(x * x, axis=-1, keepdims=True)
    return x * lax.rsqrt(ms + RMS_EPS) * w


def _mla_prep_kernel(qa_ref, kva_ref, misc_ref, pos_ref, freq_ref, qnw_ref, kvnw_ref,
                     wq_ref, wqr_ref, wkv_ref, q_ref, k_ref, v_ref):
    scale = (QK_NOPE + QK_ROPE) ** -0.5 * LOG2_E
    qn = _rms(qa_ref[...].astype(F32), qnw_ref[...]).astype(BF16)
    kvn = _rms(kva_ref[...].astype(F32), kvnw_ref[...]).astype(BF16)
    ang = pos_ref[...].astype(F32) * freq_ref[...]
    cos = jnp.cos(ang)
    sin = jnp.sin(ang)
    qm = _dot(qn, wq_ref[...])
    qr = _dot(qn, wqr_ref[...])
    kv = _dot(kvn, wkv_ref[...])
    misc = misc_ref[...]
    k_rope = (misc[:, :LANES] * cos + misc[:, LANES:2 * LANES] * sin).astype(BF16)
    for h in range(MLA_HEADS):
        q_ref[0, h, :, :LANES] = (qm[:, h * QK_PAD:h * QK_PAD + LANES] * scale).astype(BF16)
        q_rope = qm[:, h * QK_PAD + LANES:(h + 1) * QK_PAD] * cos + qr[:, h * LANES:(h + 1) * LANES] * sin
        q_ref[0, h, :, LANES:] = (q_rope * scale).astype(BF16)
        k_ref[0, h, :, :LANES] = kv[:, h * 2 * LANES:h * 2 * LANES + LANES].astype(BF16)
        k_ref[0, h, :, LANES:] = k_rope
        v_ref[0, h] = kv[:, h * 2 * LANES + LANES:(h + 1) * 2 * LANES].astype(BF16)


def _mla_prep(p2, misc2, pos2, freq, qnw, kvnw, wq, wqr, wkv, b, s):
    n = p2.shape[0]
    spb = s // TM_PREP
    kv_blk = (D_SSM + CONV_DIM) // KV_LORA
    qa_blk = (D_SSM + CONV_DIM + KV_LORA) // Q_LORA
    out_idx = lambda i: (i // spb, 0, i % spb, 0)
    return pl.pallas_call(
        _mla_prep_kernel,
        grid=(n // TM_PREP,),
        in_specs=[
            pl.BlockSpec((TM_PREP, Q_LORA), lambda i: (i, qa_blk)),
            pl.BlockSpec((TM_PREP, KV_LORA), lambda i: (i, kv_blk)),
            pl.BlockSpec((TM_PREP, 2 * LANES), lambda i: (i, 0)),
            pl.BlockSpec((TM_PREP, 1), lambda i: (i, 0)),
            pl.BlockSpec((1, LANES), lambda i: (0, 0)),
            pl.BlockSpec((1, Q_LORA), lambda i: (0, 0)),
            pl.BlockSpec((1, KV_LORA), lambda i: (0, 0)),
            pl.BlockSpec((Q_LORA, MLA_HEADS * QK_PAD), lambda i: (0, 0)),
            pl.BlockSpec((Q_LORA, MLA_HEADS * LANES), lambda i: (0, 0)),
            pl.BlockSpec((KV_LORA, MLA_HEADS * 2 * LANES), lambda i: (0, 0)),
        ],
        out_specs=[
            pl.BlockSpec((1, MLA_HEADS, TM_PREP, QK_PAD), out_idx),
            pl.BlockSpec((1, MLA_HEADS, TM_PREP, QK_PAD), out_idx),
            pl.BlockSpec((1, MLA_HEADS, TM_PREP, V_DIM), out_idx),
        ],
        out_shape=[
            jax.ShapeDtypeStruct((b, MLA_HEADS, s, QK_PAD), BF16),
            jax.ShapeDtypeStruct((b, MLA_HEADS, s, QK_PAD), BF16),
            jax.ShapeDtypeStruct((b, MLA_HEADS, s, V_DIM), BF16),
        ],
        compiler_params=_cparams("parallel"),
        name="mla_prep",
    )(p2, p2, misc2, pos2, freq, qnw, kvnw, wq, wqr, wkv)


def _regroup_gate_lin(w_ref, o_ref):
    k = lax.broadcasted_iota(jnp.int32, (2 * LANES, 2 * LANES), 0)
    c = lax.broadcasted_iota(jnp.int32, (2 * LANES, 2 * LANES), 1)
    src = jnp.where(c < LANES, 2 * c, 2 * (c - LANES) + 1)
    perm = jnp.where(k == src, 1.0, 0.0).astype(BF16)
    for j in range(w_ref.shape[1] // (2 * LANES)):
        cols = slice(2 * LANES * j, 2 * LANES * (j + 1))
        o_ref[:, cols] = _dot(w_ref[:, cols].astype(BF16), perm).astype(BF16)


def _attn_kernel(q_ref, k_ref, v_ref, wgu_ref, wd_ref, o_ref, wgl_ref, wdb_ref):
    s = _dot_nt(q_ref[0, 0], k_ref[0, 0])
    m = jnp.max(s, axis=-1, keepdims=True)
    p = jnp.exp2(s - m)
    l = jnp.sum(p, axis=-1, keepdims=True)
    o = _dot(p.astype(BF16), v_ref[0, 0])
    o_ref[0] = (o / l).astype(BF16)
    _regroup_gate_lin(wgu_ref, wgl_ref)
    wdb_ref[...] = wd_ref[...].astype(BF16)


def _attention(q, k, v, w_gate_up, w_down):
    b, h, s, _ = q.shape
    nq = s // TQ
    steps = b * h * nq
    e, d, f2 = w_gate_up.shape
    f = w_down.shape[1]
    assert (e * d) % steps == 0 and (e * f) % steps == 0
    rows_gu, rows_d = e * d // steps, e * f // steps
    assert rows_gu % 16 == 0 and rows_d % 16 == 0 and rows_gu <= 512 and rows_d <= 512
    step = lambda i, j, t: ((i * h + j) * nq + t, 0)
    o, wgl, wdb = pl.pallas_call(
        _attn_kernel,
        grid=(b, h, nq),
        in_specs=[
            pl.BlockSpec((1, 1, TQ, QK_PAD), lambda i, j, t: (i, j, t, 0)),
            pl.BlockSpec((1, 1, s, QK_PAD), lambda i, j, t: (i, j, 0, 0)),
            pl.BlockSpec((1, 1, s, V_DIM), lambda i, j, t: (i, j, 0, 0)),
            pl.BlockSpec((rows_gu, f2), step),
            pl.BlockSpec((rows_d, d), step),
        ],
        out_specs=[
            pl.BlockSpec((1, TQ, V_DIM), lambda i, j, t: (i, t, j)),
            pl.BlockSpec((rows_gu, f2), step),
            pl.BlockSpec((rows_d, d), step),
        ],
        out_shape=[
            jax.ShapeDtypeStruct((b, s, h * V_DIM), BF16),
            jax.ShapeDtypeStruct((e * d, f2), BF16),
            jax.ShapeDtypeStruct((e * f, d), BF16),
        ],
        compiler_params=_cparams("parallel", "parallel", "arbitrary"),
        name="attn",
    )(q, k, v, w_gate_up.reshape(e * d, f2), w_down.reshape(e * f, d))
    return o, wgl.reshape(e, d, f2), wdb.reshape(e, f, d)


def _out_proj_kernel(x_ref, yf_ref, yb_ref, xs_ref, z_ref, o_ref, dsk_ref, gnw_ref, anw_ref,
                     wo_ref, n2w_ref, wr1_ref, wr2_ref, br_ref,
                     h_ref, xn_ref, ri_ref, rg_ref, cnt_ref, carry_ref):
    i = pl.program_id(0)

    @pl.when(i == 0)
    def _():
        carry_ref[...] = jnp.zeros_like(carry_ref)

    tm = x_ref.shape[0]
    y = yf_ref[...] + yb_ref[...] + xs_ref[...].astype(F32) * dsk_ref[...]
    y = y * _silu(z_ref[...].astype(F32))
    gs = D_SSM // SSM_GROUPS
    halves = []
    for g in range(SSM_GROUPS):
        yg = y[:, g * gs:(g + 1) * gs]
        halves.append(_rms(yg, gnw_ref[:, g * gs:(g + 1) * gs]).astype(BF16))
    y_att = _rms(o_ref[...].astype(F32), anw_ref[...]).astype(BF16)
    lhs = jnp.concatenate(halves + [y_att], axis=1)
    h1 = x_ref[...] + _dot(lhs, wo_ref[...])
    h_ref[...] = h1
    xn = _rms(h1, n2w_ref[...])
    xn_ref[...] = xn

    x1 = xn.astype(BF16)
    x2 = (xn - x1.astype(F32)).astype(BF16)
    logits = _dot(x1, wr1_ref[...]) + (_dot(x1, wr2_ref[...]) + _dot(x2, wr1_ref[...])) + br_ref[...]
    lane = lax.broadcasted_iota(jnp.int32, (tm, LANES), 1)
    neg = jnp.float32(-jnp.inf)
    work = jnp.where(lane < N_EXPERTS, logits, neg)
    vals, ids = [], []
    for _ in range(TOP_K):
        m = jnp.max(work, axis=-1, keepdims=True)
        idx = jnp.min(jnp.where(work == m, lane, LANES), axis=-1, keepdims=True)
        vals.append(m)
        ids.append(idx)
        work = jnp.where(lane == idx, neg, work)
    es = [jnp.exp(v - vals[0]) for v in vals]
    den = es[0] + es[1] + es[2] + es[3]
    sel = jnp.zeros((tm, LANES), F32)
    for idx in ids:
        sel = sel + jnp.where(lane == idx, 1.0, 0.0)
    r = lax.broadcasted_iota(jnp.int32, (tm, tm), 0)
    c = lax.broadcasted_iota(jnp.int32, (tm, tm), 1)
    strict = jnp.where(c < r, 1.0, 0.0).astype(BF16)
    cum = _dot(strict, sel.astype(BF16)) + carry_ref[0:1, :]
    ri = jnp.zeros((tm, LANES), jnp.int32)
    rg = jnp.zeros((tm, LANES), F32)
    for kk in range(TOP_K):
        rank = jnp.sum(jnp.where(lane == ids[kk], cum, 0.0), axis=-1, keepdims=True).astype(jnp.int32)
        ri = jnp.where(lane == kk, ids[kk], ri)
        ri = jnp.where(lane == TOP_K + kk, rank, ri)
        rg = jnp.where(lane == kk, es[kk] / den, rg)
    ri_ref[...] = ri
    rg_ref[...] = rg
    total = carry_ref[0:1, :] + jnp.sum(sel, axis=0, keepdims=True)
    carry_ref[...] = jnp.broadcast_to(total, carry_ref.shape)
    cnt_ref[...] = jnp.broadcast_to(total, cnt_ref.shape)


def _out_proj(x2, yf, yb, xc2, p2, o2, dsk, gnw, anw, wo, n2w, wr1, wr2, br):
    n = x2.shape[0]
    row = lambda i: (i, 0)
    fixed = lambda i: (0, 0)
    return pl.pallas_call(
        _out_proj_kernel,
        grid=(n // TM_OUT,),
        in_specs=[
            pl.BlockSpec((TM_OUT, D_MODEL), row),
            pl.BlockSpec((TM_OUT, D_SSM), row),
            pl.BlockSpec((TM_OUT, D_SSM), row),
            pl.BlockSpec((TM_OUT, D_SSM), row),
            pl.BlockSpec((TM_OUT, D_SSM), row),
            pl.BlockSpec((TM_OUT, MLA_HEADS * V_DIM), row),
            pl.BlockSpec((1, D_SSM), fixed),
            pl.BlockSpec((1, D_SSM), fixed),
            pl.BlockSpec((1, MLA_HEADS * V_DIM), fixed),
            pl.BlockSpec((D_MODEL, D_MODEL), fixed),
            pl.BlockSpec((1, D_MODEL), fixed),
            pl.BlockSpec((D_MODEL, LANES), fixed),
            pl.BlockSpec((D_MODEL, LANES), fixed),
            pl.BlockSpec((1, LANES), fixed),
        ],
        out_specs=[
            pl.BlockSpec((TM_OUT, D_MODEL), row),
            pl.BlockSpec((TM_OUT, D_MODEL), row),
            pl.BlockSpec((TM_OUT, LANES), row),
            pl.BlockSpec((TM_OUT, LANES), row),
            pl.BlockSpec((8, LANES), fixed),
        ],
        out_shape=[
            jax.ShapeDtypeStruct((n, D_MODEL), F32),
            jax.ShapeDtypeStruct((n, D_MODEL), F32),
            jax.ShapeDtypeStruct((n, LANES), jnp.int32),
            jax.ShapeDtypeStruct((n, LANES), F32),
            jax.ShapeDtypeStruct((8, LANES), F32),
        ],
        scratch_shapes=[pltpu.VMEM((8, LANES), F32)],
        compiler_params=_cparams("arbitrary"),
        name="out_proj",
    )(x2, yf, yb, xc2, p2, o2, dsk, gnw, anw, wo, n2w, wr1, wr2, br)


def _moe_kernel(be_ref, nu_ref, nv_ref, tok_ref, tokn_ref, pos_ref, x_hbm, wgl_ref, wd_ref, bgl_ref, bd_ref,
                y_hbm, xg_ref, xb_ref, acc_ref, yb_ref, gsem, ssem):
    del be_ref
    blk = pl.program_id(0)
    hc = pl.program_id(1)
    nblk = pl.num_programs(0)
    last = pl.num_programs(1) - 1
    nu = nu_ref[0]
    slot = lax.rem(blk, 2)

    def gather(src_row, s, j):
        return pltpu.make_async_copy(x_hbm.at[pl.ds(src_row, 1)], xg_ref.at[s, pl.ds(j, 1)], gsem.at[s])

    def scatter(j, dst_row):
        return pltpu.make_async_copy(yb_ref.at[pl.ds(j, 1)], y_hbm.at[pl.ds(dst_row, 1)], ssem)

    def start_gather(rows_ref, s):
        for j in range(MOE_BLK):
            gather(rows_ref[0, 0, j], s, j).start()

    def wait_gather(s):
        for _ in range(MOE_BLK):
            gather(0, s, 0).wait()

    def start_scatter():
        for j in range(MOE_BLK):
            scatter(j, pos_ref[0, 0, j]).start()

    def wait_scatter():
        for _ in range(MOE_BLK):
            scatter(0, 0).wait()

    def compute(rows):
        hdn = _dot(xb_ref[:rows], wgl_ref[0]) + bgl_ref[0]
        acts = []
        for j in range(MOE_TC // LANES):
            glu = jnp.minimum(hdn[:, 2 * LANES * j:2 * LANES * j + LANES], SWIGLU_LIMIT)
            lin = jnp.clip(hdn[:, 2 * LANES * j + LANES:2 * LANES * (j + 1)], -SWIGLU_LIMIT, SWIGLU_LIMIT)
            acts.append((glu / (1.0 + jnp.exp(-SWIGLU_ALPHA * glu)) * (lin + 1.0)).astype(BF16))
        part = _dot(jnp.concatenate(acts, axis=1), wd_ref[0])

        @pl.when(hc == 0)
        def _():
            acc_ref[:rows] = part + bd_ref[0]

        @pl.when((hc > 0) & (hc < last))
        def _():
            acc_ref[:rows] += part

        @pl.when(hc == last)
        def _():
            @pl.when(blk > 0)
            def _():
                wait_scatter()

            yb_ref[:rows] = acc_ref[:rows] + part

    @pl.when(blk < nu)
    def _():
        @pl.when(hc == 0)
        def _():
            @pl.when(blk == 0)
            def _():
                yb_ref[...] = jnp.zeros_like(yb_ref)
                start_gather(tok_ref, 0)

            wait_gather(slot)
            xb_ref[...] = xg_ref[slot].astype(BF16)

            @pl.when(blk + 1 < nu)
            def _():
                start_gather(tokn_ref, 1 - slot)

        half = nv_ref[blk] <= MOE_BLK // 2

        @pl.when(half)
        def _():
            compute(MOE_BLK // 2)

        @pl.when(jnp.logical_not(half))
        def _():
            compute(MOE_BLK)

        @pl.when(hc == last)
        def _():
            start_scatter()

    @pl.when((blk >= nu) & (hc == last))
    def _():
        wait_scatter()
        yb_ref[...] = jnp.zeros_like(yb_ref)
        start_scatter()

    @pl.when((blk == nblk - 1) & (hc == last))
    def _():
        wait_scatter()


def _moe(block_expert, n_used, rows_valid, row_tok, row_pos, xn2, wgl, wd, bgl, bd):
    nb = row_tok.shape[0]
    nh = D_EXPERT // MOE_TC
    assert nh >= 2

    def bb(b, nu):
        return jnp.minimum(b, nu[0] - 1)

    def hh(b, h, nu):
        order = jnp.where(b % 2 == 0, h, nh - 1 - h)
        return jnp.where(b < nu[0], order, nh - 1 - (nu[0] - 1) % 2 * (nh - 1))

    smem_rows = lambda index_map: pl.BlockSpec((1, 1, MOE_BLK), index_map, memory_space=pltpu.SMEM)
    grid_spec = pltpu.PrefetchScalarGridSpec(
        num_scalar_prefetch=3,
        grid=(nb, nh),
        in_specs=[
            smem_rows(lambda b, h, be, nu, nv: (b, 0, 0)),
            smem_rows(lambda b, h, be, nu, nv: (jnp.minimum(b + 1, nb - 1), 0, 0)),
            smem_rows(lambda b, h, be, nu, nv: (b, 0, 0)),
            pl.BlockSpec(memory_space=pl.ANY),
            pl.BlockSpec((1, D_MODEL, 2 * MOE_TC), lambda b, h, be, nu, nv: (be[bb(b, nu)], 0, hh(b, h, nu))),
            pl.BlockSpec((1, MOE_TC, D_MODEL), lambda b, h, be, nu, nv: (be[bb(b, nu)], hh(b, h, nu), 0)),
            pl.BlockSpec((1, 1, 2 * MOE_TC), lambda b, h, be, nu, nv: (be[bb(b, nu)], 0, hh(b, h, nu))),
            pl.BlockSpec((1, 1, D_MODEL), lambda b, h, be, nu, nv: (be[bb(b, nu)], 0, 0)),
        ],
        out_specs=pl.BlockSpec(memory_space=pl.ANY),
        scratch_shapes=[
            pltpu.VMEM((2, MOE_BLK, D_MODEL), F32),
            pltpu.VMEM((MOE_BLK, D_MODEL), BF16),
            pltpu.VMEM((MOE_BLK, D_MODEL), F32),
            pltpu.VMEM((MOE_BLK, D_MODEL), F32),
            pltpu.SemaphoreType.DMA((2,)),
            pltpu.SemaphoreType.DMA(()),
        ],
    )
    return pl.pallas_call(
        _moe_kernel,
        grid_spec=grid_spec,
        out_shape=jax.ShapeDtypeStruct((nb * MOE_BLK, D_MODEL), F32),
        compiler_params=_cparams("arbitrary", "arbitrary"),
        name="moe",
    )(block_expert, n_used, rows_valid, row_tok, row_tok, row_pos, xn2, wgl, wd, bgl, bd)


def _combine_kernel(y0_ref, y1_ref, y2_ref, y3_ref, h_ref, rg_ref, fw_ref, o_ref):
    rg = rg_ref[...]
    out = h_ref[...]
    for k, y_ref in enumerate((y0_ref, y1_ref, y2_ref, y3_ref)):
        out = out + y_ref[...] * rg[:, k:k + 1]
    o_ref[...] = _rms(out, fw_ref[...])


def _combine(y_rows, h1, rg, final_w):
    n = h1.shape[0]
    nt = n // TD
    row = lambda i: (i, 0)
    slot_spec = lambda k: pl.BlockSpec((TD, D_MODEL), lambda i: (k * nt + i, 0))
    return pl.pallas_call(
        _combine_kernel,
        grid=(nt,),
        in_specs=[slot_spec(k) for k in range(TOP_K)] + [
            pl.BlockSpec((TD, D_MODEL), row),
            pl.BlockSpec((TD, LANES), row),
            pl.BlockSpec((1, D_MODEL), lambda i: (0, 0)),
        ],
        out_specs=pl.BlockSpec((TD, D_MODEL), row),
        out_shape=jax.ShapeDtypeStruct((n, D_MODEL), F32),
        compiler_params=_cparams("parallel"),
        name="combine",
    )(y_rows, y_rows, y_rows, y_rows, h1, rg, final_w)


def _pad_cols(w, width):
    return jnp.pad(w, ((0, 0), (0, width - w.shape[1])))


def _rot_cols(w):
    half = w.shape[1] // 2
    return jnp.concatenate([-w[:, half:], w[:, :half]], axis=1)


def _layer(h, positions, norm1_w, w_in, conv_w, conv_b, dt_bias_f, dt_bias_b, a_log_f, a_log_b, d_skip,
           ssm_norm_w, q_a_norm_w, w_q_b, kv_a_norm_w, w_kv_b, attn_norm_w, w_out, norm2_w, w_router,
           b_router, w_gate_up, b_gate_up, w_down, b_down, final_norm_w):
    b, s, _ = h.shape
    n = b * s
    x2 = h.reshape(n, D_MODEL)

    o_z, o_xbc = 0, D_SSM
    o_dtf = o_xbc + CONV_DIM
    o_dtb = o_dtf + SSM_HEADS
    o_qa = o_dtb + SSM_HEADS
    o_kva = o_qa + Q_LORA
    o_kpe = o_kva + KV_LORA
    w_z = w_in[:, o_z:o_xbc]
    w_xbc = w_in[:, o_xbc:o_dtf]
    w_qa = w_in[:, o_qa:o_kva]
    w_kva = w_in[:, o_kva:o_kpe]
    w_kpe = w_in[:, o_kpe:o_kpe + QK_ROPE]
    w_dt = w_in[:, o_dtf:o_qa]
    w_main = jnp.concatenate([w_z, w_xbc, w_kva, w_qa], axis=1).astype(BF16)
    w_misc = jnp.concatenate([_pad_cols(w_kpe, LANES), _pad_cols(_rot_cols(w_kpe), LANES),
                              _pad_cols(w_dt, LANES)], axis=1).astype(BF16)

    p2, misc2 = _in_proj(x2, norm1_w.reshape(1, D_MODEL), w_main, w_misc)

    xc = _conv(p2.reshape(b, s, P_COLS), conv_w, conv_b.reshape(1, CONV_DIM))
    par = jnp.zeros((8, LANES), F32)
    par = par.at[0, :2 * SSM_HEADS].set(jnp.concatenate([dt_bias_f, dt_bias_b]))
    par = par.at[1, :2 * SSM_HEADS].set(jnp.concatenate([a_log_f, a_log_b]))
    y_f, y_b = _ssd(xc, misc2.reshape(b, s, MISC_COLS), par)

    inv_freq = ROPE_THETA ** (-jnp.arange(0, QK_ROPE, 2, dtype=F32) / QK_ROPE)
    freq = _pad_cols(jnp.concatenate([inv_freq, inv_freq])[None, :], LANES)
    wq3 = w_q_b.reshape(Q_LORA, MLA_HEADS, QK_NOPE + QK_ROPE)
    wq_rope = wq3[:, :, QK_NOPE:]
    wq = jnp.pad(wq3, ((0, 0), (0, 0), (0, QK_PAD - QK_NOPE - QK_ROPE))).reshape(Q_LORA, MLA_HEADS * QK_PAD)
    wq_rot = jnp.concatenate([-wq_rope[:, :, QK_ROPE // 2:], wq_rope[:, :, :QK_ROPE // 2]], axis=2)
    wqr = jnp.pad(wq_rot, ((0, 0), (0, 0), (0, LANES - QK_ROPE))).reshape(Q_LORA, MLA_HEADS * LANES)
    q, k, v = _mla_prep(p2, misc2, positions.reshape(n, 1), freq, q_a_norm_w.reshape(1, Q_LORA),
                        kv_a_norm_w.reshape(1, KV_LORA), wq.astype(BF16), wqr.astype(BF16),
                        w_kv_b.astype(BF16), b, s)
    o, wgl, wdb = _attention(q, k, v, w_gate_up, w_down)

    dsk = jnp.repeat(d_skip, SSM_HEAD_DIM)[None, :]
    wr = _pad_cols(w_router, LANES)
    wr1 = wr.astype(BF16)
    wr2 = (wr - wr1.astype(F32)).astype(BF16)
    br = _pad_cols(b_router[None, :], LANES)
    h1, xn2, ri, rg, cnt = _out_proj(
        x2, y_f.reshape(n, D_SSM), y_b.reshape(n, D_SSM), xc.reshape(n, CONV_DIM), p2, o.reshape(n, MLA_HEADS * V_DIM),
        dsk, ssm_norm_w.reshape(1, D_SSM), attn_norm_w.reshape(1, MLA_HEADS * V_DIM), w_out.astype(BF16),
        norm2_w.reshape(1, D_MODEL), wr1, wr2, br)

    counts = cnt[0, :N_EXPERTS].astype(jnp.int32)
    nblk = (counts + MOE_BLK - 1) // MOE_BLK
    blk_end = jnp.cumsum(nblk)
    row_start = (blk_end - nblk) * MOE_BLK
    n_blocks = (n * TOP_K) // MOE_BLK + N_EXPERTS
    n_rows = n_blocks * MOE_BLK
    ids = ri[:, :TOP_K]
    dest = (row_start[ids] + ri[:, TOP_K:2 * TOP_K]).reshape(n * TOP_K)
    block_ids = jnp.arange(n_blocks, dtype=jnp.int32)
    block_expert = jnp.minimum(jnp.sum((blk_end[None, :] <= block_ids[:, None]).astype(jnp.int32), axis=1),
                               N_EXPERTS - 1)
    n_used = blk_end[-1:].astype(jnp.int32)
    rows_valid = jnp.clip(counts[block_expert] - (block_ids - (blk_end - nblk)[block_expert]) * MOE_BLK, 0, MOE_BLK)

    assign = (jnp.arange(TOP_K, dtype=jnp.int32)[None, :] * n + jnp.arange(n, dtype=jnp.int32)[:, None])
    inv = jnp.full((n_rows,), -1, jnp.int32).at[dest].set(assign.reshape(n * TOP_K), unique_indices=True)
    is_pad = inv < 0
    pad_pos = n * TOP_K + jnp.cumsum(is_pad.astype(jnp.int32)) - 1
    row_pos = jnp.where(is_pad, pad_pos, inv).reshape(n_blocks, 1, MOE_BLK)
    row_tok = jnp.where(is_pad, 0, inv % n).reshape(n_blocks, 1, MOE_BLK)

    bgl = b_gate_up.reshape(N_EXPERTS, D_EXPERT // LANES, LANES, 2).swapaxes(2, 3).reshape(N_EXPERTS, 1, 2 * D_EXPERT)
    y_rows = _moe(block_expert, n_used, rows_valid, row_tok, row_pos, xn2, wgl, wdb, bgl, b_down[:, None, :])

    out = _combine(y_rows, h1, rg, final_norm_w.reshape(1, D_MODEL))
    return out.reshape(b, s, D_MODEL)


def kernel(x, positions, norm1_w, w_in, conv_w, conv_b, dt_bias_f, dt_bias_b, a_log_f, a_log_b, d_skip,
           ssm_norm_w, q_a_norm_w, w_q_b, kv_a_norm_w, w_kv_b, attn_norm_w, w_out, norm2_w, w_router, b_router,
           w_gate_up, b_gate_up, w_down, b_down, final_norm_w):
    depth = norm1_w.shape[0]
    assert depth == 1, "the final norm is fused into the single layer's combine step"
    return _layer(x, positions, norm1_w[0], w_in[0], conv_w[0], conv_b[0], dt_bias_f[0], dt_bias_b[0],
                  a_log_f[0], a_log_b[0], d_skip[0], ssm_norm_w[0], q_a_norm_w[0], w_q_b[0], kv_a_norm_w[0],
                  w_kv_b[0], attn_norm_w[0], w_out[0], norm2_w[0], w_router[0], b_router[0], w_gate_up[0],
                  b_gate_up[0], w_down[0], b_down[0], final_norm_w)
```

```python
import functools
import math

import jax
import jax.numpy as jnp
from jax import lax
from jax.experimental import pallas as pl
from jax.experimental.pallas import tpu as pltpu

F32 = jnp.float32
BF16 = jnp.bfloat16

D_MODEL = 2048
D_SSM = 1024
SSM_HEAD_DIM = 64
SSM_HEADS = 16
SSM_GROUPS = 2
D_STATE = 128
D_CONV = 5
CONV_DIM = D_SSM + 2 * SSM_GROUPS * D_STATE
CHUNK = 128
MLA_HEADS = 8
QK_NOPE = 128
QK_ROPE = 64
V_DIM = 128
Q_LORA = 768
KV_LORA = 512
ROPE_THETA = 10000.0
N_EXPERTS = 32
TOP_K = 4
D_EXPERT = 2048
SWIGLU_ALPHA = 1.702
SWIGLU_LIMIT = 7.0
RMS_EPS = 1e-6
LOG2_E = 1.0 / math.log(2.0)

LANES = 128
QK_PAD = 256
P_COLS = D_SSM + CONV_DIM + KV_LORA + Q_LORA
MISC_COLS = 3 * LANES
VMEM_LIMIT = 56 * 1024 * 1024

TM_IN = 512
TM_PREP = 512
TQ = 256
TM_OUT = 256
MOE_BLK = 512
MOE_TC = 1024
TD = 256


def _cparams(*sem):
    return pltpu.CompilerParams(dimension_semantics=sem, vmem_limit_bytes=VMEM_LIMIT)


def _silu(x):
    return x / (1.0 + jnp.exp(-x))


def _softplus(x):
    return jnp.maximum(x, 0.0) + jnp.log(1.0 + jnp.exp(-jnp.abs(x)))


def _split3(x):
    x1 = x.astype(BF16)
    r = x - x1.astype(F32)
    x2 = r.astype(BF16)
    x3 = (r - x2.astype(F32)).astype(BF16)
    return x1, x2, x3


def _dot(a, b):
    return jnp.dot(a, b, preferred_element_type=F32)


def _dot_nt(a, b):
    return lax.dot_general(a, b, (((1,), (1,)), ((), ())), preferred_element_type=F32)


def _dot_exact_lhs(x, m, passes):
    parts = _split3(x)[:passes]
    acc = _dot(parts[0], m)
    for p in parts[1:]:
        acc = acc + _dot(p, m)
    return acc


def _in_proj_kernel(x_ref, nw_ref, w_ref, wm_ref, p_ref, m_ref):
    x = x_ref[...]
    ms = jnp.mean(x * x, axis=-1, keepdims=True)
    xn = (x * lax.rsqrt(ms + RMS_EPS) * nw_ref[...]).astype(BF16)
    m_ref[...] = _dot(xn, wm_ref[...])
    p_ref[...] = _dot(xn, w_ref[...]).astype(BF16)


def _resident(shape):
    return pl.BlockSpec(shape, lambda *_: (0, 0), pipeline_mode=pl.Buffered(1))


def _in_proj(x2, norm_w, w_main, w_misc):
    n = x2.shape[0]
    return pl.pallas_call(
        _in_proj_kernel,
        grid=(n // TM_IN,),
        in_specs=[
            pl.BlockSpec((TM_IN, D_MODEL), lambda i: (i, 0)),
            _resident((1, D_MODEL)),
            _resident((D_MODEL, P_COLS)),
            _resident((D_MODEL, MISC_COLS)),
        ],
        out_specs=[
            pl.BlockSpec((TM_IN, P_COLS), lambda i: (i, 0)),
            pl.BlockSpec((TM_IN, MISC_COLS), lambda i: (i, 0)),
        ],
        out_shape=[
            jax.ShapeDtypeStruct((n, P_COLS), BF16),
            jax.ShapeDtypeStruct((n, MISC_COLS), F32),
        ],
        compiler_params=_cparams("parallel"),
        name="in_proj",
    )(x2, norm_w, w_main, w_misc)


def _conv_kernel(u_ref, w_ref, b_ref, o_ref):
    u = u_ref[0].astype(F32)
    s = u.shape[0]
    row = lax.broadcasted_iota(jnp.int32, u.shape, 0)
    pad = D_CONV // 2
    acc = u * w_ref[pad:pad + 1, :] + b_ref[...]
    for k in range(D_CONV):
        off = k - pad
        if off == 0:
            continue
        shifted = pltpu.roll(u, (-off) % s, 0)
        valid = (row + off >= 0) & (row + off < s)
        acc = acc + jnp.where(valid, shifted, 0.0) * w_ref[k:k + 1, :]
    o_ref[0] = _silu(acc).astype(BF16)


def _conv(p3, conv_w, conv_b):
    b, s, _ = p3.shape
    first = D_SSM // LANES
    return pl.pallas_call(
        _conv_kernel,
        grid=(b, CONV_DIM // LANES),
        in_specs=[
            pl.BlockSpec((1, s, LANES), lambda i, c: (i, 0, first + c)),
            pl.BlockSpec((D_CONV, LANES), lambda i, c: (0, c)),
            pl.BlockSpec((1, LANES), lambda i, c: (0, c)),
        ],
        out_specs=pl.BlockSpec((1, s, LANES), lambda i, c: (i, 0, c)),
        out_shape=jax.ShapeDtypeStruct((b, s, CONV_DIM), BF16),
        compiler_params=_cparams("parallel", "parallel"),
        name="conv",
    )(p3, conv_w, conv_b)


def _ssd_direction(xc, dt_e, cs, cs_e, tot_row, state_ref, d, reverse):
    L = xc.shape[0]
    gw = D_SSM // SSM_GROUPS
    x = xc[:, :D_SSM].astype(F32)
    xdt = x * dt_e
    xdt_b = xdt.astype(BF16)
    tot_e = cs_e[tot_row:tot_row + 1, :]
    decay_out = jnp.exp(cs_e)
    xw = (xdt * jnp.exp(tot_e - cs_e)).astype(BF16)
    chunk_decay = jnp.exp(tot_e)

    cs_t = cs.T
    li = lax.broadcasted_iota(jnp.int32, (L, L), 0)
    si = lax.broadcasted_iota(jnp.int32, (L, L), 1)
    keep = (si >= li) if reverse else (si <= li)
    lane = lax.broadcasted_iota(jnp.int32, (L, LANES), 1)

    ys = []
    for g in range(SSM_GROUPS):
        bm = xc[:, D_SSM + g * D_STATE:D_SSM + (g + 1) * D_STATE]
        cm = xc[:, D_SSM + SSM_GROUPS * D_STATE + g * D_STATE:D_SSM + SSM_GROUPS * D_STATE + (g + 1) * D_STATE]
        cb = _dot_nt(cm, bm)
        st = state_ref[d, g]
        y_off = _dot(cm, st.astype(BF16)) * decay_out[:, g * gw:(g + 1) * gw]
        bm_t = bm.astype(F32).T.astype(BF16)
        state_ref[d, g] = st * chunk_decay[:, g * gw:(g + 1) * gw] + _dot(bm_t, xw[:, g * gw:(g + 1) * gw])

        pieces = []
        heads_per_group = SSM_HEADS // SSM_GROUPS
        for j in range(heads_per_group // 2):
            scores = []
            for hh in range(2):
                h = g * heads_per_group + 2 * j + hh
                hl = SSM_HEADS * d + h
                col = jnp.broadcast_to(cs[:, hl:hl + 1], (L, L))
                rowv = jnp.broadcast_to(cs_t[hl:hl + 1, :], (L, L))
                dec = jnp.exp(jnp.where(keep, col - rowv, -1e30))
                scores.append((cb * dec).astype(BF16))
            lhs = jnp.concatenate(scores, axis=1)
            c0 = g * gw + j * LANES
            xp = xdt_b[:, c0:c0 + LANES]
            zero = jnp.zeros_like(xp)
            rhs = jnp.concatenate([jnp.where(lane < SSM_HEAD_DIM, xp, zero),
                                   jnp.where(lane >= SSM_HEAD_DIM, xp, zero)], axis=0)
            pieces.append(_dot(lhs, rhs))
        ys.append(jnp.concatenate(pieces, axis=1) + y_off)
    return jnp.concatenate(ys, axis=1)


def _ssd_kernel(xf_ref, xb_ref, mf_ref, mb_ref, par_ref, yf_ref, yb_ref, state_ref):
    @pl.when(pl.program_id(1) == 0)
    def _():
        state_ref[...] = jnp.zeros_like(state_ref)

    L = CHUNK
    par = par_ref[...]
    lane = lax.broadcasted_iota(jnp.int32, (1, LANES), 1)
    active = lane < 2 * SSM_HEADS
    a_coef = jnp.where(active, -jnp.exp(par[1:2, :]), 0.0)
    r = lax.broadcasted_iota(jnp.int32, (L, L), 0)
    c = lax.broadcasted_iota(jnp.int32, (L, L), 1)
    tri_lo = jnp.where(c <= r, 1.0, 0.0).astype(BF16)
    tri_up = jnp.where(c >= r, 1.0, 0.0).astype(BF16)
    er = lax.broadcasted_iota(jnp.int32, (LANES, D_SSM), 0)
    ec = lax.broadcasted_iota(jnp.int32, (LANES, D_SSM), 1)

    for d, (x_ref, m_ref, y_ref) in enumerate(((xf_ref, mf_ref, yf_ref), (xb_ref, mb_ref, yb_ref))):
        reverse = d == 1
        expand = jnp.where(er == ec // SSM_HEAD_DIM + SSM_HEADS * d, 1.0, 0.0).astype(BF16)
        dt = jnp.where(active, _softplus(m_ref[0] + par[0:1, :]), 0.0)
        a = dt * a_coef
        a1, a2, a3 = _split3(a)
        tri = tri_up if reverse else tri_lo
        cs = _dot(tri, a1) + _dot(tri, a2) + _dot(tri, a3)
        dt_e = _dot_exact_lhs(dt, expand, 2)
        cs_e = _dot_exact_lhs(cs, expand, 3)
        y = _ssd_direction(x_ref[0], dt_e, cs, cs_e, 0 if reverse else L - 1, state_ref, d, reverse)
        y_ref[0] = y


def _ssd(xc, misc3, par):
    b, s, _ = xc.shape
    nc = s // CHUNK
    dtb = MISC_COLS // LANES - 1
    gw = D_SSM // SSM_GROUPS
    return pl.pallas_call(
        _ssd_kernel,
        grid=(b, nc),
        in_specs=[
            pl.BlockSpec((1, CHUNK, CONV_DIM), lambda i, c: (i, c, 0)),
            pl.BlockSpec((1, CHUNK, CONV_DIM), lambda i, c: (i, nc - 1 - c, 0)),
            pl.BlockSpec((1, CHUNK, LANES), lambda i, c: (i, c, dtb)),
            pl.BlockSpec((1, CHUNK, LANES), lambda i, c: (i, nc - 1 - c, dtb)),
            pl.BlockSpec((8, LANES), lambda i, c: (0, 0)),
        ],
        out_specs=[
            pl.BlockSpec((1, CHUNK, D_SSM), lambda i, c: (i, c, 0)),
            pl.BlockSpec((1, CHUNK, D_SSM), lambda i, c: (i, nc - 1 - c, 0)),
        ],
        out_shape=[jax.ShapeDtypeStruct((b, s, D_SSM), F32)] * 2,
        scratch_shapes=[pltpu.VMEM((2, SSM_GROUPS, D_STATE, gw), F32)],
        compiler_params=_cparams("parallel", "arbitrary"),
        name="ssd",
    )(xc, xc, misc3, misc3, par)


def _rms(x, w):
    ms = jnp.mean(x * x, axis=-1, keepdims=True)
    return x * lax.rsqrt(ms + RMS_EPS) * w


def _mla_prep_kernel(qa_ref, kva_ref, misc_ref, pos_ref, freq_ref, qnw_ref, kvnw_ref,
                     wq_ref, wqr_ref, wkv_ref, q_ref, k_ref, v_ref):
    scale = (QK_NOPE + QK_ROPE) ** -0.5 * LOG2_E
    qn = _rms(qa_ref[...].astype(F32), qnw_ref[...]).astype(BF16)
    kvn = _rms(kva_ref[...].astype(F32), kvnw_ref[...]).astype(BF16)
    ang = pos_ref[...].astype(F32) * freq_ref[...]
    cos = jnp.cos(ang)
    sin = jnp.sin(ang)
    qm = _dot(qn, wq_ref[...])
    qr = _dot(qn, wqr_ref[...])
    kv = _dot(kvn, wkv_ref[...])
    misc = misc_ref[...]
    k_rope = (misc[:, :LANES] * cos + misc[:, LANES:2 * LANES] * sin).astype(BF16)
    for h in range(MLA_HEADS):
        q_ref[0, h, :, :LANES] = (qm[:, h * QK_PAD:h * QK_PAD + LANES] * scale).astype(BF16)
        q_rope = qm[:, h * QK_PAD + LANES:(h + 1) * QK_PAD] * cos + qr[:, h * LANES:(h + 1) * LANES] * sin
        q_ref[0, h, :, LANES:] = (q_rope * scale).astype(BF16)
        k_ref[0, h, :, :LANES] = kv[:, h * 2 * LANES:h * 2 * LANES + LANES].astype(BF16)
        k_ref[0, h, :, LANES:] = k_rope
        v_ref[0, h] = kv[:, h * 2 * LANES + LANES:(h + 1) * 2 * LANES].astype(BF16)


def _mla_prep(p2, misc2, pos2, freq, qnw, kvnw, wq, wqr, wkv, b, s):
    n = p2.shape[0]
    spb = s // TM_PREP
    kv_blk = (D_SSM + CONV_DIM) // KV_LORA
    qa_blk = (D_SSM + CONV_DIM + KV_LORA) // Q_LORA
    out_idx = lambda i: (i // spb, 0, i % spb, 0)
    return pl.pallas_call(
        _mla_prep_kernel,
        grid=(n // TM_PREP,),
        in_specs=[
            pl.BlockSpec((TM_PREP, Q_LORA), lambda i: (i, qa_blk)),
            pl.BlockSpec((TM_PREP, KV_LORA), lambda i: (i, kv_blk)),
            pl.BlockSpec((TM_PREP, 2 * LANES), lambda i: (i, 0)),
            pl.BlockSpec((TM_PREP, 1), lambda i: (i, 0)),
            pl.BlockSpec((1, LANES), lambda i: (0, 0)),
            pl.BlockSpec((1, Q_LORA), lambda i: (0, 0)),
            pl.BlockSpec((1, KV_LORA), lambda i: (0, 0)),
            pl.BlockSpec((Q_LORA, MLA_HEADS * QK_PAD), lambda i: (0, 0)),
            pl.BlockSpec((Q_LORA, MLA_HEADS * LANES), lambda i: (0, 0)),
            pl.BlockSpec((KV_LORA, MLA_HEADS * 2 * LANES), lambda i: (0, 0)),
        ],
        out_specs=[
            pl.BlockSpec((1, MLA_HEADS, TM_PREP, QK_PAD), out_idx),
            pl.BlockSpec((1, MLA_HEADS, TM_PREP, QK_PAD), out_idx),
            pl.BlockSpec((1, MLA_HEADS, TM_PREP, V_DIM), out_idx),
        ],
        out_shape=[
            jax.ShapeDtypeStruct((b, MLA_HEADS, s, QK_PAD), BF16),
            jax.ShapeDtypeStruct((b, MLA_HEADS, s, QK_PAD), BF16),
            jax.ShapeDtypeStruct((b, MLA_HEADS, s, V_DIM), BF16),
        ],
        compiler_params=_cparams("parallel"),
        name="mla_prep",
    )(p2, p2, misc2, pos2, freq, qnw, kvnw, wq, wqr, wkv)


def _regroup_gate_lin(w_ref, o_ref):
    k = lax.broadcasted_iota(jnp.int32, (2 * LANES, 2 * LANES), 0)
    c = lax.broadcasted_iota(jnp.int32, (2 * LANES, 2 * LANES), 1)
    src = jnp.where(c < LANES, 2 * c, 2 * (c - LANES) + 1)
    perm = jnp.where(k == src, 1.0, 0.0).astype(BF16)
    for j in range(w_ref.shape[1] // (2 * LANES)):
        cols = slice(2 * LANES * j, 2 * LANES * (j + 1))
        o_ref[:, cols] = _dot(w_ref[:, cols].astype(BF16), perm).astype(BF16)


def _attn_kernel(q_ref, k_ref, v_ref, wgu_ref, wd_ref, o_ref, wgl_ref, wdb_ref):
    s = _dot_nt(q_ref[0, 0], k_ref[0, 0])
    m = jnp.max(s, axis=-1, keepdims=True)
    p = jnp.exp2(s - m)
    l = jnp.sum(p, axis=-1, keepdims=True)
    o = _dot(p.astype(BF16), v_ref[0, 0])
    o_ref[0] = (o / l).astype(BF16)
    _regroup_gate_lin(wgu_ref, wgl_ref)
    wdb_ref[...] = wd_ref[...].astype(BF16)


def _attention(q, k, v, w_gate_up, w_down):
    b, h, s, _ = q.shape
    nq = s // TQ
    steps = b * h * nq
    e, d, f2 = w_gate_up.shape
    f = w_down.shape[1]
    assert (e * d) % steps == 0 and (e * f) % steps == 0
    rows_gu, rows_d = e * d // steps, e * f // steps
    assert rows_gu % 16 == 0 and rows_d % 16 == 0 and rows_gu <= 512 and rows_d <= 512
    step = lambda i, j, t: ((i * h + j) * nq + t, 0)
    o, wgl, wdb = pl.pallas_call(
        _attn_kernel,
        grid=(b, h, nq),
        in_specs=[
            pl.BlockSpec((1, 1, TQ, QK_PAD), lambda i, j, t: (i, j, t, 0)),
            pl.BlockSpec((1, 1, s, QK_PAD), lambda i, j, t: (i, j, 0, 0)),
            pl.BlockSpec((1, 1, s, V_DIM), lambda i, j, t: (i, j, 0, 0)),
            pl.BlockSpec((rows_gu, f2), step),
            pl.BlockSpec((rows_d, d), step),
        ],
        out_specs=[
            pl.BlockSpec((1, TQ, V_DIM), lambda i, j, t: (i, t, j)),
            pl.BlockSpec((rows_gu, f2), step),
            pl.BlockSpec((rows_d, d), step),
        ],
        out_shape=[
            jax.ShapeDtypeStruct((b, s, h * V_DIM), BF16),
            jax.ShapeDtypeStruct((e * d, f2), BF16),
            jax.ShapeDtypeStruct((e * f, d), BF16),
        ],
        compiler_params=_cparams("parallel", "parallel", "arbitrary"),
        name="attn",
    )(q, k, v, w_gate_up.reshape(e * d, f2), w_down.reshape(e * f, d))
    return o, wgl.reshape(e, d, f2), wdb.reshape(e, f, d)


def _out_proj_kernel(x_ref, yf_ref, yb_ref, xs_ref, z_ref, o_ref, dsk_ref, gnw_ref, anw_ref,
                     wo_ref, n2w_ref, wr1_ref, wr2_ref, br_ref,
                     h_ref, xn_ref, ri_ref, rg_ref, cnt_ref, carry_ref):
    i = pl.program_id(0)

    @pl.when(i == 0)
    def _():
        carry_ref[...] = jnp.zeros_like(carry_ref)

    tm = x_ref.shape[0]
    y = yf_ref[...] + yb_ref[...] + xs_ref[...].astype(F32) * dsk_ref[...]
    y = y * _silu(z_ref[...].astype(F32))
    gs = D_SSM // SSM_GROUPS
    halves = []
    for g in range(SSM_GROUPS):
        yg = y[:, g * gs:(g + 1) * gs]
        halves.append(_rms(yg, gnw_ref[:, g * gs:(g + 1) * gs]).astype(BF16))
    y_att = _rms(o_ref[...].astype(F32), anw_ref[...]).astype(BF16)
    lhs = jnp.concatenate(halves + [y_att], axis=1)
    h1 = x_ref[...] + _dot(lhs, wo_ref[...])
    h_ref[...] = h1
    xn = _rms(h1, n2w_ref[...])
    xn_ref[...] = xn

    x1 = xn.astype(BF16)
    x2 = (xn - x1.astype(F32)).astype(BF16)
    logits = _dot(x1, wr1_ref[...]) + (_dot(x1, wr2_ref[...]) + _dot(x2, wr1_ref[...])) + br_ref[...]
    lane = lax.broadcasted_iota(jnp.int32, (tm, LANES), 1)
    neg = jnp.float32(-jnp.inf)
    work = jnp.where(lane < N_EXPERTS, logits, neg)
    vals, ids = [], []
    for _ in range(TOP_K):
        m = jnp.max(work, axis=-1, keepdims=True)
        idx = jnp.min(jnp.where(work == m, lane, LANES), axis=-1, keepdims=True)
        vals.append(m)
        ids.append(idx)
        work = jnp.where(lane == idx, neg, work)
    es = [jnp.exp(v - vals[0]) for v in vals]
    den = es[0] + es[1] + es[2] + es[3]
    sel = jnp.zeros((tm, LANES), F32)
    for idx in ids:
        sel = sel + jnp.where(lane == idx, 1.0, 0.0)
    r = lax.broadcasted_iota(jnp.int32, (tm, tm), 0)
    c = lax.broadcasted_iota(jnp.int32, (tm, tm), 1)
    strict = jnp.where(c < r, 1.0, 0.0).astype(BF16)
    cum = _dot(strict, sel.astype(BF16)) + carry_ref[0:1, :]
    ri = jnp.zeros((tm, LANES), jnp.int32)
    rg = jnp.zeros((tm, LANES), F32)
    for kk in range(TOP_K):
        rank = jnp.sum(jnp.where(lane == ids[kk], cum, 0.0), axis=-1, keepdims=True).astype(jnp.int32)
        ri = jnp.where(lane == kk, ids[kk], ri)
        ri = jnp.where(lane == TOP_K + kk, rank, ri)
        rg = jnp.where(lane == kk, es[kk] / den, rg)
    ri_ref[...] = ri
    rg_ref[...] = rg
    total = carry_ref[0:1, :] + jnp.sum(sel, axis=0, keepdims=True)
    carry_ref[...] = jnp.broadcast_to(total, carry_ref.shape)
    cnt_ref[...] = jnp.broadcast_to(total, cnt_ref.shape)


def _out_proj(x2, yf, yb, xc2, p2, o2, dsk, gnw, anw, wo, n2w, wr1, wr2, br):
    n = x2.shape[0]
    row = lambda i: (i, 0)
    fixed = lambda i: (0, 0)
    return pl.pallas_call(
        _out_proj_kernel,
        grid=(n // TM_OUT,),
        in_specs=[
            pl.BlockSpec((TM_OUT, D_MODEL), row),
            pl.BlockSpec((TM_OUT, D_SSM), row),
            pl.BlockSpec((TM_OUT, D_SSM), row),
            pl.BlockSpec((TM_OUT, D_SSM), row),
            pl.BlockSpec((TM_OUT, D_SSM), row),
            pl.BlockSpec((TM_OUT, MLA_HEADS * V_DIM), row),
            pl.BlockSpec((1, D_SSM), fixed),
            pl.BlockSpec((1, D_SSM), fixed),
            pl.BlockSpec((1, MLA_HEADS * V_DIM), fixed),
            pl.BlockSpec((D_MODEL, D_MODEL), fixed),
            pl.BlockSpec((1, D_MODEL), fixed),
            pl.BlockSpec((D_MODEL, LANES), fixed),
            pl.BlockSpec((D_MODEL, LANES), fixed),
            pl.BlockSpec((1, LANES), fixed),
        ],
        out_specs=[
            pl.BlockSpec((TM_OUT, D_MODEL), row),
            pl.BlockSpec((TM_OUT, D_MODEL), row),
            pl.BlockSpec((TM_OUT, LANES), row),
            pl.BlockSpec((TM_OUT, LANES), row),
            pl.BlockSpec((8, LANES), fixed),
        ],
        out_shape=[
            jax.ShapeDtypeStruct((n, D_MODEL), F32),
            jax.ShapeDtypeStruct((n, D_MODEL), F32),
            jax.ShapeDtypeStruct((n, LANES), jnp.int32),
            jax.ShapeDtypeStruct((n, LANES), F32),
            jax.ShapeDtypeStruct((8, LANES), F32),
        ],
        scratch_shapes=[pltpu.VMEM((8, LANES), F32)],
        compiler_params=_cparams("arbitrary"),
        name="out_proj",
    )(x2, yf, yb, xc2, p2, o2, dsk, gnw, anw, wo, n2w, wr1, wr2, br)


def _moe_kernel(be_ref, nu_ref, nv_ref, tok_ref, tokn_ref, pos_ref, x_hbm, wgl_ref, wd_ref, bgl_ref, bd_ref,
                y_hbm, xg_ref, xb_ref, acc_ref, yb_ref, gsem, ssem):
    del be_ref
    blk = pl.program_id(0)
    hc = pl.program_id(1)
    nblk = pl.num_programs(0)
    last = pl.num_programs(1) - 1
    nu = nu_ref[0]
    slot = lax.rem(blk, 2)

    def gather(src_row, s, j):
        return pltpu.make_async_copy(x_hbm.at[pl.ds(src_row, 1)], xg_ref.at[s, pl.ds(j, 1)], gsem.at[s])

    def scatter(j, dst_row):
        return pltpu.make_async_copy(yb_ref.at[pl.ds(j, 1)], y_hbm.at[pl.ds(dst_row, 1)], ssem)

    def start_gather(rows_ref, s):
        for j in range(MOE_BLK):
            gather(rows_ref[0, 0, j], s, j).start(priority=1)

    def wait_gather(s):
        for _ in range(MOE_BLK):
            gather(0, s, 0).wait()

    def start_scatter():
        for j in range(MOE_BLK):
            scatter(j, pos_ref[0, 0, j]).start()

    def wait_scatter():
        for _ in range(MOE_BLK):
            scatter(0, 0).wait()

    def compute(rows):
        hdn = _dot(xb_ref[:rows], wgl_ref[0]) + bgl_ref[0]
        acts = []
        for j in range(MOE_TC // LANES):
            glu = jnp.minimum(hdn[:, 2 * LANES * j:2 * LANES * j + LANES], SWIGLU_LIMIT)
            lin = jnp.clip(hdn[:, 2 * LANES * j + LANES:2 * LANES * (j + 1)], -SWIGLU_LIMIT, SWIGLU_LIMIT)
            acts.append((glu / (1.0 + jnp.exp(-SWIGLU_ALPHA * glu)) * (lin + 1.0)).astype(BF16))
        part = _dot(jnp.concatenate(acts, axis=1), wd_ref[0])

        @pl.when(hc == 0)
        def _():
            acc_ref[:rows] = part + bd_ref[0]

        @pl.when((hc > 0) & (hc < last))
        def _():
            acc_ref[:rows] += part

        @pl.when(hc == last)
        def _():
            @pl.when(blk > 0)
            def _():
                wait_scatter()

            yb_ref[:rows] = acc_ref[:rows] + part

    @pl.when(blk < nu)
    def _():
        @pl.when(hc == 0)
        def _():
            @pl.when(blk == 0)
            def _():
                yb_ref[...] = jnp.zeros_like(yb_ref)
                start_gather(tok_ref, 0)

            wait_gather(slot)
            xb_ref[...] = xg_ref[slot].astype(BF16)

            @pl.when(blk + 1 < nu)
            def _():
                start_gather(tokn_ref, 1 - slot)

        half = nv_ref[blk] <= MOE_BLK // 2

        @pl.when(half)
        def _():
            compute(MOE_BLK // 2)

        @pl.when(jnp.logical_not(half))
        def _():
            compute(MOE_BLK)

        @pl.when(hc == last)
        def _():
            start_scatter()

    @pl.when((blk >= nu) & (hc == last))
    def _():
        wait_scatter()
        yb_ref[...] = jnp.zeros_like(yb_ref)
        start_scatter()

    @pl.when((blk == nblk - 1) & (hc == last))
    def _():
        wait_scatter()


def _moe(block_expert, n_used, rows_valid, row_tok, row_pos, xn2, wgl, wd, bgl, bd):
    nb = row_tok.shape[0]
    nh = D_EXPERT // MOE_TC
    assert nh >= 2

    def bb(b, nu):
        return jnp.minimum(b, nu[0] - 1)

    def hh(b, h, nu):
        order = jnp.where(b % 2 == 0, h, nh - 1 - h)
        return jnp.where(b < nu[0], order, nh - 1 - (nu[0] - 1) % 2 * (nh - 1))

    smem_rows = lambda index_map: pl.BlockSpec((1, 1, MOE_BLK), index_map, memory_space=pltpu.SMEM)
    grid_spec = pltpu.PrefetchScalarGridSpec(
        num_scalar_prefetch=3,
        grid=(nb, nh),
        in_specs=[
            smem_rows(lambda b, h, be, nu, nv: (b, 0, 0)),
            smem_rows(lambda b, h, be, nu, nv: (jnp.minimum(b + 1, nb - 1), 0, 0)),
            smem_rows(lambda b, h, be, nu, nv: (b, 0, 0)),
            pl.BlockSpec(memory_space=pl.ANY),
            pl.BlockSpec((1, D_MODEL, 2 * MOE_TC), lambda b, h, be, nu, nv: (be[bb(b, nu)], 0, hh(b, h, nu))),
            pl.BlockSpec((1, MOE_TC, D_MODEL), lambda b, h, be, nu, nv: (be[bb(b, nu)], hh(b, h, nu), 0)),
            pl.BlockSpec((1, 1, 2 * MOE_TC), lambda b, h, be, nu, nv: (be[bb(b, nu)], 0, hh(b, h, nu))),
            pl.BlockSpec((1, 1, D_MODEL), lambda b, h, be, nu, nv: (be[bb(b, nu)], 0, 0)),
        ],
        out_specs=pl.BlockSpec(memory_space=pl.ANY),
        scratch_shapes=[
            pltpu.VMEM((2, MOE_BLK, D_MODEL), F32),
            pltpu.VMEM((MOE_BLK, D_MODEL), BF16),
            pltpu.VMEM((MOE_BLK, D_MODEL), F32),
            pltpu.VMEM((MOE_BLK, D_MODEL), F32),
            pltpu.SemaphoreType.DMA((2,)),
            pltpu.SemaphoreType.DMA(()),
        ],
    )
    return pl.pallas_call(
        _moe_kernel,
        grid_spec=grid_spec,
        out_shape=jax.ShapeDtypeStruct((nb * MOE_BLK, D_MODEL), F32),
        compiler_params=_cparams("arbitrary", "arbitrary"),
        name="moe",
    )(block_expert, n_used, rows_valid, row_tok, row_tok, row_pos, xn2, wgl, wd, bgl, bd)


def _combine_kernel(y0_ref, y1_ref, y2_ref, y3_ref, h_ref, rg_ref, fw_ref, o_ref):
    rg = rg_ref[...]
    out = h_ref[...]
    for k, y_ref in enumerate((y0_ref, y1_ref, y2_ref, y3_ref)):
        out = out + y_ref[...] * rg[:, k:k + 1]
    o_ref[...] = _rms(out, fw_ref[...])


def _combine(y_rows, h1, rg, final_w):
    n = h1.shape[0]
    nt = n // TD
    row = lambda i: (i, 0)
    slot_spec = lambda k: pl.BlockSpec((TD, D_MODEL), lambda i: (k * nt + i, 0))
    return pl.pallas_call(
        _combine_kernel,
        grid=(nt,),
        in_specs=[slot_spec(k) for k in range(TOP_K)] + [
            pl.BlockSpec((TD, D_MODEL), row),
            pl.BlockSpec((TD, LANES), row),
            pl.BlockSpec((1, D_MODEL), lambda i: (0, 0)),
        ],
        out_specs=pl.BlockSpec((TD, D_MODEL), row),
        out_shape=jax.ShapeDtypeStruct((n, D_MODEL), F32),
        compiler_params=_cparams("parallel"),
        name="combine",
    )(y_rows, y_rows, y_rows, y_rows, h1, rg, final_w)


def _pad_cols(w, width):
    return jnp.pad(w, ((0, 0), (0, width - w.shape[1])))


def _rot_cols(w):
    half = w.shape[1] // 2
    return jnp.concatenate([-w[:, half:], w[:, :half]], axis=1)


def _layer(h, positions, norm1_w, w_in, conv_w, conv_b, dt_bias_f, dt_bias_b, a_log_f, a_log_b, d_skip,
           ssm_norm_w, q_a_norm_w, w_q_b, kv_a_norm_w, w_kv_b, attn_norm_w, w_out, norm2_w, w_router,
           b_router, w_gate_up, b_gate_up, w_down, b_down, final_norm_w):
    b, s, _ = h.shape
    n = b * s
    x2 = h.reshape(n, D_MODEL)

    o_z, o_xbc = 0, D_SSM
    o_dtf = o_xbc + CONV_DIM
    o_dtb = o_dtf + SSM_HEADS
    o_qa = o_dtb + SSM_HEADS
    o_kva = o_qa + Q_LORA
    o_kpe = o_kva + KV_LORA
    w_z = w_in[:, o_z:o_xbc]
    w_xbc = w_in[:, o_xbc:o_dtf]
    w_qa = w_in[:, o_qa:o_kva]
    w_kva = w_in[:, o_kva:o_kpe]
    w_kpe = w_in[:, o_kpe:o_kpe + QK_ROPE]
    w_dt = w_in[:, o_dtf:o_qa]
    w_main = jnp.concatenate([w_z, w_xbc, w_kva, w_qa], axis=1).astype(BF16)
    w_misc = jnp.concatenate([_pad_cols(w_kpe, LANES), _pad_cols(_rot_cols(w_kpe), LANES),
                              _pad_cols(w_dt, LANES)], axis=1).astype(BF16)

    p2, misc2 = _in_proj(x2, norm1_w.reshape(1, D_MODEL), w_main, w_misc)

    xc = _conv(p2.reshape(b, s, P_COLS), conv_w, conv_b.reshape(1, CONV_DIM))
    par = jnp.zeros((8, LANES), F32)
    par = par.at[0, :2 * SSM_HEADS].set(jnp.concatenate([dt_bias_f, dt_bias_b]))
    par = par.at[1, :2 * SSM_HEADS].set(jnp.concatenate([a_log_f, a_log_b]))
    y_f, y_b = _ssd(xc, misc2.reshape(b, s, MISC_COLS), par)

    inv_freq = ROPE_THETA ** (-jnp.arange(0, QK_ROPE, 2, dtype=F32) / QK_ROPE)
    freq = _pad_cols(jnp.concatenate([inv_freq, inv_freq])[None, :], LANES)
    wq3 = w_q_b.reshape(Q_LORA, MLA_HEADS, QK_NOPE + QK_ROPE)
    wq_rope = wq3[:, :, QK_NOPE:]
    wq = jnp.pad(wq3, ((0, 0), (0, 0), (0, QK_PAD - QK_NOPE - QK_ROPE))).reshape(Q_LORA, MLA_HEADS * QK_PAD)
    wq_rot = jnp.concatenate([-wq_rope[:, :, QK_ROPE // 2:], wq_rope[:, :, :QK_ROPE // 2]], axis=2)
    wqr = jnp.pad(wq_rot, ((0, 0), (0, 0), (0, LANES - QK_ROPE))).reshape(Q_LORA, MLA_HEADS * LANES)
    q, k, v = _mla_prep(p2, misc2, positions.reshape(n, 1), freq, q_a_norm_w.reshape(1, Q_LORA),
                        kv_a_norm_w.reshape(1, KV_LORA), wq.astype(BF16), wqr.astype(BF16),
                        w_kv_b.astype(BF16), b, s)
    o, wgl, wdb = _attention(q, k, v, w_gate_up, w_down)

    dsk = jnp.repeat(d_skip, SSM_HEAD_DIM)[None, :]
    wr = _pad_cols(w_router, LANES)
    wr1 = wr.astype(BF16)
    wr2 = (wr - wr1.astype(F32)).astype(BF16)
    br = _pad_cols(b_router[None, :], LANES)
    h1, xn2, ri, rg, cnt = _out_proj(
        x2, y_f.reshape(n, D_SSM), y_b.reshape(n, D_SSM), xc.reshape(n, CONV_DIM), p2, o.reshape(n, MLA_HEADS * V_DIM),
        dsk, ssm_norm_w.reshape(1, D_SSM), attn_norm_w.reshape(1, MLA_HEADS * V_DIM), w_out.astype(BF16),
        norm2_w.reshape(1, D_MODEL), wr1, wr2, br)

    counts = cnt[0, :N_EXPERTS].astype(jnp.int32)
    nblk = (counts + MOE_BLK - 1) // MOE_BLK
    blk_end = jnp.cumsum(nblk)
    row_start = (blk_end - nblk) * MOE_BLK
    n_blocks = (n * TOP_K) // MOE_BLK + N_EXPERTS
    n_rows = n_blocks * MOE_BLK
    ids = ri[:, :TOP_K]
    dest = (row_start[ids] + ri[:, TOP_K:2 * TOP_K]).reshape(n * TOP_K)
    block_ids = jnp.arange(n_blocks, dtype=jnp.int32)
    block_expert = jnp.minimum(jnp.sum((blk_end[None, :] <= block_ids[:, None]).astype(jnp.int32), axis=1),
                               N_EXPERTS - 1)
    n_used = blk_end[-1:].astype(jnp.int32)
    rows_valid = jnp.clip(counts[block_expert] - (block_ids - (blk_end - nblk)[block_expert]) * MOE_BLK, 0, MOE_BLK)

    assign = (jnp.arange(TOP_K, dtype=jnp.int32)[None, :] * n + jnp.arange(n, dtype=jnp.int32)[:, None])
    inv = jnp.full((n_rows,), -1, jnp.int32).at[dest].set(assign.reshape(n * TOP_K), unique_indices=True)
    is_pad = inv < 0
    pad_pos = n * TOP_K + jnp.cumsum(is_pad.astype(jnp.int32)) - 1
    row_pos = jnp.where(is_pad, pad_pos, inv).reshape(n_blocks, 1, MOE_BLK)
    row_tok = jnp.where(is_pad, 0, inv % n).reshape(n_blocks, 1, MOE_BLK)

    bgl = b_gate_up.reshape(N_EXPERTS, D_EXPERT // LANES, LANES, 2).swapaxes(2, 3).reshape(N_EXPERTS, 1, 2 * D_EXPERT)
    y_rows = _moe(block_expert, n_used, rows_valid, row_tok, row_pos, xn2, wgl, wdb, bgl, b_down[:, None, :])

    out = _combine(y_rows, h1, rg, final_norm_w.reshape(1, D_MODEL))
    return out.reshape(b, s, D_MODEL)


def kernel(x, positions, norm1_w, w_in, conv_w, conv_b, dt_bias_f, dt_bias_b, a_log_f, a_log_b, d_skip,
           ssm_norm_w, q_a_norm_w, w_q_b, kv_a_norm_w, w_kv_b, attn_norm_w, w_out, norm2_w, w_router, b_router,
           w_gate_up, b_gate_up, w_down, b_down, final_norm_w):
    depth = norm1_w.shape[0]
    assert depth == 1, "the final norm is fused into the single layer's combine step"
    return _layer(x, positions, norm1_w[0], w_in[0], conv_w[0], conv_b[0], dt_bias_f[0], dt_bias_b[0],
                  a_log_f[0], a_log_b[0], d_skip[0], ssm_norm_w[0], q_a_norm_w[0], w_q_b[0], kv_a_norm_w[0],
                  w_kv_b[0], attn_norm_w[0], w_out[0], norm2_w[0], w_router[0], b_router[0], w_gate_up[0],
                  b_gate_up[0], w_down[0], b_down[0], final_norm_w)
```

```python
import math

import jax
import jax.numpy as jnp
from jax import lax
from jax.experimental import pallas as pl
from jax.experimental.pallas import tpu as pltpu

F32 = jnp.float32
BF16 = jnp.bfloat16

D_MODEL = 2048
D_SSM = 1024
SSM_HEAD_DIM = 64
SSM_HEADS = 16
SSM_GROUPS = 2
D_STATE = 128
D_CONV = 5
CONV_DIM = D_SSM + 2 * SSM_GROUPS * D_STATE
CHUNK = 128
MLA_HEADS = 8
QK_NOPE = 128
QK_ROPE = 64
V_DIM = 128
Q_LORA = 768
KV_LORA = 512
ROPE_THETA = 10000.0
N_EXPERTS = 32
TOP_K = 4
D_EXPERT = 2048
SWIGLU_ALPHA = 1.702
SWIGLU_LIMIT = 7.0
RMS_EPS = 1e-6
LOG2_E = 1.0 / math.log(2.0)

LANES = 128
QK_PAD = 256
P_COLS = D_SSM + CONV_DIM + KV_LORA + Q_LORA
MISC_COLS = 3 * LANES
VMEM_LIMIT = 56 * 1024 * 1024

TM_IN = 512
TM_PREP = 512
TQ = 256
HEADS_PER_STEP = 2
TM_OUT = 256
MOE_BLK = 512
MOE_TC = 1024
TD = 256


def _cparams(*sem):
    return pltpu.CompilerParams(dimension_semantics=sem, vmem_limit_bytes=VMEM_LIMIT)


def _silu(x):
    return x / (1.0 + jnp.exp(-x))


def _softplus(x):
    return jnp.maximum(x, 0.0) + jnp.log(1.0 + jnp.exp(-jnp.abs(x)))


def _split3(x):
    x1 = x.astype(BF16)
    r = x - x1.astype(F32)
    x2 = r.astype(BF16)
    x3 = (r - x2.astype(F32)).astype(BF16)
    return x1, x2, x3


def _dot(a, b):
    return jnp.dot(a, b, preferred_element_type=F32)


def _dot_nt(a, b):
    return lax.dot_general(a, b, (((1,), (1,)), ((), ())), preferred_element_type=F32)


def _dot_exact_lhs(x, m, passes):
    parts = _split3(x)[:passes]
    acc = _dot(parts[0], m)
    for p in parts[1:]:
        acc = acc + _dot(p, m)
    return acc


def _in_proj_kernel(x_ref, nw_ref, w_ref, wm_ref, p_ref, m_ref):
    x = x_ref[...]
    ms = jnp.mean(x * x, axis=-1, keepdims=True)
    xn = (x * lax.rsqrt(ms + RMS_EPS) * nw_ref[...]).astype(BF16)
    m_ref[...] = _dot(xn, wm_ref[...])
    p_ref[...] = _dot(xn, w_ref[...]).astype(BF16)


def _resident(shape):
    return pl.BlockSpec(shape, lambda *_: (0, 0), pipeline_mode=pl.Buffered(1))


def _in_proj(x2, norm_w, w_main, w_misc):
    n = x2.shape[0]
    return pl.pallas_call(
        _in_proj_kernel,
        grid=(n // TM_IN,),
        in_specs=[
            pl.BlockSpec((TM_IN, D_MODEL), lambda i: (i, 0)),
            _resident((1, D_MODEL)),
            _resident((D_MODEL, P_COLS)),
            _resident((D_MODEL, MISC_COLS)),
        ],
        out_specs=[
            pl.BlockSpec((TM_IN, P_COLS), lambda i: (i, 0)),
            pl.BlockSpec((TM_IN, MISC_COLS), lambda i: (i, 0)),
        ],
        out_shape=[
            jax.ShapeDtypeStruct((n, P_COLS), BF16),
            jax.ShapeDtypeStruct((n, MISC_COLS), F32),
        ],
        compiler_params=_cparams("parallel"),
        name="in_proj",
    )(x2, norm_w, w_main, w_misc)


def _conv_kernel(u_ref, w_ref, b_ref, o_ref):
    u = u_ref[0].astype(F32)
    s = u.shape[0]
    row = lax.broadcasted_iota(jnp.int32, u.shape, 0)
    pad = D_CONV // 2
    acc = u * w_ref[pad:pad + 1, :] + b_ref[...]
    for k in range(D_CONV):
        off = k - pad
        if off == 0:
            continue
        shifted = pltpu.roll(u, (-off) % s, 0)
        valid = (row + off >= 0) & (row + off < s)
        acc = acc + jnp.where(valid, shifted, 0.0) * w_ref[k:k + 1, :]
    o_ref[0] = _silu(acc).astype(BF16)


def _conv(p3, conv_w, conv_b):
    b, s, _ = p3.shape
    first = D_SSM // LANES
    return pl.pallas_call(
        _conv_kernel,
        grid=(b, CONV_DIM // LANES),
        in_specs=[
            pl.BlockSpec((1, s, LANES), lambda i, c: (i, 0, first + c)),
            pl.BlockSpec((D_CONV, LANES), lambda i, c: (0, c)),
            pl.BlockSpec((1, LANES), lambda i, c: (0, c)),
        ],
        out_specs=pl.BlockSpec((1, s, LANES), lambda i, c: (i, 0, c)),
        out_shape=jax.ShapeDtypeStruct((b, s, CONV_DIM), BF16),
        compiler_params=_cparams("parallel", "parallel"),
        name="conv",
    )(p3, conv_w, conv_b)


def _ssd_direction(xc, dt_e, cs, cs_e, tot_row, state_ref, d, reverse):
    L = xc.shape[0]
    gw = D_SSM // SSM_GROUPS
    x = xc[:, :D_SSM].astype(F32)
    xdt = x * dt_e
    xdt_b = xdt.astype(BF16)
    tot_e = cs_e[tot_row:tot_row + 1, :]
    decay_out = jnp.exp(cs_e)
    xw = (xdt * jnp.exp(tot_e - cs_e)).astype(BF16)
    chunk_decay = jnp.exp(tot_e)

    cs_t = cs.T
    li = lax.broadcasted_iota(jnp.int32, (L, L), 0)
    si = lax.broadcasted_iota(jnp.int32, (L, L), 1)
    keep = (si >= li) if reverse else (si <= li)
    lane = lax.broadcasted_iota(jnp.int32, (L, LANES), 1)

    ys = []
    for g in range(SSM_GROUPS):
        bm = xc[:, D_SSM + g * D_STATE:D_SSM + (g + 1) * D_STATE]
        cm = xc[:, D_SSM + SSM_GROUPS * D_STATE + g * D_STATE:D_SSM + SSM_GROUPS * D_STATE + (g + 1) * D_STATE]
        cb = _dot_nt(cm, bm)
        st = state_ref[d, g]
        y_off = _dot(cm, st.astype(BF16)) * decay_out[:, g * gw:(g + 1) * gw]
        bm_t = bm.astype(F32).T.astype(BF16)
        state_ref[d, g] = st * chunk_decay[:, g * gw:(g + 1) * gw] + _dot(bm_t, xw[:, g * gw:(g + 1) * gw])

        pieces = []
        heads_per_group = SSM_HEADS // SSM_GROUPS
        for j in range(heads_per_group // 2):
            scores = []
            for hh in range(2):
                h = g * heads_per_group + 2 * j + hh
                hl = SSM_HEADS * d + h
                col = jnp.broadcast_to(cs[:, hl:hl + 1], (L, L))
                rowv = jnp.broadcast_to(cs_t[hl:hl + 1, :], (L, L))
                dec = jnp.exp(jnp.where(keep, col - rowv, -1e30))
                scores.append((cb * dec).astype(BF16))
            lhs = jnp.concatenate(scores, axis=1)
            c0 = g * gw + j * LANES
            xp = xdt_b[:, c0:c0 + LANES]
            zero = jnp.zeros_like(xp)
            rhs = jnp.concatenate([jnp.where(lane < SSM_HEAD_DIM, xp, zero),
                                   jnp.where(lane >= SSM_HEAD_DIM, xp, zero)], axis=0)
            pieces.append(_dot(lhs, rhs))
        ys.append(jnp.concatenate(pieces, axis=1) + y_off)
    return jnp.concatenate(ys, axis=1)


def _ssd_kernel(xf_ref, xb_ref, mf_ref, mb_ref, par_ref, yf_ref, yb_ref, state_ref):
    @pl.when(pl.program_id(1) == 0)
    def _():
        state_ref[...] = jnp.zeros_like(state_ref)

    L = CHUNK
    par = par_ref[...]
    lane = lax.broadcasted_iota(jnp.int32, (1, LANES), 1)
    active = lane < 2 * SSM_HEADS
    a_coef = jnp.where(active, -jnp.exp(par[1:2, :]), 0.0)
    r = lax.broadcasted_iota(jnp.int32, (L, L), 0)
    c = lax.broadcasted_iota(jnp.int32, (L, L), 1)
    tri_lo = jnp.where(c <= r, 1.0, 0.0).astype(BF16)
    tri_up = jnp.where(c >= r, 1.0, 0.0).astype(BF16)
    er = lax.broadcasted_iota(jnp.int32, (LANES, D_SSM), 0)
    ec = lax.broadcasted_iota(jnp.int32, (LANES, D_SSM), 1)

    for d, (x_ref, m_ref, y_ref) in enumerate(((xf_ref, mf_ref, yf_ref), (xb_ref, mb_ref, yb_ref))):
        reverse = d == 1
        expand = jnp.where(er == ec // SSM_HEAD_DIM + SSM_HEADS * d, 1.0, 0.0).astype(BF16)
        dt = jnp.where(active, _softplus(m_ref[0] + par[0:1, :]), 0.0)
        a = dt * a_coef
        a1, a2, a3 = _split3(a)
        tri = tri_up if reverse else tri_lo
        cs = _dot(tri, a1) + _dot(tri, a2) + _dot(tri, a3)
        dt_e = _dot_exact_lhs(dt, expand, 2)
        cs_e = _dot_exact_lhs(cs, expand, 3)
        y = _ssd_direction(x_ref[0], dt_e, cs, cs_e, 0 if reverse else L - 1, state_ref, d, reverse)
        y_ref[0] = y


def _ssd(xc, misc3, par):
    b, s, _ = xc.shape
    nc = s // CHUNK
    dtb = MISC_COLS // LANES - 1
    gw = D_SSM // SSM_GROUPS
    return pl.pallas_call(
        _ssd_kernel,
        grid=(b, nc),
        in_specs=[
            pl.BlockSpec((1, CHUNK, CONV_DIM), lambda i, c: (i, c, 0)),
            pl.BlockSpec((1, CHUNK, CONV_DIM), lambda i, c: (i, nc - 1 - c, 0)),
            pl.BlockSpec((1, CHUNK, LANES), lambda i, c: (i, c, dtb)),
            pl.BlockSpec((1, CHUNK, LANES), lambda i, c: (i, nc - 1 - c, dtb)),
            pl.BlockSpec((8, LANES), lambda i, c: (0, 0)),
        ],
        out_specs=[
            pl.BlockSpec((1, CHUNK, D_SSM), lambda i, c: (i, c, 0)),
            pl.BlockSpec((1, CHUNK, D_SSM), lambda i, c: (i, nc - 1 - c, 0)),
        ],
        out_shape=[jax.ShapeDtypeStruct((b, s, D_SSM), F32)] * 2,
        scratch_shapes=[pltpu.VMEM((2, SSM_GROUPS, D_STATE, gw), F32)],
        compiler_params=_cparams("parallel", "arbitrary"),
        name="ssd",
    )(xc, xc, misc3, misc3, par)


def _rms(x, w):
    ms = jnp.mean(x * x, axis=-1, keepdims=True)
    return x * lax.rsqrt(ms + RMS_EPS) * w


def _mla_prep_kernel(qa_ref, kva_ref, misc_ref, pos_ref, freq_ref, qnw_ref, kvnw_ref,
                     wq_ref, wqr_ref, wkv_ref, q_ref, k_ref, v_ref):
    scale = (QK_NOPE + QK_ROPE) ** -0.5 * LOG2_E
    qn = _rms(qa_ref[...].astype(F32), qnw_ref[...]).astype(BF16)
    kvn = _rms(kva_ref[...].astype(F32), kvnw_ref[...]).astype(BF16)
    ang = pos_ref[...].astype(F32) * freq_ref[...]
    cos = jnp.cos(ang)
    sin = jnp.sin(ang)
    qm = _dot(qn, wq_ref[...])
    qr = _dot(qn, wqr_ref[...])
    kv = _dot(kvn, wkv_ref[...])
    misc = misc_ref[...]
    k_rope = (misc[:, :LANES] * cos + misc[:, LANES:2 * LANES] * sin).astype(BF16)
    for h in range(MLA_HEADS):
        q_ref[0, h, :, :LANES] = (qm[:, h * QK_PAD:h * QK_PAD + LANES] * scale).astype(BF16)
        q_rope = qm[:, h * QK_PAD + LANES:(h + 1) * QK_PAD] * cos + qr[:, h * LANES:(h + 1) * LANES] * sin
        q_ref[0, h, :, LANES:] = (q_rope * scale).astype(BF16)
        k_ref[0, h, :, :LANES] = kv[:, h * 2 * LANES:h * 2 * LANES + LANES].astype(BF16)
        k_ref[0, h, :, LANES:] = k_rope
        v_ref[0, h] = kv[:, h * 2 * LANES + LANES:(h + 1) * 2 * LANES].astype(BF16)


def _mla_prep(p2, misc2, pos2, freq, qnw, kvnw, wq, wqr, wkv, b, s):
    n = p2.shape[0]
    spb = s // TM_PREP
    kv_blk = (D_SSM + CONV_DIM) // KV_LORA
    qa_blk = (D_SSM + CONV_DIM + KV_LORA) // Q_LORA
    out_idx = lambda i: (i // spb, 0, i % spb, 0)
    return pl.pallas_call(
        _mla_prep_kernel,
        grid=(n // TM_PREP,),
        in_specs=[
            pl.BlockSpec((TM_PREP, Q_LORA), lambda i: (i, qa_blk)),
            pl.BlockSpec((TM_PREP, KV_LORA), lambda i: (i, kv_blk)),
            pl.BlockSpec((TM_PREP, 2 * LANES), lambda i: (i, 0)),
            pl.BlockSpec((TM_PREP, 1), lambda i: (i, 0)),
            pl.BlockSpec((1, LANES), lambda i: (0, 0)),
            pl.BlockSpec((1, Q_LORA), lambda i: (0, 0)),
            pl.BlockSpec((1, KV_LORA), lambda i: (0, 0)),
            pl.BlockSpec((Q_LORA, MLA_HEADS * QK_PAD), lambda i: (0, 0)),
            pl.BlockSpec((Q_LORA, MLA_HEADS * LANES), lambda i: (0, 0)),
            pl.BlockSpec((KV_LORA, MLA_HEADS * 2 * LANES), lambda i: (0, 0)),
        ],
        out_specs=[
            pl.BlockSpec((1, MLA_HEADS, TM_PREP, QK_PAD), out_idx),
            pl.BlockSpec((1, MLA_HEADS, TM_PREP, QK_PAD), out_idx),
            pl.BlockSpec((1, MLA_HEADS, TM_PREP, V_DIM), out_idx),
        ],
        out_shape=[
            jax.ShapeDtypeStruct((b, MLA_HEADS, s, QK_PAD), BF16),
            jax.ShapeDtypeStruct((b, MLA_HEADS, s, QK_PAD), BF16),
            jax.ShapeDtypeStruct((b, MLA_HEADS, s, V_DIM), BF16),
        ],
        compiler_params=_cparams("parallel"),
        name="mla_prep",
    )(p2, p2, misc2, pos2, freq, qnw, kvnw, wq, wqr, wkv)


def _regroup_gate_lin(w_ref, o_ref):
    k = lax.broadcasted_iota(jnp.int32, (2 * LANES, 2 * LANES), 0)
    c = lax.broadcasted_iota(jnp.int32, (2 * LANES, 2 * LANES), 1)
    src = jnp.where(c < LANES, 2 * c, 2 * (c - LANES) + 1)
    perm = jnp.where(k == src, 1.0, 0.0).astype(BF16)
    for j in range(w_ref.shape[1] // (2 * LANES)):
        cols = slice(2 * LANES * j, 2 * LANES * (j + 1))
        o_ref[:, cols] = _dot(w_ref[:, cols].astype(BF16), perm).astype(BF16)


def _attn_kernel(q_ref, k_ref, v_ref, wgu_ref, wd_ref, o_ref, wgl_ref, wdb_ref):
    for hh in range(HEADS_PER_STEP):
        s = _dot_nt(q_ref[0, hh], k_ref[0, hh])
        m = jnp.max(s, axis=-1, keepdims=True)
        p = jnp.exp2(s - m)
        l = jnp.sum(p, axis=-1, keepdims=True)
        o = _dot(p.astype(BF16), v_ref[0, hh])
        o_ref[0, :, hh * V_DIM:(hh + 1) * V_DIM] = (o / l).astype(BF16)
    _regroup_gate_lin(wgu_ref, wgl_ref)
    wdb_ref[...] = wd_ref[...].astype(BF16)


def _attention(q, k, v, w_gate_up, w_down):
    b, h, s, _ = q.shape
    nq = s // TQ
    hp = h // HEADS_PER_STEP
    steps = b * hp * nq
    e, d, f2 = w_gate_up.shape
    f = w_down.shape[1]
    assert (e * d) % steps == 0 and (e * f) % steps == 0
    rows_gu, rows_d = e * d // steps, e * f // steps
    assert rows_gu % 16 == 0 and rows_d % 16 == 0 and rows_gu <= 512 and rows_d <= 512
    step = lambda i, j, t: ((i * hp + j) * nq + t, 0)
    o, wgl, wdb = pl.pallas_call(
        _attn_kernel,
        grid=(b, hp, nq),
        in_specs=[
            pl.BlockSpec((1, HEADS_PER_STEP, TQ, QK_PAD), lambda i, j, t: (i, j, t, 0)),
            pl.BlockSpec((1, HEADS_PER_STEP, s, QK_PAD), lambda i, j, t: (i, j, 0, 0)),
            pl.BlockSpec((1, HEADS_PER_STEP, s, V_DIM), lambda i, j, t: (i, j, 0, 0)),
            pl.BlockSpec((rows_gu, f2), step),
            pl.BlockSpec((rows_d, d), step),
        ],
        out_specs=[
            pl.BlockSpec((1, TQ, HEADS_PER_STEP * V_DIM), lambda i, j, t: (i, t, j)),
            pl.BlockSpec((rows_gu, f2), step),
            pl.BlockSpec((rows_d, d), step),
        ],
        out_shape=[
            jax.ShapeDtypeStruct((b, s, h * V_DIM), BF16),
            jax.ShapeDtypeStruct((e * d, f2), BF16),
            jax.ShapeDtypeStruct((e * f, d), BF16),
        ],
        compiler_params=_cparams("parallel", "parallel", "arbitrary"),
        name="attn",
    )(q, k, v, w_gate_up.reshape(e * d, f2), w_down.reshape(e * f, d))
    return o, wgl.reshape(e, d, f2), wdb.reshape(e, f, d)


def _out_proj_kernel(x_ref, yf_ref, yb_ref, xs_ref, z_ref, o_ref, dsk_ref, gnw_ref, anw_ref,
                     wo_ref, n2w_ref, wr1_ref, wr2_ref, br_ref,
                     h_ref, xn_ref, ri_ref, rg_ref, cnt_ref, carry_ref):
    i = pl.program_id(0)

    @pl.when(i == 0)
    def _():
        carry_ref[...] = jnp.zeros_like(carry_ref)

    tm = x_ref.shape[0]
    y = yf_ref[...] + yb_ref[...] + xs_ref[...].astype(F32) * dsk_ref[...]
    y = y * _silu(z_ref[...].astype(F32))
    gs = D_SSM // SSM_GROUPS
    halves = []
    for g in range(SSM_GROUPS):
        yg = y[:, g * gs:(g + 1) * gs]
        halves.append(_rms(yg, gnw_ref[:, g * gs:(g + 1) * gs]).astype(BF16))
    y_att = _rms(o_ref[...].astype(F32), anw_ref[...]).astype(BF16)
    lhs = jnp.concatenate(halves + [y_att], axis=1)
    h1 = x_ref[...] + _dot(lhs, wo_ref[...])
    h_ref[...] = h1
    xn = _rms(h1, n2w_ref[...])
    xn_ref[...] = xn

    x1 = xn.astype(BF16)
    x2 = (xn - x1.astype(F32)).astype(BF16)
    logits = _dot(x1, wr1_ref[...]) + (_dot(x1, wr2_ref[...]) + _dot(x2, wr1_ref[...])) + br_ref[...]
    lane = lax.broadcasted_iota(jnp.int32, (tm, LANES), 1)
    neg = jnp.float32(-jnp.inf)
    work = jnp.where(lane < N_EXPERTS, logits, neg)
    vals, ids = [], []
    for _ in range(TOP_K):
        m = jnp.max(work, axis=-1, keepdims=True)
        idx = jnp.min(jnp.where(work == m, lane, LANES), axis=-1, keepdims=True)
        vals.append(m)
        ids.append(idx)
        work = jnp.where(lane == idx, neg, work)
    es = [jnp.exp(v - vals[0]) for v in vals]
    den = es[0] + es[1] + es[2] + es[3]
    sel = jnp.zeros((tm, LANES), F32)
    for idx in ids:
        sel = sel + jnp.where(lane == idx, 1.0, 0.0)
    r = lax.broadcasted_iota(jnp.int32, (tm, tm), 0)
    c = lax.broadcasted_iota(jnp.int32, (tm, tm), 1)
    strict = jnp.where(c < r, 1.0, 0.0).astype(BF16)
    cum = _dot(strict, sel.astype(BF16)) + carry_ref[0:1, :]
    ri = jnp.zeros((tm, LANES), jnp.int32)
    rg = jnp.zeros((tm, LANES), F32)
    for kk in range(TOP_K):
        rank = jnp.sum(jnp.where(lane == ids[kk], cum, 0.0), axis=-1, keepdims=True).astype(jnp.int32)
        ri = jnp.where(lane == kk, ids[kk], ri)
        ri = jnp.where(lane == TOP_K + kk, rank, ri)
        rg = jnp.where(lane == kk, es[kk] / den, rg)
    ri_ref[...] = ri
    rg_ref[...] = rg
    total = carry_ref[0:1, :] + jnp.sum(sel, axis=0, keepdims=True)
    carry_ref[...] = jnp.broadcast_to(total, carry_ref.shape)
    cnt_ref[...] = jnp.broadcast_to(total, cnt_ref.shape)


def _out_proj(x2, yf, yb, xc2, p2, o2, dsk, gnw, anw, wo, n2w, wr1, wr2, br):
    n = x2.shape[0]
    row = lambda i: (i, 0)
    fixed = lambda i: (0, 0)
    return pl.pallas_call(
        _out_proj_kernel,
        grid=(n // TM_OUT,),
        in_specs=[
            pl.BlockSpec((TM_OUT, D_MODEL), row),
            pl.BlockSpec((TM_OUT, D_SSM), row),
            pl.BlockSpec((TM_OUT, D_SSM), row),
            pl.BlockSpec((TM_OUT, D_SSM), row),
            pl.BlockSpec((TM_OUT, D_SSM), row),
            pl.BlockSpec((TM_OUT, MLA_HEADS * V_DIM), row),
            pl.BlockSpec((1, D_SSM), fixed),
            pl.BlockSpec((1, D_SSM), fixed),
            pl.BlockSpec((1, MLA_HEADS * V_DIM), fixed),
            pl.BlockSpec((D_MODEL, D_MODEL), fixed),
            pl.BlockSpec((1, D_MODEL), fixed),
            pl.BlockSpec((D_MODEL, LANES), fixed),
            pl.BlockSpec((D_MODEL, LANES), fixed),
            pl.BlockSpec((1, LANES), fixed),
        ],
        out_specs=[
            pl.BlockSpec((TM_OUT, D_MODEL), row),
            pl.BlockSpec((TM_OUT, D_MODEL), row),
            pl.BlockSpec((TM_OUT, LANES), row),
            pl.BlockSpec((TM_OUT, LANES), row),
            pl.BlockSpec((8, LANES), fixed),
        ],
        out_shape=[
            jax.ShapeDtypeStruct((n, D_MODEL), F32),
            jax.ShapeDtypeStruct((n, D_MODEL), F32),
            jax.ShapeDtypeStruct((n, LANES), jnp.int32),
            jax.ShapeDtypeStruct((n, LANES), F32),
            jax.ShapeDtypeStruct((8, LANES), F32),
        ],
        scratch_shapes=[pltpu.VMEM((8, LANES), F32)],
        compiler_params=_cparams("arbitrary"),
        name="out_proj",
    )(x2, yf, yb, xc2, p2, o2, dsk, gnw, anw, wo, n2w, wr1, wr2, br)


def _moe_kernel(be_ref, nu_ref, tok_ref, tokn_ref, pos_ref, x_hbm, wgl_ref, wd_ref, bgl_ref, bd_ref,
                y_hbm, xg_ref, xb_ref, acc_ref, yb_ref, gsem, ssem):
    del be_ref
    blk = pl.program_id(0)
    hc = pl.program_id(1)
    nblk = pl.num_programs(0)
    last = pl.num_programs(1) - 1
    nu = nu_ref[0]
    slot = lax.rem(blk, 2)

    def gather(src_row, s, j):
        return pltpu.make_async_copy(x_hbm.at[pl.ds(src_row, 1)], xg_ref.at[s, pl.ds(j, 1)], gsem.at[s])

    def scatter(j, dst_row):
        return pltpu.make_async_copy(yb_ref.at[pl.ds(j, 1)], y_hbm.at[pl.ds(dst_row, 1)], ssem)

    def start_gather(rows_ref, s):
        for j in range(MOE_BLK):
            gather(rows_ref[0, 0, j], s, j).start()

    def wait_gather(s):
        for _ in range(MOE_BLK):
            gather(0, s, 0).wait()

    def start_scatter():
        for j in range(MOE_BLK):
            scatter(j, pos_ref[0, 0, j]).start()

    def wait_scatter():
        for _ in range(MOE_BLK):
            scatter(0, 0).wait()

    @pl.when(blk < nu)
    def _():
        @pl.when(hc == 0)
        def _():
            @pl.when(blk == 0)
            def _():
                start_gather(tok_ref, 0)

            wait_gather(slot)
            xb_ref[...] = xg_ref[slot].astype(BF16)

            @pl.when(blk + 1 < nu)
            def _():
                start_gather(tokn_ref, 1 - slot)

        hdn = _dot(xb_ref[...], wgl_ref[0]) + bgl_ref[0]
        acts = []
        for j in range(MOE_TC // LANES):
            glu = jnp.minimum(hdn[:, 2 * LANES * j:2 * LANES * j + LANES], SWIGLU_LIMIT)
            lin = jnp.clip(hdn[:, 2 * LANES * j + LANES:2 * LANES * (j + 1)], -SWIGLU_LIMIT, SWIGLU_LIMIT)
            acts.append((glu / (1.0 + jnp.exp(-SWIGLU_ALPHA * glu)) * (lin + 1.0)).astype(BF16))
        part = _dot(jnp.concatenate(acts, axis=1), wd_ref[0])

        @pl.when(hc == 0)
        def _():
            acc_ref[...] = part + bd_ref[0]

        @pl.when((hc > 0) & (hc < last))
        def _():
            acc_ref[...] += part

        @pl.when(hc == last)
        def _():
            @pl.when(blk > 0)
            def _():
                wait_scatter()

            yb_ref[...] = acc_ref[...] + part
            start_scatter()

    @pl.when((blk >= nu) & (hc == last))
    def _():
        wait_scatter()
        yb_ref[...] = jnp.zeros_like(yb_ref)
        start_scatter()

    @pl.when((blk == nblk - 1) & (hc == last))
    def _():
        wait_scatter()


def _moe(block_expert, n_used, row_tok, row_pos, xn2, wgl, wd, bgl, bd):
    nb = row_tok.shape[0]
    nh = D_EXPERT // MOE_TC
    assert nh >= 2

    def bb(b, nu):
        return jnp.minimum(b, nu[0] - 1)

    def hh(b, h, nu):
        order = jnp.where(b % 2 == 0, h, nh - 1 - h)
        return jnp.where(b < nu[0], order, nh - 1 - (nu[0] - 1) % 2 * (nh - 1))

    smem_rows = lambda index_map: pl.BlockSpec((1, 1, MOE_BLK), index_map, memory_space=pltpu.SMEM)
    grid_spec = pltpu.PrefetchScalarGridSpec(
        num_scalar_prefetch=2,
        grid=(nb, nh),
        in_specs=[
            smem_rows(lambda b, h, be, nu: (b, 0, 0)),
            smem_rows(lambda b, h, be, nu: (jnp.minimum(b + 1, nb - 1), 0, 0)),
            smem_rows(lambda b, h, be, nu: (b, 0, 0)),
            pl.BlockSpec(memory_space=pl.ANY),
            pl.BlockSpec((1, D_MODEL, 2 * MOE_TC), lambda b, h, be, nu: (be[bb(b, nu)], 0, hh(b, h, nu))),
            pl.BlockSpec((1, MOE_TC, D_MODEL), lambda b, h, be, nu: (be[bb(b, nu)], hh(b, h, nu), 0)),
            pl.BlockSpec((1, 1, 2 * MOE_TC), lambda b, h, be, nu: (be[bb(b, nu)], 0, hh(b, h, nu))),
            pl.BlockSpec((1, 1, D_MODEL), lambda b, h, be, nu: (be[bb(b, nu)], 0, 0)),
        ],
        out_specs=pl.BlockSpec(memory_space=pl.ANY),
        scratch_shapes=[
            pltpu.VMEM((2, MOE_BLK, D_MODEL), F32),
            pltpu.VMEM((MOE_BLK, D_MODEL), BF16),
            pltpu.VMEM((MOE_BLK, D_MODEL), F32),
            pltpu.VMEM((MOE_BLK, D_MODEL), F32),
            pltpu.SemaphoreType.DMA((2,)),
            pltpu.SemaphoreType.DMA(()),
        ],
    )
    return pl.pallas_call(
        _moe_kernel,
        grid_spec=grid_spec,
        out_shape=jax.ShapeDtypeStruct((nb * MOE_BLK, D_MODEL), F32),
        compiler_params=_cparams("arbitrary", "arbitrary"),
        name="moe",
    )(block_expert, n_used, row_tok, row_tok, row_pos, xn2, wgl, wd, bgl, bd)


def _combine_kernel(y0_ref, y1_ref, y2_ref, y3_ref, h_ref, rg_ref, fw_ref, o_ref):
    rg = rg_ref[...]
    out = h_ref[...]
    for k, y_ref in enumerate((y0_ref, y1_ref, y2_ref, y3_ref)):
        out = out + y_ref[...] * rg[:, k:k + 1]
    o_ref[...] = _rms(out, fw_ref[...])


def _combine(y_rows, h1, rg, final_w):
    n = h1.shape[0]
    nt = n // TD
    row = lambda i: (i, 0)
    slot_spec = lambda k: pl.BlockSpec((TD, D_MODEL), lambda i: (k * nt + i, 0))
    return pl.pallas_call(
        _combine_kernel,
        grid=(nt,),
        in_specs=[slot_spec(k) for k in range(TOP_K)] + [
            pl.BlockSpec((TD, D_MODEL), row),
            pl.BlockSpec((TD, LANES), row),
            pl.BlockSpec((1, D_MODEL), lambda i: (0, 0)),
        ],
        out_specs=pl.BlockSpec((TD, D_MODEL), row),
        out_shape=jax.ShapeDtypeStruct((n, D_MODEL), F32),
        compiler_params=_cparams("parallel"),
        name="combine",
    )(y_rows, y_rows, y_rows, y_rows, h1, rg, final_w)


def _pad_cols(w, width):
    return jnp.pad(w, ((0, 0), (0, width - w.shape[1])))


def _rot_cols(w):
    half = w.shape[1] // 2
    return jnp.concatenate([-w[:, half:], w[:, :half]], axis=1)


def _layer(h, positions, norm1_w, w_in, conv_w, conv_b, dt_bias_f, dt_bias_b, a_log_f, a_log_b, d_skip,
           ssm_norm_w, q_a_norm_w, w_q_b, kv_a_norm_w, w_kv_b, attn_norm_w, w_out, norm2_w, w_router,
           b_router, w_gate_up, b_gate_up, w_down, b_down, final_norm_w):
    b, s, _ = h.shape
    n = b * s
    x2 = h.reshape(n, D_MODEL)

    o_z, o_xbc = 0, D_SSM
    o_dtf = o_xbc + CONV_DIM
    o_dtb = o_dtf + SSM_HEADS
    o_qa = o_dtb + SSM_HEADS
    o_kva = o_qa + Q_LORA
    o_kpe = o_kva + KV_LORA
    w_z = w_in[:, o_z:o_xbc]
    w_xbc = w_in[:, o_xbc:o_dtf]
    w_qa = w_in[:, o_qa:o_kva]
    w_kva = w_in[:, o_kva:o_kpe]
    w_kpe = w_in[:, o_kpe:o_kpe + QK_ROPE]
    w_dt = w_in[:, o_dtf:o_qa]
    w_main = jnp.concatenate([w_z, w_xbc, w_kva, w_qa], axis=1).astype(BF16)
    w_misc = jnp.concatenate([_pad_cols(w_kpe, LANES), _pad_cols(_rot_cols(w_kpe), LANES),
                              _pad_cols(w_dt, LANES)], axis=1).astype(BF16)

    p2, misc2 = _in_proj(x2, norm1_w.reshape(1, D_MODEL), w_main, w_misc)

    xc = _conv(p2.reshape(b, s, P_COLS), conv_w, conv_b.reshape(1, CONV_DIM))
    par = jnp.zeros((8, LANES), F32)
    par = par.at[0, :2 * SSM_HEADS].set(jnp.concatenate([dt_bias_f, dt_bias_b]))
    par = par.at[1, :2 * SSM_HEADS].set(jnp.concatenate([a_log_f, a_log_b]))
    y_f, y_b = _ssd(xc, misc2.reshape(b, s, MISC_COLS), par)

    inv_freq = ROPE_THETA ** (-jnp.arange(0, QK_ROPE, 2, dtype=F32) / QK_ROPE)
    freq = _pad_cols(jnp.concatenate([inv_freq, inv_freq])[None, :], LANES)
    wq3 = w_q_b.reshape(Q_LORA, MLA_HEADS, QK_NOPE + QK_ROPE)
    wq_rope = wq3[:, :, QK_NOPE:]
    wq = jnp.pad(wq3, ((0, 0), (0, 0), (0, QK_PAD - QK_NOPE - QK_ROPE))).reshape(Q_LORA, MLA_HEADS * QK_PAD)
    wq_rot = jnp.concatenate([-wq_rope[:, :, QK_ROPE // 2:], wq_rope[:, :, :QK_ROPE // 2]], axis=2)
    wqr = jnp.pad(wq_rot, ((0, 0), (0, 0), (0, LANES - QK_ROPE))).reshape(Q_LORA, MLA_HEADS * LANES)
    q, k, v = _mla_prep(p2, misc2, positions.reshape(n, 1), freq, q_a_norm_w.reshape(1, Q_LORA),
                        kv_a_norm_w.reshape(1, KV_LORA), wq.astype(BF16), wqr.astype(BF16),
                        w_kv_b.astype(BF16), b, s)
    o, wgl, wdb = _attention(q, k, v, w_gate_up, w_down)

    dsk = jnp.repeat(d_skip, SSM_HEAD_DIM)[None, :]
    wr = _pad_cols(w_router, LANES)
    wr1 = wr.astype(BF16)
    wr2 = (wr - wr1.astype(F32)).astype(BF16)
    br = _pad_cols(b_router[None, :], LANES)
    h1, xn2, ri, rg, cnt = _out_proj(
        x2, y_f.reshape(n, D_SSM), y_b.reshape(n, D_SSM), xc.reshape(n, CONV_DIM), p2, o.reshape(n, MLA_HEADS * V_DIM),
        dsk, ssm_norm_w.reshape(1, D_SSM), attn_norm_w.reshape(1, MLA_HEADS * V_DIM), w_out.astype(BF16),
        norm2_w.reshape(1, D_MODEL), wr1, wr2, br)

    counts = cnt[0, :N_EXPERTS].astype(jnp.int32)
    nblk = (counts + MOE_BLK - 1) // MOE_BLK
    blk_end = jnp.cumsum(nblk)
    row_start = (blk_end - nblk) * MOE_BLK
    n_blocks = (n * TOP_K) // MOE_BLK + N_EXPERTS
    n_rows = n_blocks * MOE_BLK
    ids = ri[:, :TOP_K]
    dest = (row_start[ids] + ri[:, TOP_K:2 * TOP_K]).reshape(n * TOP_K)
    block_ids = jnp.arange(n_blocks, dtype=jnp.int32)
    block_expert = jnp.minimum(jnp.sum((blk_end[None, :] <= block_ids[:, None]).astype(jnp.int32), axis=1),
                               N_EXPERTS - 1)
    n_used = blk_end[-1:].astype(jnp.int32)

    assign = (jnp.arange(TOP_K, dtype=jnp.int32)[None, :] * n + jnp.arange(n, dtype=jnp.int32)[:, None])
    inv = jnp.full((n_rows,), -1, jnp.int32).at[dest].set(assign.reshape(n * TOP_K), unique_indices=True)
    is_pad = inv < 0
    pad_pos = n * TOP_K + jnp.cumsum(is_pad.astype(jnp.int32)) - 1
    row_pos = jnp.where(is_pad, pad_pos, inv).reshape(n_blocks, 1, MOE_BLK)
    row_tok = jnp.where(is_pad, 0, inv % n).reshape(n_blocks, 1, MOE_BLK)

    bgl = b_gate_up.reshape(N_EXPERTS, D_EXPERT // LANES, LANES, 2).swapaxes(2, 3).reshape(N_EXPERTS, 1, 2 * D_EXPERT)
    y_rows = _moe(block_expert, n_used, row_tok, row_pos, xn2, wgl, wdb, bgl, b_down[:, None, :])

    out = _combine(y_rows, h1, rg, final_norm_w.reshape(1, D_MODEL))
    return out.reshape(b, s, D_MODEL)


def kernel(x, positions, norm1_w, w_in, conv_w, conv_b, dt_bias_f, dt_bias_b, a_log_f, a_log_b, d_skip,
           ssm_norm_w, q_a_norm_w, w_q_b, kv_a_norm_w, w_kv_b, attn_norm_w, w_out, norm2_w, w_router, b_router,
           w_gate_up, b_gate_up, w_down, b_down, final_norm_w):
    depth = norm1_w.shape[0]
    assert depth == 1, "the final norm is fused into the single layer's combine step"
    return _layer(x, positions, norm1_w[0], w_in[0], conv_w[0], conv_b[0], dt_bias_f[0], dt_bias_b[0],
                  a_log_f[0], a_log_b[0], d_skip[0], ssm_norm_w[0], q_a_norm_w[0], w_q_b[0], kv_a_norm_w[0],
                  w_kv_b[0], attn_norm_w[0], w_out[0], norm2_w[0], w_router[0], b_router[0], w_gate_up[0],
                  b_gate_up[0], w_down[0], b_down[0], final_norm_w)
```

```python
import math

import jax
import jax.numpy as jnp
from jax import lax
from jax.experimental import pallas as pl
from jax.experimental.pallas import tpu as pltpu

F32 = jnp.float32
BF16 = jnp.bfloat16

D_MODEL = 2048
D_SSM = 1024
SSM_HEAD_DIM = 64
SSM_HEADS = 16
SSM_GROUPS = 2
D_STATE = 128
D_CONV = 5
CONV_DIM = D_SSM + 2 * SSM_GROUPS * D_STATE
CHUNK = 128
MLA_HEADS = 8
QK_NOPE = 128
QK_ROPE = 64
V_DIM = 128
Q_LORA = 768
KV_LORA = 512
ROPE_THETA = 10000.0
N_EXPERTS = 32
TOP_K = 4
D_EXPERT = 2048
SWIGLU_ALPHA = 1.702
SWIGLU_LIMIT = 7.0
RMS_EPS = 1e-6
LOG2_E = 1.0 / math.log(2.0)

LANES = 128
QK_PAD = 256
P_COLS = D_SSM + CONV_DIM + KV_LORA + Q_LORA
MISC_COLS = 3 * LANES
VMEM_LIMIT = 56 * 1024 * 1024

TM_IN = 512
TM_PREP = 512
TQ = 256
HEADS_PER_STEP = 2
SSD_CHUNKS_PER_STEP = 4
TM_OUT = 256
MOE_BLK = 512
MOE_TC = 1024
TD = 256


def _cparams(*sem):
    return pltpu.CompilerParams(dimension_semantics=sem, vmem_limit_bytes=VMEM_LIMIT)


def _silu(x):
    return x / (1.0 + jnp.exp(-x))


def _softplus(x):
    return jnp.maximum(x, 0.0) + jnp.log(1.0 + jnp.exp(-jnp.abs(x)))


def _split3(x):
    x1 = x.astype(BF16)
    r = x - x1.astype(F32)
    x2 = r.astype(BF16)
    x3 = (r - x2.astype(F32)).astype(BF16)
    return x1, x2, x3


def _dot(a, b):
    return jnp.dot(a, b, preferred_element_type=F32)


def _dot_nt(a, b):
    return lax.dot_general(a, b, (((1,), (1,)), ((), ())), preferred_element_type=F32)


def _dot_exact_lhs(x, m, passes):
    parts = _split3(x)[:passes]
    acc = _dot(parts[0], m)
    for p in parts[1:]:
        acc = acc + _dot(p, m)
    return acc


def _in_proj_kernel(x_ref, nw_ref, w_ref, wm_ref, p_ref, m_ref):
    x = x_ref[...]
    ms = jnp.mean(x * x, axis=-1, keepdims=True)
    xn = (x * lax.rsqrt(ms + RMS_EPS) * nw_ref[...]).astype(BF16)
    m_ref[...] = _dot(xn, wm_ref[...])
    p_ref[...] = _dot(xn, w_ref[...]).astype(BF16)


def _resident(shape):
    return pl.BlockSpec(shape, lambda *_: (0, 0), pipeline_mode=pl.Buffered(1))


def _in_proj(x2, norm_w, w_main, w_misc):
    n = x2.shape[0]
    return pl.pallas_call(
        _in_proj_kernel,
        grid=(n // TM_IN,),
        in_specs=[
            pl.BlockSpec((TM_IN, D_MODEL), lambda i: (i, 0)),
            _resident((1, D_MODEL)),
            _resident((D_MODEL, P_COLS)),
            _resident((D_MODEL, MISC_COLS)),
        ],
        out_specs=[
            pl.BlockSpec((TM_IN, P_COLS), lambda i: (i, 0)),
            pl.BlockSpec((TM_IN, MISC_COLS), lambda i: (i, 0)),
        ],
        out_shape=[
            jax.ShapeDtypeStruct((n, P_COLS), BF16),
            jax.ShapeDtypeStruct((n, MISC_COLS), F32),
        ],
        compiler_params=_cparams("parallel"),
        name="in_proj",
    )(x2, norm_w, w_main, w_misc)


def _conv_kernel(u_ref, w_ref, b_ref, o_ref):
    u = u_ref[0].astype(F32)
    s = u.shape[0]
    row = lax.broadcasted_iota(jnp.int32, u.shape, 0)
    pad = D_CONV // 2
    acc = u * w_ref[pad:pad + 1, :] + b_ref[...]
    for k in range(D_CONV):
        off = k - pad
        if off == 0:
            continue
        shifted = pltpu.roll(u, (-off) % s, 0)
        valid = (row + off >= 0) & (row + off < s)
        acc = acc + jnp.where(valid, shifted, 0.0) * w_ref[k:k + 1, :]
    o_ref[0] = _silu(acc).astype(BF16)


def _conv(p3, conv_w, conv_b):
    b, s, _ = p3.shape
    first = D_SSM // LANES
    return pl.pallas_call(
        _conv_kernel,
        grid=(b, CONV_DIM // LANES),
        in_specs=[
            pl.BlockSpec((1, s, LANES), lambda i, c: (i, 0, first + c)),
            pl.BlockSpec((D_CONV, LANES), lambda i, c: (0, c)),
            pl.BlockSpec((1, LANES), lambda i, c: (0, c)),
        ],
        out_specs=pl.BlockSpec((1, s, LANES), lambda i, c: (i, 0, c)),
        out_shape=jax.ShapeDtypeStruct((b, s, CONV_DIM), BF16),
        compiler_params=_cparams("parallel", "parallel"),
        name="conv",
    )(p3, conv_w, conv_b)


def _ssd_direction(xc, dt_e, cs, cs_e, tot_row, state_ref, d, reverse):
    L = xc.shape[0]
    gw = D_SSM // SSM_GROUPS
    x = xc[:, :D_SSM].astype(F32)
    xdt = x * dt_e
    xdt_b = xdt.astype(BF16)
    tot_e = cs_e[tot_row:tot_row + 1, :]
    decay_out = jnp.exp(cs_e)
    xw = (xdt * jnp.exp(tot_e - cs_e)).astype(BF16)
    chunk_decay = jnp.exp(tot_e)

    cs_t = cs.T
    li = lax.broadcasted_iota(jnp.int32, (L, L), 0)
    si = lax.broadcasted_iota(jnp.int32, (L, L), 1)
    keep = (si >= li) if reverse else (si <= li)
    lane = lax.broadcasted_iota(jnp.int32, (L, LANES), 1)

    ys = []
    for g in range(SSM_GROUPS):
        bm = xc[:, D_SSM + g * D_STATE:D_SSM + (g + 1) * D_STATE]
        cm = xc[:, D_SSM + SSM_GROUPS * D_STATE + g * D_STATE:D_SSM + SSM_GROUPS * D_STATE + (g + 1) * D_STATE]
        cb = _dot_nt(cm, bm)
        st = state_ref[d, g]
        y_off = _dot(cm, st.astype(BF16)) * decay_out[:, g * gw:(g + 1) * gw]
        bm_t = bm.astype(F32).T.astype(BF16)
        state_ref[d, g] = st * chunk_decay[:, g * gw:(g + 1) * gw] + _dot(bm_t, xw[:, g * gw:(g + 1) * gw])

        pieces = []
        heads_per_group = SSM_HEADS // SSM_GROUPS
        for j in range(heads_per_group // 2):
            scores = []
            for hh in range(2):
                h = g * heads_per_group + 2 * j + hh
                hl = SSM_HEADS * d + h
                col = jnp.broadcast_to(cs[:, hl:hl + 1], (L, L))
                rowv = jnp.broadcast_to(cs_t[hl:hl + 1, :], (L, L))
                dec = jnp.exp(jnp.where(keep, col - rowv, -1e30))
                scores.append((cb * dec).astype(BF16))
            lhs = jnp.concatenate(scores, axis=1)
            c0 = g * gw + j * LANES
            xp = xdt_b[:, c0:c0 + LANES]
            zero = jnp.zeros_like(xp)
            rhs = jnp.concatenate([jnp.where(lane < SSM_HEAD_DIM, xp, zero),
                                   jnp.where(lane >= SSM_HEAD_DIM, xp, zero)], axis=0)
            pieces.append(_dot(lhs, rhs))
        ys.append(jnp.concatenate(pieces, axis=1) + y_off)
    return jnp.concatenate(ys, axis=1)


def _ssd_kernel(xf_ref, xb_ref, mf_ref, mb_ref, par_ref, yf_ref, yb_ref, state_ref):
    @pl.when(pl.program_id(1) == 0)
    def _():
        state_ref[...] = jnp.zeros_like(state_ref)

    L = CHUNK
    par = par_ref[...]
    lane = lax.broadcasted_iota(jnp.int32, (1, LANES), 1)
    active = lane < 2 * SSM_HEADS
    a_coef = jnp.where(active, -jnp.exp(par[1:2, :]), 0.0)
    r = lax.broadcasted_iota(jnp.int32, (L, L), 0)
    c = lax.broadcasted_iota(jnp.int32, (L, L), 1)
    tri_lo = jnp.where(c <= r, 1.0, 0.0).astype(BF16)
    tri_up = jnp.where(c >= r, 1.0, 0.0).astype(BF16)
    er = lax.broadcasted_iota(jnp.int32, (LANES, D_SSM), 0)
    ec = lax.broadcasted_iota(jnp.int32, (LANES, D_SSM), 1)

    for d, (x_ref, m_ref, y_ref) in enumerate(((xf_ref, mf_ref, yf_ref), (xb_ref, mb_ref, yb_ref))):
        reverse = d == 1
        expand = jnp.where(er == ec // SSM_HEAD_DIM + SSM_HEADS * d, 1.0, 0.0).astype(BF16)
        tri = tri_up if reverse else tri_lo
        order = range(SSD_CHUNKS_PER_STEP - 1, -1, -1) if reverse else range(SSD_CHUNKS_PER_STEP)
        for ci in order:
            rows = slice(ci * L, (ci + 1) * L)
            dt = jnp.where(active, _softplus(m_ref[0, rows, :] + par[0:1, :]), 0.0)
            a = dt * a_coef
            a1, a2, a3 = _split3(a)
            cs = _dot(tri, a1) + _dot(tri, a2) + _dot(tri, a3)
            dt_e = _dot_exact_lhs(dt, expand, 2)
            cs_e = _dot_exact_lhs(cs, expand, 3)
            y = _ssd_direction(x_ref[0, rows, :], dt_e, cs, cs_e, 0 if reverse else L - 1, state_ref, d, reverse)
            y_ref[0, rows, :] = y


def _ssd(xc, misc3, par):
    b, s, _ = xc.shape
    t = CHUNK * SSD_CHUNKS_PER_STEP
    nc = s // t
    dtb = MISC_COLS // LANES - 1
    gw = D_SSM // SSM_GROUPS
    return pl.pallas_call(
        _ssd_kernel,
        grid=(b, nc),
        in_specs=[
            pl.BlockSpec((1, t, CONV_DIM), lambda i, c: (i, c, 0)),
            pl.BlockSpec((1, t, CONV_DIM), lambda i, c: (i, nc - 1 - c, 0)),
            pl.BlockSpec((1, t, LANES), lambda i, c: (i, c, dtb)),
            pl.BlockSpec((1, t, LANES), lambda i, c: (i, nc - 1 - c, dtb)),
            pl.BlockSpec((8, LANES), lambda i, c: (0, 0)),
        ],
        out_specs=[
            pl.BlockSpec((1, t, D_SSM), lambda i, c: (i, c, 0)),
            pl.BlockSpec((1, t, D_SSM), lambda i, c: (i, nc - 1 - c, 0)),
        ],
        out_shape=[jax.ShapeDtypeStruct((b, s, D_SSM), F32)] * 2,
        scratch_shapes=[pltpu.VMEM((2, SSM_GROUPS, D_STATE, gw), F32)],
        compiler_params=_cparams("parallel", "arbitrary"),
        name="ssd",
    )(xc, xc, misc3, misc3, par)


def _rms(x, w):
    ms = jnp.mean(x * x, axis=-1, keepdims=True)
    return x * lax.rsqrt(ms + RMS_EPS) * w


def _mla_prep_kernel(qa_ref, kva_ref, misc_ref, pos_ref, freq_ref, qnw_ref, kvnw_ref,
                     wq_ref, wqr_ref, wkv_ref, q_ref, k_ref, v_ref):
    scale = (QK_NOPE + QK_ROPE) ** -0.5 * LOG2_E
    qn = _rms(qa_ref[...].astype(F32), qnw_ref[...]).astype(BF16)
    kvn = _rms(kva_ref[...].astype(F32), kvnw_ref[...]).astype(BF16)
    ang = pos_ref[...].astype(F32) * freq_ref[...]
    cos = jnp.cos(ang)
    sin = jnp.sin(ang)
    qm = _dot(qn, wq_ref[...])
    qr = _dot(qn, wqr_ref[...])
    kv = _dot(kvn, wkv_ref[...])
    misc = misc_ref[...]
    k_rope = (misc[:, :LANES] * cos + misc[:, LANES:2 * LANES] * sin).astype(BF16)
    for h in range(MLA_HEADS):
        q_ref[0, h, :, :LANES] = (qm[:, h * QK_PAD:h * QK_PAD + LANES] * scale).astype(BF16)
        q_rope = qm[:, h * QK_PAD + LANES:(h + 1) * QK_PAD] * cos + qr[:, h * LANES:(h + 1) * LANES] * sin
        q_ref[0, h, :, LANES:] = (q_rope * scale).astype(BF16)
        k_ref[0, h, :, :LANES] = kv[:, h * 2 * LANES:h * 2 * LANES + LANES].astype(BF16)
        k_ref[0, h, :, LANES:] = k_rope
        v_ref[0, h] = kv[:, h * 2 * LANES + LANES:(h + 1) * 2 * LANES].astype(BF16)


def _mla_prep(p2, misc2, pos2, freq, qnw, kvnw, wq, wqr, wkv, b, s):
    n = p2.shape[0]
    spb = s // TM_PREP
    kv_blk = (D_SSM + CONV_DIM) // KV_LORA
    qa_blk = (D_SSM + CONV_DIM + KV_LORA) // Q_LORA
    out_idx = lambda i: (i // spb, 0, i % spb, 0)
    return pl.pallas_call(
        _mla_prep_kernel,
        grid=(n // TM_PREP,),
        in_specs=[
            pl.BlockSpec((TM_PREP, Q_LORA), lambda i: (i, qa_blk)),
            pl.BlockSpec((TM_PREP, KV_LORA), lambda i: (i, kv_blk)),
            pl.BlockSpec((TM_PREP, 2 * LANES), lambda i: (i, 0)),
            pl.BlockSpec((TM_PREP, 1), lambda i: (i, 0)),
            pl.BlockSpec((1, LANES), lambda i: (0, 0)),
            pl.BlockSpec((1, Q_LORA), lambda i: (0, 0)),
            pl.BlockSpec((1, KV_LORA), lambda i: (0, 0)),
            pl.BlockSpec((Q_LORA, MLA_HEADS * QK_PAD), lambda i: (0, 0)),
            pl.BlockSpec((Q_LORA, MLA_HEADS * LANES), lambda i: (0, 0)),
            pl.BlockSpec((KV_LORA, MLA_HEADS * 2 * LANES), lambda i: (0, 0)),
        ],
        out_specs=[
            pl.BlockSpec((1, MLA_HEADS, TM_PREP, QK_PAD), out_idx),
            pl.BlockSpec((1, MLA_HEADS, TM_PREP, QK_PAD), out_idx),
            pl.BlockSpec((1, MLA_HEADS, TM_PREP, V_DIM), out_idx),
        ],
        out_shape=[
            jax.ShapeDtypeStruct((b, MLA_HEADS, s, QK_PAD), BF16),
            jax.ShapeDtypeStruct((b, MLA_HEADS, s, QK_PAD), BF16),
            jax.ShapeDtypeStruct((b, MLA_HEADS, s, V_DIM), BF16),
        ],
        compiler_params=_cparams("parallel"),
        name="mla_prep",
    )(p2, p2, misc2, pos2, freq, qnw, kvnw, wq, wqr, wkv)


def _regroup_gate_lin(w_ref, o_ref):
    k = lax.broadcasted_iota(jnp.int32, (2 * LANES, 2 * LANES), 0)
    c = lax.broadcasted_iota(jnp.int32, (2 * LANES, 2 * LANES), 1)
    src = jnp.where(c < LANES, 2 * c, 2 * (c - LANES) + 1)
    perm = jnp.where(k == src, 1.0, 0.0).astype(BF16)
    for j in range(w_ref.shape[1] // (2 * LANES)):
        cols = slice(2 * LANES * j, 2 * LANES * (j + 1))
        o_ref[:, cols] = _dot(w_ref[:, cols].astype(BF16), perm).astype(BF16)


def _attn_kernel(q_ref, k_ref, v_ref, wgu_ref, wd_ref, o_ref, wgl_ref, wdb_ref):
    for hh in range(HEADS_PER_STEP):
        s = _dot_nt(q_ref[0, hh], k_ref[0, hh])
        m = jnp.max(s, axis=-1, keepdims=True)
        p = jnp.exp2(s - m)
        l = jnp.sum(p, axis=-1, keepdims=True)
        o = _dot(p.astype(BF16), v_ref[0, hh])
        o_ref[0, :, hh * V_DIM:(hh + 1) * V_DIM] = (o / l).astype(BF16)
    _regroup_gate_lin(wgu_ref, wgl_ref)
    wdb_ref[...] = wd_ref[...].astype(BF16)


def _attention(q, k, v, w_gate_up, w_down):
    b, h, s, _ = q.shape
    nq = s // TQ
    hp = h // HEADS_PER_STEP
    steps = b * hp * nq
    e, d, f2 = w_gate_up.shape
    f = w_down.shape[1]
    assert (e * d) % steps == 0 and (e * f) % steps == 0
    rows_gu, rows_d = e * d // steps, e * f // steps
    assert rows_gu % 16 == 0 and rows_d % 16 == 0 and rows_gu <= 512 and rows_d <= 512
    step = lambda i, j, t: ((i * hp + j) * nq + t, 0)
    o, wgl, wdb = pl.pallas_call(
        _attn_kernel,
        grid=(b, hp, nq),
        in_specs=[
            pl.BlockSpec((1, HEADS_PER_STEP, TQ, QK_PAD), lambda i, j, t: (i, j, t, 0)),
            pl.BlockSpec((1, HEADS_PER_STEP, s, QK_PAD), lambda i, j, t: (i, j, 0, 0)),
            pl.BlockSpec((1, HEADS_PER_STEP, s, V_DIM), lambda i, j, t: (i, j, 0, 0)),
            pl.BlockSpec((rows_gu, f2), step),
            pl.BlockSpec((rows_d, d), step),
        ],
        out_specs=[
            pl.BlockSpec((1, TQ, HEADS_PER_STEP * V_DIM), lambda i, j, t: (i, t, j)),
            pl.BlockSpec((rows_gu, f2), step),
            pl.BlockSpec((rows_d, d), step),
        ],
        out_shape=[
            jax.ShapeDtypeStruct((b, s, h * V_DIM), BF16),
            jax.ShapeDtypeStruct((e * d, f2), BF16),
            jax.ShapeDtypeStruct((e * f, d), BF16),
        ],
        compiler_params=_cparams("parallel", "parallel", "arbitrary"),
        name="attn",
    )(q, k, v, w_gate_up.reshape(e * d, f2), w_down.reshape(e * f, d))
    return o, wgl.reshape(e, d, f2), wdb.reshape(e, f, d)


def _out_proj_kernel(x_ref, yf_ref, yb_ref, xs_ref, z_ref, o_ref, dsk_ref, gnw_ref, anw_ref,
                     wo_ref, n2w_ref, wr1_ref, wr2_ref, br_ref,
                     h_ref, xn_ref, ri_ref, rg_ref, cnt_ref, carry_ref):
    i = pl.program_id(0)

    @pl.when(i == 0)
    def _():
        carry_ref[...] = jnp.zeros_like(carry_ref)

    tm = x_ref.shape[0]
    y = yf_ref[...] + yb_ref[...] + xs_ref[...].astype(F32) * dsk_ref[...]
    y = y * _silu(z_ref[...].astype(F32))
    gs = D_SSM // SSM_GROUPS
    halves = []
    for g in range(SSM_GROUPS):
        yg = y[:, g * gs:(g + 1) * gs]
        halves.append(_rms(yg, gnw_ref[:, g * gs:(g + 1) * gs]).astype(BF16))
    y_att = _rms(o_ref[...].astype(F32), anw_ref[...]).astype(BF16)
    lhs = jnp.concatenate(halves + [y_att], axis=1)
    h1 = x_ref[...] + _dot(lhs, wo_ref[...])
    h_ref[...] = h1
    xn = _rms(h1, n2w_ref[...])
    xn_ref[...] = xn

    x1 = xn.astype(BF16)
    x2 = (xn - x1.astype(F32)).astype(BF16)
    logits = _dot(x1, wr1_ref[...]) + (_dot(x1, wr2_ref[...]) + _dot(x2, wr1_ref[...])) + br_ref[...]
    lane = lax.broadcasted_iota(jnp.int32, (tm, LANES), 1)
    neg = jnp.float32(-jnp.inf)
    work = jnp.where(lane < N_EXPERTS, logits, neg)
    vals, ids = [], []
    for _ in range(TOP_K):
        m = jnp.max(work, axis=-1, keepdims=True)
        idx = jnp.min(jnp.where(work == m, lane, LANES), axis=-1, keepdims=True)
        vals.append(m)
        ids.append(idx)
        work = jnp.where(lane == idx, neg, work)
    es = [jnp.exp(v - vals[0]) for v in vals]
    den = es[0] + es[1] + es[2] + es[3]
    sel = jnp.zeros((tm, LANES), F32)
    for idx in ids:
        sel = sel + jnp.where(lane == idx, 1.0, 0.0)
    r = lax.broadcasted_iota(jnp.int32, (tm, tm), 0)
    c = lax.broadcasted_iota(jnp.int32, (tm, tm), 1)
    strict = jnp.where(c < r, 1.0, 0.0).astype(BF16)
    cum = _dot(strict, sel.astype(BF16)) + carry_ref[0:1, :]
    ri = jnp.zeros((tm, LANES), jnp.int32)
    rg = jnp.zeros((tm, LANES), F32)
    for kk in range(TOP_K):
        rank = jnp.sum(jnp.where(lane == ids[kk], cum, 0.0), axis=-1, keepdims=True).astype(jnp.int32)
        ri = jnp.where(lane == kk, ids[kk], ri)
        ri = jnp.where(lane == TOP_K + kk, rank, ri)
        rg = jnp.where(lane == kk, es[kk] / den, rg)
    ri_ref[...] = ri
    rg_ref[...] = rg
    total = carry_ref[0:1, :] + jnp.sum(sel, axis=0, keepdims=True)
    carry_ref[...] = jnp.broadcast_to(total, carry_ref.shape)
    cnt_ref[...] = jnp.broadcast_to(total, cnt_ref.shape)


def _out_proj(x2, yf, yb, xc2, p2, o2, dsk, gnw, anw, wo, n2w, wr1, wr2, br):
    n = x2.shape[0]
    row = lambda i: (i, 0)
    fixed = lambda i: (0, 0)
    return pl.pallas_call(
        _out_proj_kernel,
        grid=(n // TM_OUT,),
        in_specs=[
            pl.BlockSpec((TM_OUT, D_MODEL), row),
            pl.BlockSpec((TM_OUT, D_SSM), row),
            pl.BlockSpec((TM_OUT, D_SSM), row),
            pl.BlockSpec((TM_OUT, D_SSM), row),
            pl.BlockSpec((TM_OUT, D_SSM), row),
            pl.BlockSpec((TM_OUT, MLA_HEADS * V_DIM), row),
            pl.BlockSpec((1, D_SSM), fixed),
            pl.BlockSpec((1, D_SSM), fixed),
            pl.BlockSpec((1, MLA_HEADS * V_DIM), fixed),
            pl.BlockSpec((D_MODEL, D_MODEL), fixed),
            pl.BlockSpec((1, D_MODEL), fixed),
            pl.BlockSpec((D_MODEL, LANES), fixed),
            pl.BlockSpec((D_MODEL, LANES), fixed),
            pl.BlockSpec((1, LANES), fixed),
        ],
        out_specs=[
            pl.BlockSpec((TM_OUT, D_MODEL), row),
            pl.BlockSpec((TM_OUT, D_MODEL), row),
            pl.BlockSpec((TM_OUT, LANES), row),
            pl.BlockSpec((TM_OUT, LANES), row),
            pl.BlockSpec((8, LANES), fixed),
        ],
        out_shape=[
            jax.ShapeDtypeStruct((n, D_MODEL), F32),
            jax.ShapeDtypeStruct((n, D_MODEL), F32),
            jax.ShapeDtypeStruct((n, LANES), jnp.int32),
            jax.ShapeDtypeStruct((n, LANES), F32),
            jax.ShapeDtypeStruct((8, LANES), F32),
        ],
        scratch_shapes=[pltpu.VMEM((8, LANES), F32)],
        compiler_params=_cparams("arbitrary"),
        name="out_proj",
    )(x2, yf, yb, xc2, p2, o2, dsk, gnw, anw, wo, n2w, wr1, wr2, br)


def _moe_kernel(be_ref, nu_ref, tok_ref, tokn_ref, pos_ref, x_hbm, wgl_ref, wd_ref, bgl_ref, bd_ref,
                y_hbm, xg_ref, xb_ref, acc_ref, yb_ref, gsem, ssem):
    del be_ref
    blk = pl.program_id(0)
    hc = pl.program_id(1)
    nblk = pl.num_programs(0)
    last = pl.num_programs(1) - 1
    nu = nu_ref[0]
    slot = lax.rem(blk, 2)

    def gather(src_row, s, j):
        return pltpu.make_async_copy(x_hbm.at[pl.ds(src_row, 1)], xg_ref.at[s, pl.ds(j, 1)], gsem.at[s])

    def scatter(j, dst_row):
        return pltpu.make_async_copy(yb_ref.at[pl.ds(j, 1)], y_hbm.at[pl.ds(dst_row, 1)], ssem)

    def start_gather(rows_ref, s):
        for j in range(MOE_BLK):
            gather(rows_ref[0, 0, j], s, j).start()

    def wait_gather(s):
        for _ in range(MOE_BLK):
            gather(0, s, 0).wait()

    def start_scatter():
        for j in range(MOE_BLK):
            scatter(j, pos_ref[0, 0, j]).start()

    def wait_scatter():
        for _ in range(MOE_BLK):
            scatter(0, 0).wait()

    @pl.when(blk < nu)
    def _():
        @pl.when(hc == 0)
        def _():
            @pl.when(blk == 0)
            def _():
                start_gather(tok_ref, 0)

            wait_gather(slot)
            xb_ref[...] = xg_ref[slot].astype(BF16)

            @pl.when(blk + 1 < nu)
            def _():
                start_gather(tokn_ref, 1 - slot)

        hdn = _dot(xb_ref[...], wgl_ref[0]) + bgl_ref[0]
        acts = []
        for j in range(MOE_TC // LANES):
            glu = jnp.minimum(hdn[:, 2 * LANES * j:2 * LANES * j + LANES], SWIGLU_LIMIT)
            lin = jnp.clip(hdn[:, 2 * LANES * j + LANES:2 * LANES * (j + 1)], -SWIGLU_LIMIT, SWIGLU_LIMIT)
            acts.append((glu / (1.0 + jnp.exp(-SWIGLU_ALPHA * glu)) * (lin + 1.0)).astype(BF16))
        part = _dot(jnp.concatenate(acts, axis=1), wd_ref[0])

        @pl.when(hc == 0)
        def _():
            acc_ref[...] = part + bd_ref[0]

        @pl.when((hc > 0) & (hc < last))
        def _():
            acc_ref[...] += part

        @pl.when(hc == last)
        def _():
            @pl.when(blk > 0)
            def _():
                wait_scatter()

            yb_ref[...] = acc_ref[...] + part
            start_scatter()

    @pl.when((blk >= nu) & (hc == last))
    def _():
        wait_scatter()
        yb_ref[...] = jnp.zeros_like(yb_ref)
        start_scatter()

    @pl.when((blk == nblk - 1) & (hc == last))
    def _():
        wait_scatter()


def _moe(block_expert, n_used, row_tok, row_pos, xn2, wgl, wd, bgl, bd):
    nb = row_tok.shape[0]
    nh = D_EXPERT // MOE_TC
    assert nh >= 2

    def bb(b, nu):
        return jnp.minimum(b, nu[0] - 1)

    def hh(b, h, nu):
        order = jnp.where(b % 2 == 0, h, nh - 1 - h)
        return jnp.where(b < nu[0], order, nh - 1 - (nu[0] - 1) % 2 * (nh - 1))

    smem_rows = lambda index_map: pl.BlockSpec((1, 1, MOE_BLK), index_map, memory_space=pltpu.SMEM)
    grid_spec = pltpu.PrefetchScalarGridSpec(
        num_scalar_prefetch=2,
        grid=(nb, nh),
        in_specs=[
            smem_rows(lambda b, h, be, nu: (b, 0, 0)),
            smem_rows(lambda b, h, be, nu: (jnp.minimum(b + 1, nb - 1), 0, 0)),
            smem_rows(lambda b, h, be, nu: (b, 0, 0)),
            pl.BlockSpec(memory_space=pl.ANY),
            pl.BlockSpec((1, D_MODEL, 2 * MOE_TC), lambda b, h, be, nu: (be[bb(b, nu)], 0, hh(b, h, nu))),
            pl.BlockSpec((1, MOE_TC, D_MODEL), lambda b, h, be, nu: (be[bb(b, nu)], hh(b, h, nu), 0)),
            pl.BlockSpec((1, 1, 2 * MOE_TC), lambda b, h, be, nu: (be[bb(b, nu)], 0, hh(b, h, nu))),
            pl.BlockSpec((1, 1, D_MODEL), lambda b, h, be, nu: (be[bb(b, nu)], 0, 0)),
        ],
        out_specs=pl.BlockSpec(memory_space=pl.ANY),
        scratch_shapes=[
            pltpu.VMEM((2, MOE_BLK, D_MODEL), F32),
            pltpu.VMEM((MOE_BLK, D_MODEL), BF16),
            pltpu.VMEM((MOE_BLK, D_MODEL), F32),
            pltpu.VMEM((MOE_BLK, D_MODEL), F32),
            pltpu.SemaphoreType.DMA((2,)),
            pltpu.SemaphoreType.DMA(()),
        ],
    )
    return pl.pallas_call(
        _moe_kernel,
        grid_spec=grid_spec,
        out_shape=jax.ShapeDtypeStruct((nb * MOE_BLK, D_MODEL), F32),
        compiler_params=_cparams("arbitrary", "arbitrary"),
        name="moe",
    )(block_expert, n_used, row_tok, row_tok, row_pos, xn2, wgl, wd, bgl, bd)


def _combine_kernel(y0_ref, y1_ref, y2_ref, y3_ref, h_ref, rg_ref, fw_ref, o_ref):
    rg = rg_ref[...]
    out = h_ref[...]
    for k, y_ref in enumerate((y0_ref, y1_ref, y2_ref, y3_ref)):
        out = out + y_ref[...] * rg[:, k:k + 1]
    o_ref[...] = _rms(out, fw_ref[...])


def _combine(y_rows, h1, rg, final_w):
    n = h1.shape[0]
    nt = n // TD
    row = lambda i: (i, 0)
    slot_spec = lambda k: pl.BlockSpec((TD, D_MODEL), lambda i: (k * nt + i, 0))
    return pl.pallas_call(
        _combine_kernel,
        grid=(nt,),
        in_specs=[slot_spec(k) for k in range(TOP_K)] + [
            pl.BlockSpec((TD, D_MODEL), row),
            pl.BlockSpec((TD, LANES), row),
            pl.BlockSpec((1, D_MODEL), lambda i: (0, 0)),
        ],
        out_specs=pl.BlockSpec((TD, D_MODEL), row),
        out_shape=jax.ShapeDtypeStruct((n, D_MODEL), F32),
        compiler_params=_cparams("parallel"),
        name="combine",
    )(y_rows, y_rows, y_rows, y_rows, h1, rg, final_w)


def _pad_cols(w, width):
    return jnp.pad(w, ((0, 0), (0, width - w.shape[1])))


def _rot_cols(w):
    half = w.shape[1] // 2
    return jnp.concatenate([-w[:, half:], w[:, :half]], axis=1)


def _layer(h, positions, norm1_w, w_in, conv_w, conv_b, dt_bias_f, dt_bias_b, a_log_f, a_log_b, d_skip,
           ssm_norm_w, q_a_norm_w, w_q_b, kv_a_norm_w, w_kv_b, attn_norm_w, w_out, norm2_w, w_router,
           b_router, w_gate_up, b_gate_up, w_down, b_down, final_norm_w):
    b, s, _ = h.shape
    n = b * s
    x2 = h.reshape(n, D_MODEL)

    o_z, o_xbc = 0, D_SSM
    o_dtf = o_xbc + CONV_DIM
    o_dtb = o_dtf + SSM_HEADS
    o_qa = o_dtb + SSM_HEADS
    o_kva = o_qa + Q_LORA
    o_kpe = o_kva + KV_LORA
    w_z = w_in[:, o_z:o_xbc]
    w_xbc = w_in[:, o_xbc:o_dtf]
    w_qa = w_in[:, o_qa:o_kva]
    w_kva = w_in[:, o_kva:o_kpe]
    w_kpe = w_in[:, o_kpe:o_kpe + QK_ROPE]
    w_dt = w_in[:, o_dtf:o_qa]
    w_main = jnp.concatenate([w_z, w_xbc, w_kva, w_qa], axis=1).astype(BF16)
    w_misc = jnp.concatenate([_pad_cols(w_kpe, LANES), _pad_cols(_rot_cols(w_kpe), LANES),
                              _pad_cols(w_dt, LANES)], axis=1).astype(BF16)

    p2, misc2 = _in_proj(x2, norm1_w.reshape(1, D_MODEL), w_main, w_misc)

    xc = _conv(p2.reshape(b, s, P_COLS), conv_w, conv_b.reshape(1, CONV_DIM))
    par = jnp.zeros((8, LANES), F32)
    par = par.at[0, :2 * SSM_HEADS].set(jnp.concatenate([dt_bias_f, dt_bias_b]))
    par = par.at[1, :2 * SSM_HEADS].set(jnp.concatenate([a_log_f, a_log_b]))
    y_f, y_b = _ssd(xc, misc2.reshape(b, s, MISC_COLS), par)

    inv_freq = ROPE_THETA ** (-jnp.arange(0, QK_ROPE, 2, dtype=F32) / QK_ROPE)
    freq = _pad_cols(jnp.concatenate([inv_freq, inv_freq])[None, :], LANES)
    wq3 = w_q_b.reshape(Q_LORA, MLA_HEADS, QK_NOPE + QK_ROPE)
    wq_rope = wq3[:, :, QK_NOPE:]
    wq = jnp.pad(wq3, ((0, 0), (0, 0), (0, QK_PAD - QK_NOPE - QK_ROPE))).reshape(Q_LORA, MLA_HEADS * QK_PAD)
    wq_rot = jnp.concatenate([-wq_rope[:, :, QK_ROPE // 2:], wq_rope[:, :, :QK_ROPE // 2]], axis=2)
    wqr = jnp.pad(wq_rot, ((0, 0), (0, 0), (0, LANES - QK_ROPE))).reshape(Q_LORA, MLA_HEADS * LANES)
    q, k, v = _mla_prep(p2, misc2, positions.reshape(n, 1), freq, q_a_norm_w.reshape(1, Q_LORA),
                        kv_a_norm_w.reshape(1, KV_LORA), wq.astype(BF16), wqr.astype(BF16),
                        w_kv_b.astype(BF16), b, s)
    o, wgl, wdb = _attention(q, k, v, w_gate_up, w_down)

    dsk = jnp.repeat(d_skip, SSM_HEAD_DIM)[None, :]
    wr = _pad_cols(w_router, LANES)
    wr1 = wr.astype(BF16)
    wr2 = (wr - wr1.astype(F32)).astype(BF16)
    br = _pad_cols(b_router[None, :], LANES)
    h1, xn2, ri, rg, cnt = _out_proj(
        x2, y_f.reshape(n, D_SSM), y_b.reshape(n, D_SSM), xc.reshape(n, CONV_DIM), p2, o.reshape(n, MLA_HEADS * V_DIM),
        dsk, ssm_norm_w.reshape(1, D_SSM), attn_norm_w.reshape(1, MLA_HEADS * V_DIM), w_out.astype(BF16),
        norm2_w.reshape(1, D_MODEL), wr1, wr2, br)

    counts = cnt[0, :N_EXPERTS].astype(jnp.int32)
    nblk = (counts + MOE_BLK - 1) // MOE_BLK
    blk_end = jnp.cumsum(nblk)
    row_start = (blk_end - nblk) * MOE_BLK
    n_blocks = (n * TOP_K) // MOE_BLK + N_EXPERTS
    n_rows = n_blocks * MOE_BLK
    ids = ri[:, :TOP_K]
    dest = (row_start[ids] + ri[:, TOP_K:2 * TOP_K]).reshape(n * TOP_K)
    block_ids = jnp.arange(n_blocks, dtype=jnp.int32)
    block_expert = jnp.minimum(jnp.sum((blk_end[None, :] <= block_ids[:, None]).astype(jnp.int32), axis=1),
                               N_EXPERTS - 1)
    n_used = blk_end[-1:].astype(jnp.int32)

    assign = (jnp.arange(TOP_K, dtype=jnp.int32)[None, :] * n + jnp.arange(n, dtype=jnp.int32)[:, None])
    inv = jnp.full((n_rows,), -1, jnp.int32).at[dest].set(assign.reshape(n * TOP_K), unique_indices=True)
    is_pad = inv < 0
    pad_pos = n * TOP_K + jnp.cumsum(is_pad.astype(jnp.int32)) - 1
    row_pos = jnp.where(is_pad, pad_pos, inv).reshape(n_blocks, 1, MOE_BLK)
    row_tok = jnp.where(is_pad, 0, inv % n).reshape(n_blocks, 1, MOE_BLK)

    bgl = b_gate_up.reshape(N_EXPERTS, D_EXPERT // LANES, LANES, 2).swapaxes(2, 3).reshape(N_EXPERTS, 1, 2 * D_EXPERT)
    y_rows = _moe(block_expert, n_used, row_tok, row_pos, xn2, wgl, wdb, bgl, b_down[:, None, :])

    out = _combine(y_rows, h1, rg, final_norm_w.reshape(1, D_MODEL))
    return out.reshape(b, s, D_MODEL)


def kernel(x, positions, norm1_w, w_in, conv_w, conv_b, dt_bias_f, dt_bias_b, a_log_f, a_log_b, d_skip,
           ssm_norm_w, q_a_norm_w, w_q_b, kv_a_norm_w, w_kv_b, attn_norm_w, w_out, norm2_w, w_router, b_router,
           w_gate_up, b_gate_up, w_down, b_down, final_norm_w):
    depth = norm1_w.shape[0]
    assert depth == 1, "the final norm is fused into the single layer's combine step"
    return _layer(x, positions, norm1_w[0], w_in[0], conv_w[0], conv_b[0], dt_bias_f[0], dt_bias_b[0],
                  a_log_f[0], a_log_b[0], d_skip[0], ssm_norm_w[0], q_a_norm_w[0], w_q_b[0], kv_a_norm_w[0],
                  w_kv_b[0], attn_norm_w[0], w_out[0], norm2_w[0], w_router[0], b_router[0], w_gate_up[0],
                  b_gate_up[0], w_down[0], b_down[0], final_norm_w)
```

```python
import math

import jax
import jax.numpy as jnp
from jax import lax
from jax.experimental import pallas as pl
from jax.experimental.pallas import tpu as pltpu

F32 = jnp.float32
BF16 = jnp.bfloat16

D_MODEL = 2048
D_SSM = 1024
SSM_HEAD_DIM = 64
SSM_HEADS = 16
SSM_GROUPS = 2
D_STATE = 128
D_CONV = 5
CONV_DIM = D_SSM + 2 * SSM_GROUPS * D_STATE
CHUNK = 128
MLA_HEADS = 8
QK_NOPE = 128
QK_ROPE = 64
V_DIM = 128
Q_LORA = 768
KV_LORA = 512
ROPE_THETA = 10000.0
N_EXPERTS = 32
TOP_K = 4
D_EXPERT = 2048
SWIGLU_ALPHA = 1.702
SWIGLU_LIMIT = 7.0
RMS_EPS = 1e-6
LOG2_E = 1.0 / math.log(2.0)

LANES = 128
QK_PAD = 256
P_COLS = D_SSM + CONV_DIM + KV_LORA + Q_LORA
MISC_COLS = 3 * LANES
VMEM_LIMIT = 56 * 1024 * 1024

TM_IN = 512
TM_PREP = 512
TQ = 256
HEADS_PER_STEP = 2
TM_OUT = 512
MOE_BLK = 512
MOE_TC = 1024
TD = 256


def _cparams(*sem):
    return pltpu.CompilerParams(dimension_semantics=sem, vmem_limit_bytes=VMEM_LIMIT)


def _silu(x):
    return x / (1.0 + jnp.exp(-x))


def _softplus(x):
    return jnp.maximum(x, 0.0) + jnp.log(1.0 + jnp.exp(-jnp.abs(x)))


def _split3(x):
    x1 = x.astype(BF16)
    r = x - x1.astype(F32)
    x2 = r.astype(BF16)
    x3 = (r - x2.astype(F32)).astype(BF16)
    return x1, x2, x3


def _dot(a, b):
    return jnp.dot(a, b, preferred_element_type=F32)


def _dot_nt(a, b):
    return lax.dot_general(a, b, (((1,), (1,)), ((), ())), preferred_element_type=F32)


def _dot_exact_lhs(x, m, passes):
    parts = _split3(x)[:passes]
    acc = _dot(parts[0], m)
    for p in parts[1:]:
        acc = acc + _dot(p, m)
    return acc


def _in_proj_kernel(x_ref, nw_ref, w_ref, wm_ref, p_ref, m_ref):
    x = x_ref[...]
    ms = jnp.mean(x * x, axis=-1, keepdims=True)
    xn = (x * lax.rsqrt(ms + RMS_EPS) * nw_ref[...]).astype(BF16)
    m_ref[...] = _dot(xn, wm_ref[...])
    p_ref[...] = _dot(xn, w_ref[...]).astype(BF16)


def _resident(shape):
    return pl.BlockSpec(shape, lambda *_: (0, 0), pipeline_mode=pl.Buffered(1))


def _in_proj(x2, norm_w, w_main, w_misc):
    n = x2.shape[0]
    return pl.pallas_call(
        _in_proj_kernel,
        grid=(n // TM_IN,),
        in_specs=[
            pl.BlockSpec((TM_IN, D_MODEL), lambda i: (i, 0)),
            _resident((1, D_MODEL)),
            _resident((D_MODEL, P_COLS)),
            _resident((D_MODEL, MISC_COLS)),
        ],
        out_specs=[
            pl.BlockSpec((TM_IN, P_COLS), lambda i: (i, 0)),
            pl.BlockSpec((TM_IN, MISC_COLS), lambda i: (i, 0)),
        ],
        out_shape=[
            jax.ShapeDtypeStruct((n, P_COLS), BF16),
            jax.ShapeDtypeStruct((n, MISC_COLS), F32),
        ],
        compiler_params=_cparams("parallel"),
        name="in_proj",
    )(x2, norm_w, w_main, w_misc)


def _conv_kernel(u_ref, w_ref, b_ref, o_ref):
    u = u_ref[0].astype(F32)
    s = u.shape[0]
    row = lax.broadcasted_iota(jnp.int32, u.shape, 0)
    pad = D_CONV // 2
    acc = u * w_ref[pad:pad + 1, :] + b_ref[...]
    for k in range(D_CONV):
        off = k - pad
        if off == 0:
            continue
        shifted = pltpu.roll(u, (-off) % s, 0)
        valid = (row + off >= 0) & (row + off < s)
        acc = acc + jnp.where(valid, shifted, 0.0) * w_ref[k:k + 1, :]
    o_ref[0] = _silu(acc).astype(BF16)


def _conv(p3, conv_w, conv_b):
    b, s, _ = p3.shape
    first = D_SSM // LANES
    return pl.pallas_call(
        _conv_kernel,
        grid=(b, CONV_DIM // LANES),
        in_specs=[
            pl.BlockSpec((1, s, LANES), lambda i, c: (i, 0, first + c)),
            pl.BlockSpec((D_CONV, LANES), lambda i, c: (0, c)),
            pl.BlockSpec((1, LANES), lambda i, c: (0, c)),
        ],
        out_specs=pl.BlockSpec((1, s, LANES), lambda i, c: (i, 0, c)),
        out_shape=jax.ShapeDtypeStruct((b, s, CONV_DIM), BF16),
        compiler_params=_cparams("parallel", "parallel"),
        name="conv",
    )(p3, conv_w, conv_b)


def _ssd_direction(xc, dt_e, cs, cs_e, tot_row, state_ref, d, reverse):
    L = xc.shape[0]
    gw = D_SSM // SSM_GROUPS
    x = xc[:, :D_SSM].astype(F32)
    xdt = x * dt_e
    xdt_b = xdt.astype(BF16)
    tot_e = cs_e[tot_row:tot_row + 1, :]
    decay_out = jnp.exp(cs_e)
    xw = (xdt * jnp.exp(tot_e - cs_e)).astype(BF16)
    chunk_decay = jnp.exp(tot_e)

    cs_t = cs.T
    li = lax.broadcasted_iota(jnp.int32, (L, L), 0)
    si = lax.broadcasted_iota(jnp.int32, (L, L), 1)
    keep = (si >= li) if reverse else (si <= li)
    lane = lax.broadcasted_iota(jnp.int32, (L, LANES), 1)

    ys = []
    for g in range(SSM_GROUPS):
        bm = xc[:, D_SSM + g * D_STATE:D_SSM + (g + 1) * D_STATE]
        cm = xc[:, D_SSM + SSM_GROUPS * D_STATE + g * D_STATE:D_SSM + SSM_GROUPS * D_STATE + (g + 1) * D_STATE]
        cb = _dot_nt(cm, bm)
        st = state_ref[d, g]
        y_off = _dot(cm, st.astype(BF16)) * decay_out[:, g * gw:(g + 1) * gw]
        bm_t = bm.astype(F32).T.astype(BF16)
        state_ref[d, g] = st * chunk_decay[:, g * gw:(g + 1) * gw] + _dot(bm_t, xw[:, g * gw:(g + 1) * gw])

        pieces = []
        heads_per_group = SSM_HEADS // SSM_GROUPS
        for j in range(heads_per_group // 2):
            scores = []
            for hh in range(2):
                h = g * heads_per_group + 2 * j + hh
                hl = SSM_HEADS * d + h
                col = jnp.broadcast_to(cs[:, hl:hl + 1], (L, L))
                rowv = jnp.broadcast_to(cs_t[hl:hl + 1, :], (L, L))
                dec = jnp.exp(jnp.where(keep, col - rowv, -1e30))
                scores.append((cb * dec).astype(BF16))
            lhs = jnp.concatenate(scores, axis=1)
            c0 = g * gw + j * LANES
            xp = xdt_b[:, c0:c0 + LANES]
            zero = jnp.zeros_like(xp)
            rhs = jnp.concatenate([jnp.where(lane < SSM_HEAD_DIM, xp, zero),
                                   jnp.where(lane >= SSM_HEAD_DIM, xp, zero)], axis=0)
            pieces.append(_dot(lhs, rhs))
        ys.append(jnp.concatenate(pieces, axis=1) + y_off)
    return jnp.concatenate(ys, axis=1)


def _ssd_kernel(xf_ref, xb_ref, mf_ref, mb_ref, par_ref, yf_ref, yb_ref, state_ref):
    @pl.when(pl.program_id(1) == 0)
    def _():
        state_ref[...] = jnp.zeros_like(state_ref)

    L = CHUNK
    par = par_ref[...]
    lane = lax.broadcasted_iota(jnp.int32, (1, LANES), 1)
    active = lane < 2 * SSM_HEADS
    a_coef = jnp.where(active, -jnp.exp(par[1:2, :]), 0.0)
    r = lax.broadcasted_iota(jnp.int32, (L, L), 0)
    c = lax.broadcasted_iota(jnp.int32, (L, L), 1)
    tri_lo = jnp.where(c <= r, 1.0, 0.0).astype(BF16)
    tri_up = jnp.where(c >= r, 1.0, 0.0).astype(BF16)
    er = lax.broadcasted_iota(jnp.int32, (LANES, D_SSM), 0)
    ec = lax.broadcasted_iota(jnp.int32, (LANES, D_SSM), 1)

    for d, (x_ref, m_ref, y_ref) in enumerate(((xf_ref, mf_ref, yf_ref), (xb_ref, mb_ref, yb_ref))):
        reverse = d == 1
        expand = jnp.where(er == ec // SSM_HEAD_DIM + SSM_HEADS * d, 1.0, 0.0).astype(BF16)
        dt = jnp.where(active, _softplus(m_ref[0] + par[0:1, :]), 0.0)
        a = dt * a_coef
        a1, a2, a3 = _split3(a)
        tri = tri_up if reverse else tri_lo
        cs = _dot(tri, a1) + _dot(tri, a2) + _dot(tri, a3)
        dt_e = _dot_exact_lhs(dt, expand, 2)
        cs_e = _dot_exact_lhs(cs, expand, 3)
        y = _ssd_direction(x_ref[0], dt_e, cs, cs_e, 0 if reverse else L - 1, state_ref, d, reverse)
        y_ref[0] = y


def _ssd(xc, misc3, par):
    b, s, _ = xc.shape
    nc = s // CHUNK
    dtb = MISC_COLS // LANES - 1
    gw = D_SSM // SSM_GROUPS
    return pl.pallas_call(
        _ssd_kernel,
        grid=(b, nc),
        in_specs=[
            pl.BlockSpec((1, CHUNK, CONV_DIM), lambda i, c: (i, c, 0)),
            pl.BlockSpec((1, CHUNK, CONV_DIM), lambda i, c: (i, nc - 1 - c, 0)),
            pl.BlockSpec((1, CHUNK, LANES), lambda i, c: (i, c, dtb)),
            pl.BlockSpec((1, CHUNK, LANES), lambda i, c: (i, nc - 1 - c, dtb)),
            pl.BlockSpec((8, LANES), lambda i, c: (0, 0)),
        ],
        out_specs=[
            pl.BlockSpec((1, CHUNK, D_SSM), lambda i, c: (i, c, 0)),
            pl.BlockSpec((1, CHUNK, D_SSM), lambda i, c: (i, nc - 1 - c, 0)),
        ],
        out_shape=[jax.ShapeDtypeStruct((b, s, D_SSM), F32)] * 2,
        scratch_shapes=[pltpu.VMEM((2, SSM_GROUPS, D_STATE, gw), F32)],
        compiler_params=_cparams("parallel", "arbitrary"),
        name="ssd",
    )(xc, xc, misc3, misc3, par)


def _rms(x, w):
    ms = jnp.mean(x * x, axis=-1, keepdims=True)
    return x * lax.rsqrt(ms + RMS_EPS) * w


def _mla_prep_kernel(qa_ref, kva_ref, misc_ref, pos_ref, freq_ref, qnw_ref, kvnw_ref,
                     wq_ref, wqr_ref, wkv_ref, q_ref, k_ref, v_ref):
    scale = (QK_NOPE + QK_ROPE) ** -0.5 * LOG2_E
    qn = _rms(qa_ref[...].astype(F32), qnw_ref[...]).astype(BF16)
    kvn = _rms(kva_ref[...].astype(F32), kvnw_ref[...]).astype(BF16)
    ang = pos_ref[...].astype(F32) * freq_ref[...]
    cos = jnp.cos(ang)
    sin = jnp.sin(ang)
    qm = _dot(qn, wq_ref[...])
    qr = _dot(qn, wqr_ref[...])
    kv = _dot(kvn, wkv_ref[...])
    misc = misc_ref[...]
    k_rope = (misc[:, :LANES] * cos + misc[:, LANES:2 * LANES] * sin).astype(BF16)
    for h in range(MLA_HEADS):
        q_ref[0, h, :, :LANES] = (qm[:, h * QK_PAD:h * QK_PAD + LANES] * scale).astype(BF16)
        q_rope = qm[:, h * QK_PAD + LANES:(h + 1) * QK_PAD] * cos + qr[:, h * LANES:(h + 1) * LANES] * sin
        q_ref[0, h, :, LANES:] = (q_rope * scale).astype(BF16)
        k_ref[0, h, :, :LANES] = kv[:, h * 2 * LANES:h * 2 * LANES + LANES].astype(BF16)
        k_ref[0, h, :, LANES:] = k_rope
        v_ref[0, h] = kv[:, h * 2 * LANES + LANES:(h + 1) * 2 * LANES].astype(BF16)


def _mla_prep(p2, misc2, pos2, freq, qnw, kvnw, wq, wqr, wkv, b, s):
    n = p2.shape[0]
    spb = s // TM_PREP
    kv_blk = (D_SSM + CONV_DIM) // KV_LORA
    qa_blk = (D_SSM + CONV_DIM + KV_LORA) // Q_LORA
    out_idx = lambda i: (i // spb, 0, i % spb, 0)
    return pl.pallas_call(
        _mla_prep_kernel,
        grid=(n // TM_PREP,),
        in_specs=[
            pl.BlockSpec((TM_PREP, Q_LORA), lambda i: (i, qa_blk)),
            pl.BlockSpec((TM_PREP, KV_LORA), lambda i: (i, kv_blk)),
            pl.BlockSpec((TM_PREP, 2 * LANES), lambda i: (i, 0)),
            pl.BlockSpec((TM_PREP, 1), lambda i: (i, 0)),
            pl.BlockSpec((1, LANES), lambda i: (0, 0)),
            pl.BlockSpec((1, Q_LORA), lambda i: (0, 0)),
            pl.BlockSpec((1, KV_LORA), lambda i: (0, 0)),
            pl.BlockSpec((Q_LORA, MLA_HEADS * QK_PAD), lambda i: (0, 0)),
            pl.BlockSpec((Q_LORA, MLA_HEADS * LANES), lambda i: (0, 0)),
            pl.BlockSpec((KV_LORA, MLA_HEADS * 2 * LANES), lambda i: (0, 0)),
        ],
        out_specs=[
            pl.BlockSpec((1, MLA_HEADS, TM_PREP, QK_PAD), out_idx),
            pl.BlockSpec((1, MLA_HEADS, TM_PREP, QK_PAD), out_idx),
            pl.BlockSpec((1, MLA_HEADS, TM_PREP, V_DIM), out_idx),
        ],
        out_shape=[
            jax.ShapeDtypeStruct((b, MLA_HEADS, s, QK_PAD), BF16),
            jax.ShapeDtypeStruct((b, MLA_HEADS, s, QK_PAD), BF16),
            jax.ShapeDtypeStruct((b, MLA_HEADS, s, V_DIM), BF16),
        ],
        compiler_params=_cparams("parallel"),
        name="mla_prep",
    )(p2, p2, misc2, pos2, freq, qnw, kvnw, wq, wqr, wkv)


def _regroup_gate_lin(w_ref, o_ref):
    k = lax.broadcasted_iota(jnp.int32, (2 * LANES, 2 * LANES), 0)
    c = lax.broadcasted_iota(jnp.int32, (2 * LANES, 2 * LANES), 1)
    src = jnp.where(c < LANES, 2 * c, 2 * (c - LANES) + 1)
    perm = jnp.where(k == src, 1.0, 0.0).astype(BF16)
    for j in range(w_ref.shape[1] // (2 * LANES)):
        cols = slice(2 * LANES * j, 2 * LANES * (j + 1))
        o_ref[:, cols] = _dot(w_ref[:, cols].astype(BF16), perm).astype(BF16)


def _attn_kernel(q_ref, k_ref, v_ref, wgu_ref, wd_ref, o_ref, wgl_ref, wdb_ref):
    for hh in range(HEADS_PER_STEP):
        s = _dot_nt(q_ref[0, hh], k_ref[0, hh])
        m = jnp.max(s, axis=-1, keepdims=True)
        p = jnp.exp2(s - m)
        l = jnp.sum(p, axis=-1, keepdims=True)
        o = _dot(p.astype(BF16), v_ref[0, hh])
        o_ref[0, :, hh * V_DIM:(hh + 1) * V_DIM] = (o / l).astype(BF16)
    _regroup_gate_lin(wgu_ref, wgl_ref)
    wdb_ref[...] = wd_ref[...].astype(BF16)


def _attention(q, k, v, w_gate_up, w_down):
    b, h, s, _ = q.shape
    nq = s // TQ
    hp = h // HEADS_PER_STEP
    steps = b * hp * nq
    e, d, f2 = w_gate_up.shape
    f = w_down.shape[1]
    assert (e * d) % steps == 0 and (e * f) % steps == 0
    rows_gu, rows_d = e * d // steps, e * f // steps
    assert rows_gu % 16 == 0 and rows_d % 16 == 0 and rows_gu <= 512 and rows_d <= 512
    step = lambda i, j, t: ((i * hp + j) * nq + t, 0)
    o, wgl, wdb = pl.pallas_call(
        _attn_kernel,
        grid=(b, hp, nq),
        in_specs=[
            pl.BlockSpec((1, HEADS_PER_STEP, TQ, QK_PAD), lambda i, j, t: (i, j, t, 0)),
            pl.BlockSpec((1, HEADS_PER_STEP, s, QK_PAD), lambda i, j, t: (i, j, 0, 0)),
            pl.BlockSpec((1, HEADS_PER_STEP, s, V_DIM), lambda i, j, t: (i, j, 0, 0)),
            pl.BlockSpec((rows_gu, f2), step),
            pl.BlockSpec((rows_d, d), step),
        ],
        out_specs=[
            pl.BlockSpec((1, TQ, HEADS_PER_STEP * V_DIM), lambda i, j, t: (i, t, j)),
            pl.BlockSpec((rows_gu, f2), step),
            pl.BlockSpec((rows_d, d), step),
        ],
        out_shape=[
            jax.ShapeDtypeStruct((b, s, h * V_DIM), BF16),
            jax.ShapeDtypeStruct((e * d, f2), BF16),
            jax.ShapeDtypeStruct((e * f, d), BF16),
        ],
        compiler_params=_cparams("parallel", "parallel", "arbitrary"),
        name="attn",
    )(q, k, v, w_gate_up.reshape(e * d, f2), w_down.reshape(e * f, d))
    return o, wgl.reshape(e, d, f2), wdb.reshape(e, f, d)


def _out_proj_kernel(x_ref, yf_ref, yb_ref, xs_ref, z_ref, o_ref, dsk_ref, gnw_ref, anw_ref,
                     wo_ref, n2w_ref, wr1_ref, wr2_ref, br_ref,
                     h_ref, xn_ref, ri_ref, rg_ref, cnt_ref, carry_ref):
    i = pl.program_id(0)

    @pl.when(i == 0)
    def _():
        carry_ref[...] = jnp.zeros_like(carry_ref)

    tm = x_ref.shape[0]
    y = yf_ref[...] + yb_ref[...] + xs_ref[...].astype(F32) * dsk_ref[...]
    y = y * _silu(z_ref[...].astype(F32))
    gs = D_SSM // SSM_GROUPS
    halves = []
    for g in range(SSM_GROUPS):
        yg = y[:, g * gs:(g + 1) * gs]
        halves.append(_rms(yg, gnw_ref[:, g * gs:(g + 1) * gs]).astype(BF16))
    y_att = _rms(o_ref[...].astype(F32), anw_ref[...]).astype(BF16)
    lhs = jnp.concatenate(halves + [y_att], axis=1)
    h1 = x_ref[...] + _dot(lhs, wo_ref[...])
    h_ref[...] = h1
    xn = _rms(h1, n2w_ref[...])
    xn_ref[...] = xn

    x1 = xn.astype(BF16)
    x2 = (xn - x1.astype(F32)).astype(BF16)
    logits = _dot(x1, wr1_ref[...]) + (_dot(x1, wr2_ref[...]) + _dot(x2, wr1_ref[...])) + br_ref[...]
    lane = lax.broadcasted_iota(jnp.int32, (tm, LANES), 1)
    neg = jnp.float32(-jnp.inf)
    work = jnp.where(lane < N_EXPERTS, logits, neg)
    vals, ids = [], []
    for _ in range(TOP_K):
        m = jnp.max(work, axis=-1, keepdims=True)
        idx = jnp.min(jnp.where(work == m, lane, LANES), axis=-1, keepdims=True)
        vals.append(m)
        ids.append(idx)
        work = jnp.where(lane == idx, neg, work)
    es = [jnp.exp(v - vals[0]) for v in vals]
    den = es[0] + es[1] + es[2] + es[3]
    sel = jnp.zeros((tm, LANES), F32)
    for idx in ids:
        sel = sel + jnp.where(lane == idx, 1.0, 0.0)
    r = lax.broadcasted_iota(jnp.int32, (tm, tm), 0)
    c = lax.broadcasted_iota(jnp.int32, (tm, tm), 1)
    strict = jnp.where(c < r, 1.0, 0.0).astype(BF16)
    cum = _dot(strict, sel.astype(BF16)) + carry_ref[0:1, :]
    ri = jnp.zeros((tm, LANES), jnp.int32)
    rg = jnp.zeros((tm, LANES), F32)
    for kk in range(TOP_K):
        rank = jnp.sum(jnp.where(lane == ids[kk], cum, 0.0), axis=-1, keepdims=True).astype(jnp.int32)
        ri = jnp.where(lane == kk, ids[kk], ri)
        ri = jnp.where(lane == TOP_K + kk, rank, ri)
        rg = jnp.where(lane == kk, es[kk] / den, rg)
    ri_ref[...] = ri
    rg_ref[...] = rg
    total = carry_ref[0:1, :] + jnp.sum(sel, axis=0, keepdims=True)
    carry_ref[...] = jnp.broadcast_to(total, carry_ref.shape)
    cnt_ref[...] = jnp.broadcast_to(total, cnt_ref.shape)


def _out_proj(x2, yf, yb, xc2, p2, o2, dsk, gnw, anw, wo, n2w, wr1, wr2, br):
    n = x2.shape[0]
    row = lambda i: (i, 0)
    fixed = lambda i: (0, 0)
    return pl.pallas_call(
        _out_proj_kernel,
        grid=(n // TM_OUT,),
        in_specs=[
            pl.BlockSpec((TM_OUT, D_MODEL), row),
            pl.BlockSpec((TM_OUT, D_SSM), row),
            pl.BlockSpec((TM_OUT, D_SSM), row),
            pl.BlockSpec((TM_OUT, D_SSM), row),
            pl.BlockSpec((TM_OUT, D_SSM), row),
            pl.BlockSpec((TM_OUT, MLA_HEADS * V_DIM), row),
            _resident((1, D_SSM)),
            _resident((1, D_SSM)),
            _resident((1, MLA_HEADS * V_DIM)),
            _resident((D_MODEL, D_MODEL)),
            _resident((1, D_MODEL)),
            _resident((D_MODEL, LANES)),
            _resident((D_MODEL, LANES)),
            _resident((1, LANES)),
        ],
        out_specs=[
            pl.BlockSpec((TM_OUT, D_MODEL), row),
            pl.BlockSpec((TM_OUT, D_MODEL), row),
            pl.BlockSpec((TM_OUT, LANES), row),
            pl.BlockSpec((TM_OUT, LANES), row),
            pl.BlockSpec((8, LANES), fixed),
        ],
        out_shape=[
            jax.ShapeDtypeStruct((n, D_MODEL), F32),
            jax.ShapeDtypeStruct((n, D_MODEL), F32),
            jax.ShapeDtypeStruct((n, LANES), jnp.int32),
            jax.ShapeDtypeStruct((n, LANES), F32),
            jax.ShapeDtypeStruct((8, LANES), F32),
        ],
        scratch_shapes=[pltpu.VMEM((8, LANES), F32)],
        compiler_params=_cparams("arbitrary"),
        name="out_proj",
    )(x2, yf, yb, xc2, p2, o2, dsk, gnw, anw, wo, n2w, wr1, wr2, br)


def _moe_kernel(be_ref, nu_ref, tok_ref, tokn_ref, pos_ref, x_hbm, wgl_ref, wd_ref, bgl_ref, bd_ref,
                y_hbm, xg_ref, xb_ref, acc_ref, yb_ref, gsem, ssem):
    del be_ref
    blk = pl.program_id(0)
    hc = pl.program_id(1)
    nblk = pl.num_programs(0)
    last = pl.num_programs(1) - 1
    nu = nu_ref[0]
    slot = lax.rem(blk, 2)

    def gather(src_row, s, j):
        return pltpu.make_async_copy(x_hbm.at[pl.ds(src_row, 1)], xg_ref.at[s, pl.ds(j, 1)], gsem.at[s])

    def scatter(j, dst_row):
        return pltpu.make_async_copy(yb_ref.at[pl.ds(j, 1)], y_hbm.at[pl.ds(dst_row, 1)], ssem)

    def start_gather(rows_ref, s):
        for j in range(MOE_BLK):
            gather(rows_ref[0, 0, j], s, j).start()

    def wait_gather(s):
        for _ in range(MOE_BLK):
            gather(0, s, 0).wait()

    def start_scatter():
        for j in range(MOE_BLK):
            scatter(j, pos_ref[0, 0, j]).start()

    def wait_scatter():
        for _ in range(MOE_BLK):
            scatter(0, 0).wait()

    @pl.when(blk < nu)
    def _():
        @pl.when(hc == 0)
        def _():
            @pl.when(blk == 0)
            def _():
                start_gather(tok_ref, 0)

            wait_gather(slot)
            xb_ref[...] = xg_ref[slot].astype(BF16)

            @pl.when(blk + 1 < nu)
            def _():
                start_gather(tokn_ref, 1 - slot)

        hdn = _dot(xb_ref[...], wgl_ref[0]) + bgl_ref[0]
        acts = []
        for j in range(MOE_TC // LANES):
            glu = jnp.minimum(hdn[:, 2 * LANES * j:2 * LANES * j + LANES], SWIGLU_LIMIT)
            lin = jnp.clip(hdn[:, 2 * LANES * j + LANES:2 * LANES * (j + 1)], -SWIGLU_LIMIT, SWIGLU_LIMIT)
            acts.append((glu / (1.0 + jnp.exp(-SWIGLU_ALPHA * glu)) * (lin + 1.0)).astype(BF16))
        part = _dot(jnp.concatenate(acts, axis=1), wd_ref[0])

        @pl.when(hc == 0)
        def _():
            acc_ref[...] = part + bd_ref[0]

        @pl.when((hc > 0) & (hc < last))
        def _():
            acc_ref[...] += part

        @pl.when(hc == last)
        def _():
            @pl.when(blk > 0)
            def _():
                wait_scatter()

            yb_ref[...] = acc_ref[...] + part
            start_scatter()

    @pl.when((blk >= nu) & (hc == last))
    def _():
        wait_scatter()
        yb_ref[...] = jnp.zeros_like(yb_ref)
        start_scatter()

    @pl.when((blk == nblk - 1) & (hc == last))
    def _():
        wait_scatter()


def _moe(block_expert, n_used, row_tok, row_pos, xn2, wgl, wd, bgl, bd):
    nb = row_tok.shape[0]
    nh = D_EXPERT // MOE_TC
    assert nh >= 2

    def bb(b, nu):
        return jnp.minimum(b, nu[0] - 1)

    def hh(b, h, nu):
        order = jnp.where(b % 2 == 0, h, nh - 1 - h)
        return jnp.where(b < nu[0], order, nh - 1 - (nu[0] - 1) % 2 * (nh - 1))

    smem_rows = lambda index_map: pl.BlockSpec((1, 1, MOE_BLK), index_map, memory_space=pltpu.SMEM)
    grid_spec = pltpu.PrefetchScalarGridSpec(
        num_scalar_prefetch=2,
        grid=(nb, nh),
        in_specs=[
            smem_rows(lambda b, h, be, nu: (b, 0, 0)),
            smem_rows(lambda b, h, be, nu: (jnp.minimum(b + 1, nb - 1), 0, 0)),
            smem_rows(lambda b, h, be, nu: (b, 0, 0)),
            pl.BlockSpec(memory_space=pl.ANY),
            pl.BlockSpec((1, D_MODEL, 2 * MOE_TC), lambda b, h, be, nu: (be[bb(b, nu)], 0, hh(b, h, nu))),
            pl.BlockSpec((1, MOE_TC, D_MODEL), lambda b, h, be, nu: (be[bb(b, nu)], hh(b, h, nu), 0)),
            pl.BlockSpec((1, 1, 2 * MOE_TC), lambda b, h, be, nu: (be[bb(b, nu)], 0, hh(b, h, nu))),
            pl.BlockSpec((1, 1, D_MODEL), lambda b, h, be, nu: (be[bb(b, nu)], 0, 0)),
        ],
        out_specs=pl.BlockSpec(memory_space=pl.ANY),
        scratch_shapes=[
            pltpu.VMEM((2, MOE_BLK, D_MODEL), F32),
            pltpu.VMEM((MOE_BLK, D_MODEL), BF16),
            pltpu.VMEM((MOE_BLK, D_MODEL), F32),
            pltpu.VMEM((MOE_BLK, D_MODEL), F32),
            pltpu.SemaphoreType.DMA((2,)),
            pltpu.SemaphoreType.DMA(()),
        ],
    )
    return pl.pallas_call(
        _moe_kernel,
        grid_spec=grid_spec,
        out_shape=jax.ShapeDtypeStruct((nb * MOE_BLK, D_MODEL), F32),
        compiler_params=_cparams("arbitrary", "arbitrary"),
        name="moe",
    )(block_expert, n_used, row_tok, row_tok, row_pos, xn2, wgl, wd, bgl, bd)


def _combine_kernel(y0_ref, y1_ref, y2_ref, y3_ref, h_ref, rg_ref, fw_ref, o_ref):
    rg = rg_ref[...]
    out = h_ref[...]
    for k, y_ref in enumerate((y0_ref, y1_ref, y2_ref, y3_ref)):
        out = out + y_ref[...] * rg[:, k:k + 1]
    o_ref[...] = _rms(out, fw_ref[...])


def _combine(y_rows, h1, rg, final_w):
    n = h1.shape[0]
    nt = n // TD
    row = lambda i: (i, 0)
    slot_spec = lambda k: pl.BlockSpec((TD, D_MODEL), lambda i: (k * nt + i, 0))
    return pl.pallas_call(
        _combine_kernel,
        grid=(nt,),
        in_specs=[slot_spec(k) for k in range(TOP_K)] + [
            pl.BlockSpec((TD, D_MODEL), row),
            pl.BlockSpec((TD, LANES), row),
            pl.BlockSpec((1, D_MODEL), lambda i: (0, 0)),
        ],
        out_specs=pl.BlockSpec((TD, D_MODEL), row),
        out_shape=jax.ShapeDtypeStruct((n, D_MODEL), F32),
        compiler_params=_cparams("parallel"),
        name="combine",
    )(y_rows, y_rows, y_rows, y_rows, h1, rg, final_w)


def _pad_cols(w, width):
    return jnp.pad(w, ((0, 0), (0, width - w.shape[1])))


def _rot_cols(w):
    half = w.shape[1] // 2
    return jnp.concatenate([-w[:, half:], w[:, :half]], axis=1)


def _layer(h, positions, norm1_w, w_in, conv_w, conv_b, dt_bias_f, dt_bias_b, a_log_f, a_log_b, d_skip,
           ssm_norm_w, q_a_norm_w, w_q_b, kv_a_norm_w, w_kv_b, attn_norm_w, w_out, norm2_w, w_router,
           b_router, w_gate_up, b_gate_up, w_down, b_down, final_norm_w):
    b, s, _ = h.shape
    n = b * s
    x2 = h.reshape(n, D_MODEL)

    o_z, o_xbc = 0, D_SSM
    o_dtf = o_xbc + CONV_DIM
    o_dtb = o_dtf + SSM_HEADS
    o_qa = o_dtb + SSM_HEADS
    o_kva = o_qa + Q_LORA
    o_kpe = o_kva + KV_LORA
    w_z = w_in[:, o_z:o_xbc]
    w_xbc = w_in[:, o_xbc:o_dtf]
    w_qa = w_in[:, o_qa:o_kva]
    w_kva = w_in[:, o_kva:o_kpe]
    w_kpe = w_in[:, o_kpe:o_kpe + QK_ROPE]
    w_dt = w_in[:, o_dtf:o_qa]
    w_main = jnp.concatenate([w_z, w_xbc, w_kva, w_qa], axis=1).astype(BF16)
    w_misc = jnp.concatenate([_pad_cols(w_kpe, LANES), _pad_cols(_rot_cols(w_kpe), LANES),
                              _pad_cols(w_dt, LANES)], axis=1).astype(BF16)

    p2, misc2 = _in_proj(x2, norm1_w.reshape(1, D_MODEL), w_main, w_misc)

    xc = _conv(p2.reshape(b, s, P_COLS), conv_w, conv_b.reshape(1, CONV_DIM))
    par = jnp.zeros((8, LANES), F32)
    par = par.at[0, :2 * SSM_HEADS].set(jnp.concatenate([dt_bias_f, dt_bias_b]))
    par = par.at[1, :2 * SSM_HEADS].set(jnp.concatenate([a_log_f, a_log_b]))
    y_f, y_b = _ssd(xc, misc2.reshape(b, s, MISC_COLS), par)

    inv_freq = ROPE_THETA ** (-jnp.arange(0, QK_ROPE, 2, dtype=F32) / QK_ROPE)
    freq = _pad_cols(jnp.concatenate([inv_freq, inv_freq])[None, :], LANES)
    wq3 = w_q_b.reshape(Q_LORA, MLA_HEADS, QK_NOPE + QK_ROPE)
    wq_rope = wq3[:, :, QK_NOPE:]
    wq = jnp.pad(wq3, ((0, 0), (0, 0), (0, QK_PAD - QK_NOPE - QK_ROPE))).reshape(Q_LORA, MLA_HEADS * QK_PAD)
    wq_rot = jnp.concatenate([-wq_rope[:, :, QK_ROPE // 2:], wq_rope[:, :, :QK_ROPE // 2]], axis=2)
    wqr = jnp.pad(wq_rot, ((0, 0), (0, 0), (0, LANES - QK_ROPE))).reshape(Q_LORA, MLA_HEADS * LANES)
    q, k, v = _mla_prep(p2, misc2, positions.reshape(n, 1), freq, q_a_norm_w.reshape(1, Q_LORA),
                        kv_a_norm_w.reshape(1, KV_LORA), wq.astype(BF16), wqr.astype(BF16),
                        w_kv_b.astype(BF16), b, s)
    o, wgl, wdb = _attention(q, k, v, w_gate_up, w_down)

    dsk = jnp.repeat(d_skip, SSM_HEAD_DIM)[None, :]
    wr = _pad_cols(w_router, LANES)
    wr1 = wr.astype(BF16)
    wr2 = (wr - wr1.astype(F32)).astype(BF16)
    br = _pad_cols(b_router[None, :], LANES)
    h1, xn2, ri, rg, cnt = _out_proj(
        x2, y_f.reshape(n, D_SSM), y_b.reshape(n, D_SSM), xc.reshape(n, CONV_DIM), p2, o.reshape(n, MLA_HEADS * V_DIM),
        dsk, ssm_norm_w.reshape(1, D_SSM), attn_norm_w.reshape(1, MLA_HEADS * V_DIM), w_out.astype(BF16),
        norm2_w.reshape(1, D_MODEL), wr1, wr2, br)

    counts = cnt[0, :N_EXPERTS].astype(jnp.int32)
    nblk = (counts + MOE_BLK - 1) // MOE_BLK
    blk_end = jnp.cumsum(nblk)
    row_start = (blk_end - nblk) * MOE_BLK
    n_blocks = (n * TOP_K) // MOE_BLK + N_EXPERTS
    n_rows = n_blocks * MOE_BLK
    ids = ri[:, :TOP_K]
    dest = (row_start[ids] + ri[:, TOP_K:2 * TOP_K]).reshape(n * TOP_K)
    block_ids = jnp.arange(n_blocks, dtype=jnp.int32)
    block_expert = jnp.minimum(jnp.sum((blk_end[None, :] <= block_ids[:, None]).astype(jnp.int32), axis=1),
                               N_EXPERTS - 1)
    n_used = blk_end[-1:].astype(jnp.int32)

    assign = (jnp.arange(TOP_K, dtype=jnp.int32)[None, :] * n + jnp.arange(n, dtype=jnp.int32)[:, None])
    inv = jnp.full((n_rows,), -1, jnp.int32).at[dest].set(assign.reshape(n * TOP_K), unique_indices=True)
    is_pad = inv < 0
    pad_pos = n * TOP_K + jnp.cumsum(is_pad.astype(jnp.int32)) - 1
    row_pos = jnp.where(is_pad, pad_pos, inv).reshape(n_blocks, 1, MOE_BLK)
    row_tok = jnp.where(is_pad, 0, inv % n).reshape(n_blocks, 1, MOE_BLK)

    bgl = b_gate_up.reshape(N_EXPERTS, D_EXPERT // LANES, LANES, 2).swapaxes(2, 3).reshape(N_EXPERTS, 1, 2 * D_EXPERT)
    y_rows = _moe(block_expert, n_used, row_tok, row_pos, xn2, wgl, wdb, bgl, b_down[:, None, :])

    out = _combine(y_rows, h1, rg, final_norm_w.reshape(1, D_MODEL))
    return out.reshape(b, s, D_MODEL)


def kernel(x, positions, norm1_w, w_in, conv_w, conv_b, dt_bias_f, dt_bias_b, a_log_f, a_log_b, d_skip,
           ssm_norm_w, q_a_norm_w, w_q_b, kv_a_norm_w, w_kv_b, attn_norm_w, w_out, norm2_w, w_router, b_router,
           w_gate_up, b_gate_up, w_down, b_down, final_norm_w):
    depth = norm1_w.shape[0]
    assert depth == 1, "the final norm is fused into the single layer's combine step"
    return _layer(x, positions, norm1_w[0], w_in[0], conv_w[0], conv_b[0], dt_bias_f[0], dt_bias_b[0],
                  a_log_f[0], a_log_b[0], d_skip[0], ssm_norm_w[0], q_a_norm_w[0], w_q_b[0], kv_a_norm_w[0],
                  w_kv_b[0], attn_norm_w[0], w_out[0], norm2_w[0], w_router[0], b_router[0], w_gate_up[0],
                  b_gate_up[0], w_down[0], b_down[0], final_norm_w)
```
